```python
import math
import jax, jax.numpy as jnp
from jax import lax
import numpy as np

D_MODEL = 1024
BATCH = 8
SEQ = 2048
DEPTH = 2
DEC_BATCH = 128
DEC_SEQ = 1
PAST_LEN = 16384
PAGE_SIZE = 128

N_MIXERS = 2
N_GDN = (DEPTH + 1) // 2
N_SSD = DEPTH // 2

CONV_W = 4
CHUNK = 64

GDN_HEADS = 8
GDN_DK = 128
GDN_DV = 128
GDN_QK = GDN_HEADS * GDN_DK
GDN_VD = GDN_HEADS * GDN_DV
GDN_CONV_DIM = 2 * GDN_QK + GDN_VD
GDN_IN = GDN_CONV_DIM + GDN_VD + 2 * GDN_HEADS

SSD_INNER = 2 * D_MODEL
SSD_HEADDIM = 64
SSD_HEADS = SSD_INNER // SSD_HEADDIM
SSD_GROUPS = 4
SSD_STATE = 128
SSD_HPG = SSD_HEADS // SSD_GROUPS
SSD_CONV_DIM = SSD_INNER + 2 * SSD_GROUPS * SSD_STATE
SSD_IN = SSD_INNER + SSD_CONV_DIM + SSD_HEADS

D_FF = 2816
FFN_CONV_W = 3

DN_ALPHA = (2 * DEPTH) ** 0.25
DN_BETA = (8 * DEPTH) ** -0.25
LN_EPS = 1e-5
RMS_EPS = 1e-6

kernel_name = "hybrid_gdn_mamba2_convffn_deepnorm_step"


def _layer_norm(x, g, b):
    xf = x.astype(jnp.float32)
    mu = jnp.mean(xf, -1, keepdims=True)
    var = jnp.mean(jnp.square(xf - mu), -1, keepdims=True)
    return ((xf - mu) * lax.rsqrt(var + LN_EPS) * g + b).astype(x.dtype)


def _rmsnorm(t, w):
    return t * lax.rsqrt(jnp.mean(jnp.square(t), -1, keepdims=True) + RMS_EPS) * w


def _l2norm(t):
    return t * lax.rsqrt(jnp.sum(jnp.square(t), -1, keepdims=True) + 1e-6)


def _causal_dwconv(x, buf, w, b):
    xx = jnp.concatenate([buf.astype(x.dtype), x], axis=1)
    seq_len, width = x.shape[1], w.shape[0]
    y = b + xx[:, 0:seq_len] * w[0]
    for k in range(1, width):
        y = y + xx[:, k:k + seq_len] * w[k]
    return y, xx[:, -(width - 1):]


def _to_chunks(a, csz):
    seq_len = a.shape[1]
    n = -(-seq_len // csz)
    a = jnp.pad(a, [(0, 0), (0, n * csz - seq_len)] + [(0, 0)] * (a.ndim - 2))
    return a.reshape(a.shape[0], n, csz, *a.shape[2:])


def _gated_delta_rule(q, k, v, beta, g, s0):
    seq_len = q.shape[1]
    csz = min(CHUNK, seq_len)
    qc, kc, vc, bc, gc = (_to_chunks(t, csz) for t in (q, k, v, beta, g))
    gcum = jnp.cumsum(gc, axis=2)
    gcum_h = jnp.swapaxes(gcum, 2, 3)
    incl = jnp.tril(jnp.ones((csz, csz), dtype=bool))
    strict = jnp.tril(jnp.ones((csz, csz), dtype=bool), -1)
    gamma = jnp.exp(jnp.where(incl, gcum_h[..., :, None] - gcum_h[..., None, :], -jnp.inf))
    kb = kc * bc[..., None]
    kk = jnp.einsum('bnihd,bnjhd->bnhij', kb, kc) * gamma
    eye = jnp.eye(csz, dtype=kk.dtype)
    a_mat = eye + jnp.where(strict, kk, 0.0)
    t_mat = lax.linalg.triangular_solve(a_mat, jnp.broadcast_to(eye, a_mat.shape),
                                        left_side=True, lower=True)
    u = jnp.einsum('bnhij,bnjhd->bnihd', t_mat, vc * bc[..., None])
    w = jnp.einsum('bnhij,bnjhd->bnihd', t_mat, kb * jnp.exp(gcum)[..., None])
    qk = jnp.einsum('bnihd,bnjhd->bnhij', qc, kc) * gamma
    q_dec = qc * jnp.exp(gcum)[..., None]
    g_last = gcum[:, :, -1]
    k_dec = kc * jnp.exp(g_last[:, :, None] - gcum)[..., None]

    def step(s, inp):
        u_n, w_n, qk_n, qd_n, kd_n, gl_n = inp
        v_new = u_n - jnp.einsum('bihk,bhkv->bihv', w_n, s)
        o_n = jnp.einsum('bihk,bhkv->bihv', qd_n, s) + jnp.einsum('bhij,bjhv->bihv', qk_n, v_new)
        s = s * jnp.exp(gl_n)[..., None, None] + jnp.einsum('bihk,bihv->bhkv', kd_n, v_new)
        return s, o_n

    xs = tuple(jnp.moveaxis(t, 1, 0) for t in (u, w, qk, q_dec, k_dec, g_last))
    s_fin, o = lax.scan(step, s0, xs)
    o = jnp.moveaxis(o, 0, 1)
    o = o.reshape(o.shape[0], -1, GDN_HEADS, GDN_DV)[:, :seq_len]
    return o, s_fin


def _ssd_scan(x, dt, a, bm, cm, s0):
    bsz, seq_len = x.shape[0], x.shape[1]
    csz = min(CHUNK, seq_len)
    xc = _to_chunks(x * dt[..., None], csz)
    ac = _to_chunks(dt * a, csz)
    bc = _to_chunks(bm, csz)
    cc = _to_chunks(cm, csz)
    n = xc.shape[1]
    acum = jnp.cumsum(ac, axis=2)
    acum_h = jnp.swapaxes(acum, 2, 3)
    incl = jnp.tril(jnp.ones((csz, csz), dtype=bool))
    seg = jnp.exp(jnp.where(incl, acum_h[..., :, None] - acum_h[..., None, :], -jnp.inf))
    cb = jnp.einsum('bnigs,bnjgs->bngij', cc, bc)
    scores = seg.reshape(bsz, n, SSD_GROUPS, SSD_HPG, csz, csz) * cb[:, :, :, None]
    xg = xc.reshape(bsz, n, csz, SSD_GROUPS, SSD_HPG, SSD_HEADDIM)
    y_diag = jnp.einsum('bnghij,bnjghp->bnighp', scores, xg).reshape(bsz, n, csz, SSD_HEADS, SSD_HEADDIM)
    a_last = acum[:, :, -1]
    decay_to_end = jnp.exp(a_last[:, :, None] - acum)
    xdec = (xc * decay_to_end[..., None]).reshape(bsz, n, csz, SSD_GROUPS, SSD_HPG, SSD_HEADDIM)
    chunk_states = jnp.einsum('bnjgs,bnjghp->bnghps', bc, xdec).reshape(
        bsz, n, SSD_HEADS, SSD_HEADDIM, SSD_STATE)

    def step(s, inp):
        c_n, acum_n, cs_n, al_n = inp
        sg = s.reshape(bsz, SSD_GROUPS, SSD_HPG, SSD_HEADDIM, SSD_STATE)
        y_off = jnp.einsum('bigs,bghps->bighp', c_n, sg).reshape(
            bsz, csz, SSD_HEADS, SSD_HEADDIM) * jnp.exp(acum_n)[..., None]
        s = s * jnp.exp(al_n)[..., None, None] + cs_n
        return s, y_off

    xs = tuple(jnp.moveaxis(t, 1, 0) for t in (cc, acum, chunk_states, a_last))
    s_fin, y_off = lax.scan(step, s0, xs)
    y = y_diag + jnp.moveaxis(y_off, 0, 1)
    y = y.reshape(bsz, -1, SSD_HEADS, SSD_HEADDIM)[:, :seq_len]
    return y, s_fin


def _gdn_mixer(x, conv_buf, s0, w_in, conv_w, conv_b, a_log, dt_bias, norm_w, w_out):
    bsz, seq_len, _ = x.shape
    proj = x @ w_in
    qkv = proj[..., :GDN_CONV_DIM]
    z = proj[..., GDN_CONV_DIM:GDN_CONV_DIM + GDN_VD]
    b_raw = proj[..., GDN_CONV_DIM + GDN_VD:GDN_CONV_DIM + GDN_VD + GDN_HEADS]
    a_raw = proj[..., GDN_CONV_DIM + GDN_VD + GDN_HEADS:]
    qkv, new_buf = _causal_dwconv(qkv, conv_buf, conv_w, conv_b)
    qkv = jax.nn.silu(qkv).astype(jnp.float32)
    q = _l2norm(qkv[..., :GDN_QK].reshape(bsz, seq_len, GDN_HEADS, GDN_DK)) * (GDN_DK ** -0.5)
    k = _l2norm(qkv[..., GDN_QK:2 * GDN_QK].reshape(bsz, seq_len, GDN_HEADS, GDN_DK))
    v = qkv[..., 2 * GDN_QK:].reshape(bsz, seq_len, GDN_HEADS, GDN_DV)
    beta = jax.nn.sigmoid(b_raw.astype(jnp.float32))
    g = -jnp.exp(a_log.astype(jnp.float32)) * jax.nn.softplus(a_raw.astype(jnp.float32) + dt_bias)
    o, s_fin = _gated_delta_rule(q, k, v, beta, g, s0.astype(jnp.float32))
    o = _rmsnorm(o, norm_w) * jax.nn.silu(z.astype(jnp.float32).reshape(bsz, seq_len, GDN_HEADS, GDN_DV))
    out = o.reshape(bsz, seq_len, GDN_VD).astype(x.dtype) @ w_out
    return out, new_buf, s_fin.astype(s0.dtype)


def _ssd_mixer(x, conv_buf, s0, w_in, conv_w, conv_b, a_log, dt_bias, d_skip, norm_w, w_out):
    bsz, seq_len, _ = x.shape
    gn = SSD_GROUPS * SSD_STATE
    proj = x @ w_in
    z = proj[..., :SSD_INNER]
    xbc = proj[..., SSD_INNER:SSD_INNER + SSD_CONV_DIM]
    dt_raw = proj[..., SSD_INNER + SSD_CONV_DIM:]
    xbc, new_buf = _causal_dwconv(xbc, conv_buf, conv_w, conv_b)
    xbc = jax.nn.silu(xbc).astype(jnp.float32)
    xs = xbc[..., :SSD_INNER].reshape(bsz, seq_len, SSD_HEADS, SSD_HEADDIM)
    bm = xbc[..., SSD_INNER:SSD_INNER + gn].reshape(bsz, seq_len, SSD_GROUPS, SSD_STATE)
    cm = xbc[..., SSD_INNER + gn:].reshape(bsz, seq_len, SSD_GROUPS, SSD_STATE)
    dt = jax.nn.softplus(dt_raw.astype(jnp.float32) + dt_bias)
    a = -jnp.exp(a_log.astype(jnp.float32))
    y, s_fin = _ssd_scan(xs, dt, a, bm, cm, s0.astype(jnp.float32))
    y = y + d_skip[:, None] * xs
    y = y.reshape(bsz, seq_len, SSD_INNER) * jax.nn.silu(z.astype(jnp.float32))
    y = _rmsnorm(y.reshape(bsz, seq_len, SSD_GROUPS, SSD_INNER // SSD_GROUPS),
                 norm_w.reshape(SSD_GROUPS, SSD_INNER // SSD_GROUPS)).reshape(bsz, seq_len, SSD_INNER)
    out = y.astype(x.dtype) @ w_out
    return out, new_buf, s_fin.astype(s0.dtype)


def _conv_ffn(x, buf, w_up, conv_w, conv_b, w_down):
    gv = x @ w_up
    gate, val = gv[..., :D_FF], gv[..., D_FF:]
    gate, new_buf = _causal_dwconv(gate, buf, conv_w, conv_b)
    return (jax.nn.silu(gate) * val) @ w_down, new_buf


def _trunk(x, gdn_conv, gdn_state, ssd_conv, ssd_state, ffn_conv, weights):
    (gdn_w_in, gdn_conv_w, gdn_conv_b, gdn_a_log, gdn_dt_bias, gdn_norm_w, gdn_w_out,
     ssd_w_in, ssd_conv_w, ssd_conv_b, ssd_a_log, ssd_dt_bias, ssd_d, ssd_norm_w, ssd_w_out,
     ffn_w_up, ffn_conv_w, ffn_conv_b, ffn_w_down, ln1_g, ln1_b, ln2_g, ln2_b) = weights
    gdn_conv_out, gdn_state_out, ssd_conv_out, ssd_state_out, ffn_conv_out = [], [], [], [], []
    for i in range(DEPTH):
        j = i // N_MIXERS
        if i % N_MIXERS == 0:
            h, nb, ns = _gdn_mixer(x, gdn_conv[j], gdn_state[j], gdn_w_in[j], gdn_conv_w[j], gdn_conv_b[j],
                                   gdn_a_log[j], gdn_dt_bias[j], gdn_norm_w[j], gdn_w_out[j])
            gdn_conv_out.append(nb)
            gdn_state_out.append(ns)
        else:
            h, nb, ns = _ssd_mixer(x, ssd_conv[j], ssd_state[j], ssd_w_in[j], ssd_conv_w[j], ssd_conv_b[j],
                                   ssd_a_log[j], ssd_dt_bias[j], ssd_d[j], ssd_norm_w[j], ssd_w_out[j])
            ssd_conv_out.append(nb)
            ssd_state_out.append(ns)
        x = _layer_norm(DN_ALPHA * x + h, ln1_g[i], ln1_b[i])
        f, fb = _conv_ffn(x, ffn_conv[i], ffn_w_up[i], ffn_conv_w[i], ffn_conv_b[i], ffn_w_down[i])
        ffn_conv_out.append(fb)
        x = _layer_norm(DN_ALPHA * x + f, ln2_g[i], ln2_b[i])
    return (x, jnp.stack(gdn_conv_out), jnp.stack(gdn_state_out), jnp.stack(ssd_conv_out),
            jnp.stack(ssd_state_out), jnp.stack(ffn_conv_out))


def setup_inputs(seed: int = 0) -> dict:
    key = jax.random.key(seed)
    ks = iter(jax.random.split(key, 64))

    def nrm(shape, scale):
        return jax.random.normal(next(ks), shape, jnp.float32) * scale

    def dt_bias(shape):
        dt = jnp.exp(jax.random.uniform(next(ks), shape, jnp.float32, math.log(1e-3), math.log(1e-1)))
        return dt + jnp.log(-jnp.expm1(-dt))

    def a_log(shape):
        return jnp.log(jax.random.uniform(next(ks), shape, jnp.float32, 1.0, 16.0))

    return {
        "x_prompt": nrm((BATCH, SEQ, D_MODEL), 1.0),
        "x_sample": nrm((DEC_BATCH, DEC_SEQ, D_MODEL), 1.0),
        "cache_gdn_conv": nrm((N_GDN, DEC_BATCH, CONV_W - 1, GDN_CONV_DIM), 1.0),
        "state_gdn": nrm((N_GDN, DEC_BATCH, GDN_HEADS, GDN_DK, GDN_DV), 0.5),
        "cache_ssd_conv": nrm((N_SSD, DEC_BATCH, CONV_W - 1, SSD_CONV_DIM), 1.0),
        "state_ssd": nrm((N_SSD, DEC_BATCH, SSD_HEADS, SSD_HEADDIM, SSD_STATE), 0.1),
        "cache_ffn_conv": nrm((DEPTH, DEC_BATCH, FFN_CONV_W - 1, D_FF), 1.0),
        "gdn_w_in": nrm((N_GDN, D_MODEL, GDN_IN), D_MODEL ** -0.5),
        "gdn_conv_w": nrm((N_GDN, CONV_W, GDN_CONV_DIM), CONV_W ** -0.5),
        "gdn_conv_b": nrm((N_GDN, GDN_CONV_DIM), 0.02),
        "gdn_a_log": a_log((N_GDN, GDN_HEADS)),
        "gdn_dt_bias": dt_bias((N_GDN, GDN_HEADS)),
        "gdn_norm_w": 1.0 + nrm((N_GDN, GDN_DV), 0.02),
        "gdn_w_out": nrm((N_GDN, GDN_VD, D_MODEL), DN_BETA * GDN_VD ** -0.5),
        "ssd_w_in": nrm((N_SSD, D_MODEL, SSD_IN), D_MODEL ** -0.5),
        "ssd_conv_w": nrm((N_SSD, CONV_W, SSD_CONV_DIM), CONV_W ** -0.5),
        "ssd_conv_b": nrm((N_SSD, SSD_CONV_DIM), 0.02),
        "ssd_a_log": a_log((N_SSD, SSD_HEADS)),
        "ssd_dt_bias": dt_bias((N_SSD, SSD_HEADS)),
        "ssd_d": 1.0 + nrm((N_SSD, SSD_HEADS), 0.02),
        "ssd_norm_w": 1.0 + nrm((N_SSD, SSD_INNER), 0.02),
        "ssd_w_out": nrm((N_SSD, SSD_INNER, D_MODEL), DN_BETA * SSD_INNER ** -0.5),
        "ffn_w_up": nrm((DEPTH, D_MODEL, 2 * D_FF), D_MODEL ** -0.5),
        "ffn_conv_w": nrm((DEPTH, FFN_CONV_W, D_FF), FFN_CONV_W ** -0.5),
        "ffn_conv_b": nrm((DEPTH, D_FF), 0.02),
        "ffn_w_down": nrm((DEPTH, D_FF, D_MODEL), DN_BETA * D_FF ** -0.5),
        "ln1_g": 1.0 + nrm((DEPTH, D_MODEL), 0.02),
        "ln1_b": nrm((DEPTH, D_MODEL), 0.02),
        "ln2_g": 1.0 + nrm((DEPTH, D_MODEL), 0.02),
        "ln2_b": nrm((DEPTH, D_MODEL), 0.02),
    }


def reference(x_prompt, x_sample, cache_gdn_conv, state_gdn, cache_ssd_conv, state_ssd, cache_ffn_conv,
              gdn_w_in, gdn_conv_w, gdn_conv_b, gdn_a_log, gdn_dt_bias, gdn_norm_w, gdn_w_out,
              ssd_w_in, ssd_conv_w, ssd_conv_b, ssd_a_log, ssd_dt_bias, ssd_d, ssd_norm_w, ssd_w_out,
              ffn_w_up, ffn_conv_w, ffn_conv_b, ffn_w_down, ln1_g, ln1_b, ln2_g, ln2_b):
    weights = (gdn_w_in, gdn_conv_w, gdn_conv_b, gdn_a_log, gdn_dt_bias, gdn_norm_w, gdn_w_out,
               ssd_w_in, ssd_conv_w, ssd_conv_b, ssd_a_log, ssd_dt_bias, ssd_d, ssd_norm_w, ssd_w_out,
               ffn_w_up, ffn_conv_w, ffn_conv_b, ffn_w_down, ln1_g, ln1_b, ln2_g, ln2_b)
    bp = x_prompt.shape[0]
    z_gdn_conv = jnp.zeros((cache_gdn_conv.shape[0], bp) + cache_gdn_conv.shape[2:], cache_gdn_conv.dtype)
    z_gdn_state = jnp.zeros((state_gdn.shape[0], bp) + state_gdn.shape[2:], state_gdn.dtype)
    z_ssd_conv = jnp.zeros((cache_ssd_conv.shape[0], bp) + cache_ssd_conv.shape[2:], cache_ssd_conv.dtype)
    z_ssd_state = jnp.zeros((state_ssd.shape[0], bp) + state_ssd.shape[2:], state_ssd.dtype)
    z_ffn_conv = jnp.zeros((cache_ffn_conv.shape[0], bp) + cache_ffn_conv.shape[2:], cache_ffn_conv.dtype)
    y_p, gcp, gsp, scp, ssp, fcp = _trunk(x_prompt, z_gdn_conv, z_gdn_state, z_ssd_conv, z_ssd_state,
                                          z_ffn_conv, weights)
    y_s, gcs, gss, scs, sss, fcs = _trunk(x_sample, cache_gdn_conv, state_gdn, cache_ssd_conv, state_ssd,
                                          cache_ffn_conv, weights)
    return (y_p, y_s, gcp, gcs, gsp, gss, scp, scs, ssp, sss, fcp, fcs)
```

```python
import functools

import jax
import jax.numpy as jnp
from jax import lax
from jax.experimental import pallas as pl
from jax.experimental.pallas import tpu as pltpu

F32 = jnp.float32
BF16 = jnp.bfloat16

D_MODEL = 1024
DEPTH = 2
CONV_W = 4
CHUNK = 64

GDN_HEADS = 8
GDN_DK = 128
GDN_DV = 128
GDN_QK = GDN_HEADS * GDN_DK
GDN_VD = GDN_HEADS * GDN_DV
GDN_CONV_DIM = 2 * GDN_QK + GDN_VD

SSD_INNER = 2 * D_MODEL
SSD_HEADDIM = 64
SSD_HEADS = SSD_INNER // SSD_HEADDIM
SSD_GROUPS = 4
SSD_STATE = 128
SSD_HPG = SSD_HEADS // SSD_GROUPS
SSD_GN = SSD_GROUPS * SSD_STATE
SSD_CONV_DIM = SSD_INNER + 2 * SSD_GN
SSD_GROUP_W = SSD_INNER // SSD_GROUPS

D_FF = 2816
FFN_CONV_W = 3
FFN_BLOCK = D_FF // 2

DN_ALPHA = (2 * DEPTH) ** 0.25
LN_EPS = 1e-5
RMS_EPS = 1e-6
L2_EPS = 1e-6

LANES = 128
SUBLANES = 8
VMEM_LIMIT = 56 * 1024 * 1024

NEG_BIG = -1e30


def _dot(a, b):
    return jnp.dot(a.astype(BF16), b.astype(BF16), preferred_element_type=F32)


def _dot_nt(a, b):
    return lax.dot_general(a.astype(BF16), b.astype(BF16), (((1,), (1,)), ((), ())),
                           preferred_element_type=F32)


def _dot_tn(a, b):
    return lax.dot_general(a.astype(BF16), b.astype(BF16), (((0,), (0,)), ((), ())),
                           preferred_element_type=F32)


def _split3(x):
    hi = x.astype(BF16)
    r1 = x - hi.astype(F32)
    mid = r1.astype(BF16)
    lo = (r1 - mid.astype(F32)).astype(BF16)
    return hi, mid, lo


def _dot_exact_lhs(e, x):
    hi, mid, lo = _split3(x)
    return (jnp.dot(e, hi, preferred_element_type=F32) + jnp.dot(e, mid, preferred_element_type=F32)
            + jnp.dot(e, lo, preferred_element_type=F32))


def _dot_exact_rhs(x, e):
    hi, mid, lo = _split3(x)
    return (jnp.dot(hi, e, preferred_element_type=F32) + jnp.dot(mid, e, preferred_element_type=F32)
            + jnp.dot(lo, e, preferred_element_type=F32))


def _dot3(a, b):
    a_hi = a.astype(BF16)
    a_lo = (a - a_hi.astype(F32)).astype(BF16)
    b_hi = b.astype(BF16)
    b_lo = (b - b_hi.astype(F32)).astype(BF16)
    return (jnp.dot(a_hi, b_hi, preferred_element_type=F32) + jnp.dot(a_hi, b_lo, preferred_element_type=F32)
            + jnp.dot(a_lo, b_hi, preferred_element_type=F32))


def _silu(x):
    return x * jax.nn.sigmoid(x)


def _softplus(x):
    return jnp.maximum(x, 0.0) + jnp.log1p(jnp.exp(-jnp.abs(x)))


def _layer_norm(x, g, b):
    mu = jnp.mean(x, axis=-1, keepdims=True)
    xc = x - mu
    var = jnp.mean(xc * xc, axis=-1, keepdims=True)
    return xc * lax.rsqrt(var + LN_EPS) * g + b


def _l2norm(t):
    return t * lax.rsqrt(jnp.sum(t * t, axis=-1, keepdims=True) + L2_EPS)


def _col(a, i):
    return a[:, i:i + 1]


def _neumann_inverse(m, eye):
    size = m.shape[0]
    n = -m
    p = eye + n
    k = 1
    while 2 * k < size:
        n = _dot3(n, n)
        p = p + _dot3(p, n)
        k *= 2
    return p


def _const_spec(shape):
    nd = len(shape)
    return pl.BlockSpec(shape, lambda *_: (0,) * nd)


def _params(n_grid):
    return pltpu.CompilerParams(dimension_semantics=("arbitrary",) * n_grid, vmem_limit_bytes=VMEM_LIMIT)


def _chunk_mats(tl):
    r = jnp.arange(tl)
    same = (r[:, None] // CHUNK) == (r[None, :] // CHUNK)
    tri = (same & (r[:, None] >= r[None, :])).astype(BF16)
    tot = same.astype(BF16)
    return tri, tot


def _causal_conv(xx_ref, cw_ref, cb_ref, cs, tl, width):
    base = SUBLANES - (width - 1)
    acc = cb_ref[:, cs] + xx_ref[base:base + tl, cs] * cw_ref[0:1, cs]
    for k in range(1, width):
        acc = acc + xx_ref[base + k:base + k + tl, cs] * cw_ref[k:k + 1, cs]
    return acc


def _gdn_in_kernel(x_ref, wqkv_ref, wz_ref, wba_ref, cw_ref, cb_ref, alog_ref, dtb_ref, tri_ref, tot_ref,
                   q_ref, qd_ref, k_ref, kbg_ref, kd_ref, vb_ref, z_ref, slab_ref, slabt_ref, cache_ref,
                   xx_ref, *, tl):
    l = pl.program_id(1)
    xb = x_ref[...].astype(BF16)
    lane = lax.broadcasted_iota(jnp.int32, (tl, LANES), 1)

    ba = jnp.dot(xb, wba_ref[...], preferred_element_type=F32)
    beta = jax.nn.sigmoid(ba)
    g = -jnp.exp(alog_ref[...]) * _softplus(ba + dtb_ref[...])
    g = jnp.where((lane >= GDN_HEADS) & (lane < 2 * GDN_HEADS), g, 0.0)
    gcum = _dot_exact_lhs(tri_ref[...], g)
    gtot = _dot_exact_lhs(tot_ref[...], g)
    slab = jnp.where(lane < GDN_HEADS, beta, gcum)
    slab_ref[...] = slab
    slab_t = slab.T
    for c in range(tl // CHUNK):
        slabt_ref[c] = slab_t[0:2 * GDN_HEADS, c * CHUNK:(c + 1) * CHUNK]
    eg = jnp.exp(gcum)
    ed = jnp.exp(gtot - gcum)

    z_ref[...] = jnp.dot(xb, wz_ref[...], preferred_element_type=F32)

    @pl.when(l == 0)
    def _():
        xx_ref[0:SUBLANES, :] = jnp.zeros((SUBLANES, GDN_CONV_DIM), F32)

    for j in range(3):
        cs = slice(j * GDN_QK, (j + 1) * GDN_QK)
        xx_ref[SUBLANES:SUBLANES + tl, cs] = jnp.dot(xb, wqkv_ref[:, cs], preferred_element_type=F32)
    cache_ref[...] = xx_ref[SUBLANES + tl - (CONV_W - 1):SUBLANES + tl, :]

    for j in range(3):
        cs = slice(j * GDN_QK, (j + 1) * GDN_QK)
        act = _silu(_causal_conv(xx_ref, cw_ref, cb_ref, cs, tl, CONV_W))
        for h in range(GDN_HEADS):
            hs = slice(h * GDN_DK, (h + 1) * GDN_DK)
            t = act[:, hs]
            bcol = _col(slab, h)
            egc = _col(eg, GDN_HEADS + h)
            if j == 0:
                qn = _l2norm(t) * (GDN_DK ** -0.5)
                q_ref[:, hs] = qn.astype(BF16)
                qd_ref[:, hs] = (qn * egc).astype(BF16)
            elif j == 1:
                kn = _l2norm(t)
                k_ref[:, hs] = kn.astype(BF16)
                kbg_ref[:, hs] = (kn * (bcol * egc)).astype(BF16)
                kd_ref[:, hs] = (kn * _col(ed, GDN_HEADS + h)).astype(BF16)
            else:
                vb_ref[:, hs] = t * bcol

    xx_ref[0:SUBLANES, :] = xx_ref[tl:tl + SUBLANES, :]


def _gdn_in(x, p, tl):
    bsz, seq, _ = x.shape
    nl = seq // tl
    tri, tot = _chunk_mats(tl)
    row = lambda w: pl.BlockSpec((None, tl, w), lambda b, l: (b, l, 0))
    bf = lambda w: jax.ShapeDtypeStruct((bsz, seq, w), BF16)
    ff = lambda w: jax.ShapeDtypeStruct((bsz, seq, w), F32)
    out_shape = (bf(GDN_QK), bf(GDN_QK), bf(GDN_QK), bf(GDN_QK), bf(GDN_QK), ff(GDN_VD), ff(GDN_VD), ff(LANES),
                 jax.ShapeDtypeStruct((bsz, seq // CHUNK, 2 * GDN_HEADS, CHUNK), F32),
                 jax.ShapeDtypeStruct((bsz, CONV_W - 1, GDN_CONV_DIM), F32))
    out_specs = (row(GDN_QK),) * 5 + (row(GDN_VD), row(GDN_VD), row(LANES),
                 pl.BlockSpec((None, tl // CHUNK, 2 * GDN_HEADS, CHUNK), lambda b, l: (b, l, 0, 0)),
                 pl.BlockSpec((None, CONV_W - 1, GDN_CONV_DIM), lambda b, l: (b, 0, 0)))
    consts = (p["wqkv"], p["wz"], p["wba"], p["cw"], p["cb"], p["alog"], p["dtb"], tri, tot)
    return pl.pallas_call(
        functools.partial(_gdn_in_kernel, tl=tl),
        grid=(bsz, nl),
        in_specs=[row(D_MODEL)] + [_const_spec(c.shape) for c in consts],
        out_specs=out_specs,
        out_shape=out_shape,
        scratch_shapes=[pltpu.VMEM((tl + SUBLANES, GDN_CONV_DIM), F32)],
        compiler_params=_params(2),
        name="gdn_in",
    )(x, *consts)


def _gdn_scan_kernel(q_ref, qd_ref, k_ref, kbg_ref, kd_ref, vb_ref, slab_ref, slabt_ref,
                     o_ref, sout_ref, s_ref, *, tl):
    l = pl.program_id(1)

    @pl.when(l == 0)
    def _():
        s_ref[...] = jnp.zeros(s_ref.shape, F32)

    ri = lax.broadcasted_iota(jnp.int32, (CHUNK, CHUNK), 0)
    ci = lax.broadcasted_iota(jnp.int32, (CHUNK, CHUNK), 1)
    incl = ri >= ci
    strict = ri > ci
    eye = (ri == ci).astype(F32)

    def chunk_body(c, carry):
        off = pl.multiple_of(c * CHUNK, CHUNK)
        rows = pl.ds(off, CHUNK)
        sl = slab_ref[rows, :]
        st = slabt_ref[c]
        for h in range(GDN_HEADS):
            hs = slice(h * GDN_DK, (h + 1) * GDN_DK)
            gcol = _col(sl, GDN_HEADS + h)
            bcol = _col(sl, h)
            grow = st[GDN_HEADS + h:GDN_HEADS + h + 1, :]
            gam = jnp.exp(jnp.where(incl, gcol - grow, NEG_BIG))
            kc = k_ref[rows, hs]
            m = jnp.where(strict, _dot_nt(kc, kc) * gam * bcol, 0.0)
            t_inv = _neumann_inverse(m, eye)
            s = s_ref[h]
            sb = s.astype(BF16)
            rhs = vb_ref[rows, hs] - jnp.dot(kbg_ref[rows, hs], sb, preferred_element_type=F32)
            v_new = _dot(t_inv, rhs).astype(BF16)
            qk = _dot_nt(q_ref[rows, hs], kc) * gam
            o_ref[rows, hs] = (jnp.dot(qd_ref[rows, hs], sb, preferred_element_type=F32)
                               + jnp.dot(qk.astype(BF16), v_new, preferred_element_type=F32))
            decay = jnp.exp(grow[:, CHUNK - 1:CHUNK])
            s_ref[h] = s * decay + _dot_tn(kd_ref[rows, hs], v_new)
        return carry

    lax.fori_loop(0, tl // CHUNK, chunk_body, 0)

    @pl.when(l == pl.num_programs(1) - 1)
    def _():
        sout_ref[...] = s_ref[...]


def _gdn_scan(q, qd, k, kbg, kd, vb, slab, slabt, tl):
    bsz, seq, _ = q.shape
    row = lambda w: pl.BlockSpec((None, tl, w), lambda b, l: (b, l, 0))
    return pl.pallas_call(
        functools.partial(_gdn_scan_kernel, tl=tl),
        grid=(bsz, seq // tl),
        in_specs=[row(GDN_QK)] * 5 + [row(GDN_VD), row(LANES),
                  pl.BlockSpec((None, tl // CHUNK, 2 * GDN_HEADS, CHUNK), lambda b, l: (b, l, 0, 0))],
        out_specs=(row(GDN_VD), pl.BlockSpec((None, GDN_HEADS, GDN_DK, GDN_DV), lambda b, l: (b, 0, 0, 0))),
        out_shape=(jax.ShapeDtypeStruct((bsz, seq, GDN_VD), F32),
                   jax.ShapeDtypeStruct((bsz, GDN_HEADS, GDN_DK, GDN_DV), F32)),
        scratch_shapes=[pltpu.VMEM((GDN_HEADS, GDN_DK, GDN_DV), F32)],
        compiler_params=_params(2),
        name="gdn_scan",
    )(q, qd, k, kbg, kd, vb, slab, slabt)


def _gdn_out_kernel(o_ref, z_ref, x_ref, nw_ref, wout_ref, g_ref, b_ref, y_ref):
    o = o_ref[...]
    z = z_ref[...]
    parts = []
    for h in range(GDN_HEADS):
        hs = slice(h * GDN_DV, (h + 1) * GDN_DV)
        oh = o[:, hs]
        rn = oh * lax.rsqrt(jnp.mean(oh * oh, axis=-1, keepdims=True) + RMS_EPS) * nw_ref[...]
        parts.append((rn * _silu(z[:, hs])).astype(BF16))
    mix = jnp.dot(jnp.concatenate(parts, axis=1), wout_ref[...], preferred_element_type=F32)
    y_ref[...] = _layer_norm(DN_ALPHA * x_ref[...] + mix, g_ref[...], b_ref[...])


def _gdn_out(o, z, x, p, ln_g, ln_b, tl):
    bsz, seq, _ = x.shape
    row = lambda w: pl.BlockSpec((None, tl, w), lambda b, l: (b, l, 0))
    consts = (p["nw"], p["wout"], ln_g, ln_b)
    return pl.pallas_call(
        _gdn_out_kernel,
        grid=(bsz, seq // tl),
        in_specs=[row(GDN_VD), row(GDN_VD), row(D_MODEL)] + [_const_spec(c.shape) for c in consts],
        out_specs=row(D_MODEL),
        out_shape=jax.ShapeDtypeStruct((bsz, seq, D_MODEL), F32),
        compiler_params=_params(2),
        name="gdn_out",
    )(o, z, x, *consts)


def _ffn_kernel(x_ref, wup_ref, cw_ref, cb_ref, wdown_ref, g_ref, b_ref, y_ref, cache_ref, gg_ref, *, tl):
    l = pl.program_id(1)

    @pl.when(l == 0)
    def _():
        gg_ref[0:SUBLANES, :] = jnp.zeros((SUBLANES, D_FF), F32)

    x = x_ref[...]
    xb = x.astype(BF16)
    acc = jnp.zeros((tl, D_MODEL), F32)
    for j in range(D_FF // FFN_BLOCK):
        cs = slice(j * FFN_BLOCK, (j + 1) * FFN_BLOCK)
        vs = slice(D_FF + j * FFN_BLOCK, D_FF + (j + 1) * FFN_BLOCK)
        gg_ref[SUBLANES:SUBLANES + tl, cs] = jnp.dot(xb, wup_ref[:, cs], preferred_element_type=F32)
        val = jnp.dot(xb, wup_ref[:, vs], preferred_element_type=F32)
        hid = _silu(_causal_conv(gg_ref, cw_ref, cb_ref, cs, tl, FFN_CONV_W)) * val
        acc = acc + jnp.dot(hid.astype(BF16), wdown_ref[cs, :], preferred_element_type=F32)
    cache_ref[...] = gg_ref[SUBLANES + tl - (FFN_CONV_W - 1):SUBLANES + tl, :]
    gg_ref[0:SUBLANES, :] = gg_ref[tl:tl + SUBLANES, :]
    y_ref[...] = _layer_norm(DN_ALPHA * x + acc, g_ref[...], b_ref[...])


def _ffn(x, p, ln_g, ln_b, tl):
    bsz, seq, _ = x.shape
    row = lambda w: pl.BlockSpec((None, tl, w), lambda b, l: (b, l, 0))
    consts = (p["wup"], p["cw"], p["cb"], p["wdown"], ln_g, ln_b)
    return pl.pallas_call(
        functools.partial(_ffn_kernel, tl=tl),
        grid=(bsz, seq // tl),
        in_specs=[row(D_MODEL)] + [_const_spec(c.shape) for c in consts],
        out_specs=(row(D_MODEL), pl.BlockSpec((None, FFN_CONV_W - 1, D_FF), lambda b, l: (b, 0, 0))),
        out_shape=(jax.ShapeDtypeStruct((bsz, seq, D_MODEL), F32),
                   jax.ShapeDtypeStruct((bsz, FFN_CONV_W - 1, D_FF), F32)),
        scratch_shapes=[pltpu.VMEM((tl + SUBLANES, D_FF), F32)],
        compiler_params=_params(2),
        name="ffn",
    )(x, *consts)


def _ssd_in_kernel(x_ref, wz_ref, wxbc_ref, wdt_ref, cw_ref, cb_ref, a_ref, dtb_ref, tri_ref,
                   z_ref, xs_ref, bm_ref, cm_ref, slab_ref, slabt_ref, cache_ref, xx_ref, *, tl):
    l = pl.program_id(1)
    xb = x_ref[...].astype(BF16)
    lane = lax.broadcasted_iota(jnp.int32, (tl, LANES), 1)

    dt = _softplus(jnp.dot(xb, wdt_ref[...], preferred_element_type=F32) + dtb_ref[...])
    dt = jnp.where(lane < 2 * SSD_HEADS, dt, 0.0)
    acum = _dot_exact_lhs(tri_ref[...], dt * a_ref[...])
    slab = jnp.where(lane < SSD_HEADS, dt, acum)
    slab_ref[...] = slab
    slab_t = slab.T
    for c in range(tl // CHUNK):
        slabt_ref[c] = slab_t[0:2 * SSD_HEADS, c * CHUNK:(c + 1) * CHUNK]

    z_ref[...] = jnp.dot(xb, wz_ref[...], preferred_element_type=F32)

    @pl.when(l == 0)
    def _():
        xx_ref[0:SUBLANES, :] = jnp.zeros((SUBLANES, SSD_CONV_DIM), F32)

    nblk = SSD_CONV_DIM // D_MODEL
    for j in range(nblk):
        cs = slice(j * D_MODEL, (j + 1) * D_MODEL)
        xx_ref[SUBLANES:SUBLANES + tl, cs] = jnp.dot(xb, wxbc_ref[:, cs], preferred_element_type=F32)
    cache_ref[...] = xx_ref[SUBLANES + tl - (CONV_W - 1):SUBLANES + tl, :]

    for j in range(nblk):
        cs = slice(j * D_MODEL, (j + 1) * D_MODEL)
        act = _silu(_causal_conv(xx_ref, cw_ref, cb_ref, cs, tl, CONV_W))
        if j < 2:
            xs_ref[:, cs] = act
        else:
            bm_ref[...] = act[:, 0:SSD_GN].astype(BF16)
            cm_ref[...] = act[:, SSD_GN:2 * SSD_GN].astype(BF16)

    xx_ref[0:SUBLANES, :] = xx_ref[tl:tl + SUBLANES, :]


def _ssd_in(x, p, tl):
    bsz, seq, _ = x.shape
    tri, _ = _chunk_mats(tl)
    row = lambda w: pl.BlockSpec((None, tl, w), lambda b, l: (b, l, 0))
    out_shape = (jax.ShapeDtypeStruct((bsz, seq, SSD_INNER), F32), jax.ShapeDtypeStruct((bsz, seq, SSD_INNER), F32),
                 jax.ShapeDtypeStruct((bsz, seq, SSD_GN), BF16), jax.ShapeDtypeStruct((bsz, seq, SSD_GN), BF16),
                 jax.ShapeDtypeStruct((bsz, seq, LANES), F32),
                 jax.ShapeDtypeStruct((bsz, seq // CHUNK, 2 * SSD_HEADS, CHUNK), F32),
                 jax.ShapeDtypeStruct((bsz, CONV_W - 1, SSD_CONV_DIM), F32))
    out_specs = (row(SSD_INNER), row(SSD_INNER), row(SSD_GN), row(SSD_GN), row(LANES),
                 pl.BlockSpec((None, tl // CHUNK, 2 * SSD_HEADS, CHUNK), lambda b, l: (b, l, 0, 0)),
                 pl.BlockSpec((None, CONV_W - 1, SSD_CONV_DIM), lambda b, l: (b, 0, 0)))
    consts = (p["wz"], p["wxbc"], p["wdt"], p["cw"], p["cb"], p["a"], p["dtb"], tri)
    return pl.pallas_call(
        functools.partial(_ssd_in_kernel, tl=tl),
        grid=(bsz, seq // tl),
        in_specs=[row(D_MODEL)] + [_const_spec(c.shape) for c in consts],
        out_specs=out_specs,
        out_shape=out_shape,
        scratch_shapes=[pltpu.VMEM((tl + SUBLANES, SSD_CONV_DIM), F32)],
        compiler_params=_params(2),
        name="ssd_in",
    )(x, *consts)


def _ssd_scan_kernel(xs_ref, bm_ref, cm_ref, slab_ref, slabt_ref, edt_ref, eac_ref, dskip_ref,
                     y_ref, sout_ref, st_ref, *, tl):
    l = pl.program_id(1)

    @pl.when(l == 0)
    def _():
        st_ref[...] = jnp.zeros(st_ref.shape, F32)

    ri = lax.broadcasted_iota(jnp.int32, (CHUNK, CHUNK), 0)
    ci = lax.broadcasted_iota(jnp.int32, (CHUNK, CHUNK), 1)
    incl = ri >= ci

    def chunk_body(c, carry):
        off = pl.multiple_of(c * CHUNK, CHUNK)
        rows = pl.ds(off, CHUNK)
        sl = slab_ref[rows, :]
        st = slabt_ref[c]
        dt_x = _dot_exact_rhs(sl, edt_ref[...])
        ac_x = _dot_exact_rhs(sl, eac_ref[...])
        xs = xs_ref[rows, :]
        xdt = xs * dt_x
        alast_x = ac_x[CHUNK - 1:CHUNK, :]
        xdec = (xdt * jnp.exp(alast_x - ac_x)).astype(BF16)
        xdt_b = xdt.astype(BF16)
        ea = jnp.exp(ac_x)
        sdecay = jnp.exp(alast_x)
        for g in range(SSD_GROUPS):
            gs = slice(g * SSD_GROUP_W, (g + 1) * SSD_GROUP_W)
            ns = slice(g * SSD_STATE, (g + 1) * SSD_STATE)
            cg = cm_ref[rows, ns]
            bg = bm_ref[rows, ns]
            cb = _dot_nt(cg, bg)
            st_g = st_ref[:, gs]
            y_off = jnp.dot(cg, st_g.astype(BF16), preferred_element_type=F32) * ea[:, gs]
            diag = []
            for hh in range(SSD_HPG):
                h = g * SSD_HPG + hh
                acol = _col(sl, SSD_HEADS + h)
                arow = st[SSD_HEADS + h:SSD_HEADS + h + 1, :]
                seg = jnp.exp(jnp.where(incl, acol - arow, NEG_BIG))
                ps = slice(h * SSD_HEADDIM, (h + 1) * SSD_HEADDIM)
                diag.append(jnp.dot((seg * cb).astype(BF16), xdt_b[:, ps], preferred_element_type=F32))
            y_ref[rows, gs] = jnp.concatenate(diag, axis=1) + y_off + dskip_ref[:, gs] * xs[:, gs]
            st_ref[:, gs] = st_g * sdecay[:, gs] + _dot_tn(bg, xdec[:, gs])
        return carry

    lax.fori_loop(0, tl // CHUNK, chunk_body, 0)

    @pl.when(l == pl.num_programs(1) - 1)
    def _():
        sout_ref[...] = st_ref[...].T.reshape(SSD_HEADS, SSD_HEADDIM, SSD_STATE)


def _ssd_scan(xs, bm, cm, slab, slabt, p, tl):
    bsz, seq, _ = xs.shape
    row = lambda w: pl.BlockSpec((None, tl, w), lambda b, l: (b, l, 0))
    consts = (p["edt"], p["eac"], p["dskip"])
    return pl.pallas_call(
        functools.partial(_ssd_scan_kernel, tl=tl),
        grid=(bsz, seq // tl),
        in_specs=[row(SSD_INNER), row(SSD_GN), row(SSD_GN), row(LANES),
                  pl.BlockSpec((None, tl // CHUNK, 2 * SSD_HEADS, CHUNK), lambda b, l: (b, l, 0, 0))]
                 + [_const_spec(c.shape) for c in consts],
        out_specs=(row(SSD_INNER),
                   pl.BlockSpec((None, SSD_HEADS, SSD_HEADDIM, SSD_STATE), lambda b, l: (b, 0, 0, 0))),
        out_shape=(jax.ShapeDtypeStruct((bsz, seq, SSD_INNER), F32),
                   jax.ShapeDtypeStruct((bsz, SSD_HEADS, SSD_HEADDIM, SSD_STATE), F32)),
        scratch_shapes=[pltpu.VMEM((SSD_STATE, SSD_INNER), F32)],
        compiler_params=_params(2),
        name="ssd_scan",
    )(xs, bm, cm, slab, slabt, *consts)


def _ssd_out_kernel(y_ref, z_ref, x_ref, nw_ref, wout_ref, g_ref, b_ref, o_ref):
    t = y_ref[...] * _silu(z_ref[...])
    parts = []
    for g in range(SSD_GROUPS):
        gs = slice(g * SSD_GROUP_W, (g + 1) * SSD_GROUP_W)
        tg = t[:, gs]
        parts.append((tg * lax.rsqrt(jnp.mean(tg * tg, axis=-1, keepdims=True) + RMS_EPS) * nw_ref[:, gs]).astype(BF16))
    mix = jnp.dot(jnp.concatenate(parts, axis=1), wout_ref[...], preferred_element_type=F32)
    o_ref[...] = _layer_norm(DN_ALPHA * x_ref[...] + mix, g_ref[...], b_ref[...])


def _ssd_out(y, z, x, p, ln_g, ln_b, tl):
    bsz, seq, _ = x.shape
    row = lambda w: pl.BlockSpec((None, tl, w), lambda b, l: (b, l, 0))
    consts = (p["nw"], p["wout"], ln_g, ln_b)
    return pl.pallas_call(
        _ssd_out_kernel,
        grid=(bsz, seq // tl),
        in_specs=[row(SSD_INNER), row(SSD_INNER), row(D_MODEL)] + [_const_spec(c.shape) for c in consts],
        out_specs=row(D_MODEL),
        out_shape=jax.ShapeDtypeStruct((bsz, seq, D_MODEL), F32),
        compiler_params=_params(2),
        name="ssd_out",
    )(y, z, x, *consts)


def _step_conv(pre, c_refs, cw_ref, cb_ref):
    width = len(c_refs) + 1
    acc = cb_ref[...] + c_refs[0][...] * cw_ref[0:1, :]
    for k in range(1, width - 1):
        acc = acc + c_refs[k][...] * cw_ref[k:k + 1, :]
    return acc + pre * cw_ref[width - 1:width, :]


def _gdn_in_step_kernel(x_ref, c0_ref, c1_ref, c2_ref, wqkv_ref, wz_ref, wba_ref, cw_ref, cb_ref, alog_ref, dtb_ref,
                        q_ref, k_ref, v_ref, z_ref, slab_ref, pre_ref):
    xb = x_ref[...].astype(BF16)
    n = xb.shape[0]
    lane = lax.broadcasted_iota(jnp.int32, (n, LANES), 1)
    ba = jnp.dot(xb, wba_ref[...], preferred_element_type=F32)
    g = -jnp.exp(alog_ref[...]) * _softplus(ba + dtb_ref[...])
    slab_ref[...] = jnp.where(lane < GDN_HEADS, jax.nn.sigmoid(ba), g)
    z_ref[...] = jnp.dot(xb, wz_ref[...], preferred_element_type=F32)
    pre = jnp.dot(xb, wqkv_ref[...], preferred_element_type=F32)
    pre_ref[...] = pre
    act = _silu(_step_conv(pre, (c0_ref, c1_ref, c2_ref), cw_ref, cb_ref))
    for h in range(GDN_HEADS):
        hs = slice(h * GDN_DK, (h + 1) * GDN_DK)
        q_ref[:, hs] = _l2norm(act[:, hs]) * (GDN_DK ** -0.5)
        k_ref[:, hs] = _l2norm(act[:, GDN_QK + h * GDN_DK:GDN_QK + (h + 1) * GDN_DK])
    v_ref[...] = act[:, 2 * GDN_QK:]


def _gdn_in_step(x, cache, p):
    n = x.shape[0]
    ff = lambda w: jax.ShapeDtypeStruct((n, w), F32)
    return pl.pallas_call(
        _gdn_in_step_kernel,
        out_shape=(ff(GDN_QK), ff(GDN_QK), ff(GDN_VD), ff(GDN_VD), ff(LANES), ff(GDN_CONV_DIM)),
        compiler_params=pltpu.CompilerParams(vmem_limit_bytes=VMEM_LIMIT),
        name="gdn_in_step",
    )(x, cache[:, 0], cache[:, 1], cache[:, 2], p["wqkv"], p["wz"], p["wba"], p["cw"], p["cb"], p["alog"], p["dtb"])


def _gdn_step_kernel(q_ref, k_ref, v_ref, slab_ref, s_ref, o_ref, sout_ref, *, tb):
    zpad = jnp.zeros((LANES - GDN_HEADS, GDN_DK), F32)

    def token_body(t, carry):
        kt = jnp.concatenate([k_ref[t], zpad], axis=0).T
        qt = jnp.concatenate([q_ref[t], zpad], axis=0).T
        v = v_ref[t]
        sl = slab_ref[pl.ds(t, 1), :]
        alpha = jnp.exp(sl)
        for h in range(GDN_HEADS):
            s = s_ref[t, h]
            kc = _col(kt, h)
            a_h = alpha[:, GDN_HEADS + h:GDN_HEADS + h + 1]
            ks = jnp.sum(s * kc, axis=0, keepdims=True)
            v_new = sl[:, h:h + 1] * (v[h:h + 1, :] - a_h * ks)
            s_new = s * a_h + kc * v_new
            sout_ref[t, h] = s_new
            o_ref[t, h:h + 1, :] = jnp.sum(s_new * _col(qt, h), axis=0, keepdims=True)
        return carry

    lax.fori_loop(0, tb, token_body, 0)


def _gdn_step(q, k, v, slab, state, tb):
    n = q.shape[0]
    q3, k3, v3 = (t.reshape(n, GDN_HEADS, GDN_DK) for t in (q, k, v))
    tok = pl.BlockSpec((tb, GDN_HEADS, GDN_DK), lambda i: (i, 0, 0))
    st = pl.BlockSpec((tb, GDN_HEADS, GDN_DK, GDN_DV), lambda i: (i, 0, 0, 0))
    o, s_new = pl.pallas_call(
        functools.partial(_gdn_step_kernel, tb=tb),
        grid=(n // tb,),
        in_specs=[tok, tok, tok, pl.BlockSpec((tb, LANES), lambda i: (i, 0)), st],
        out_specs=(tok, st),
        out_shape=(jax.ShapeDtypeStruct((n, GDN_HEADS, GDN_DV), F32), jax.ShapeDtypeStruct(state.shape, F32)),
        compiler_params=_params(1),
        name="gdn_step",
    )(q3, k3, v3, slab, state)
    return o.reshape(n, GDN_VD), s_new


def _ffn_step_kernel(x_ref, c0_ref, c1_ref, wup_ref, cw_ref, cb_ref, wdown_ref, g_ref, b_ref, y_ref, pre_ref):
    x = x_ref[...]
    xb = x.astype(BF16)
    gate = jnp.dot(xb, wup_ref[:, 0:D_FF], preferred_element_type=F32)
    val = jnp.dot(xb, wup_ref[:, D_FF:2 * D_FF], preferred_element_type=F32)
    pre_ref[...] = gate
    hid = _silu(_step_conv(gate, (c0_ref, c1_ref), cw_ref, cb_ref)) * val
    y_ref[...] = _layer_norm(DN_ALPHA * x + jnp.dot(hid.astype(BF16), wdown_ref[...], preferred_element_type=F32),
                             g_ref[...], b_ref[...])


def _ffn_step(x, cache, p, ln_g, ln_b):
    n = x.shape[0]
    return pl.pallas_call(
        _ffn_step_kernel,
        out_shape=(jax.ShapeDtypeStruct((n, D_MODEL), F32), jax.ShapeDtypeStruct((n, D_FF), F32)),
        compiler_params=pltpu.CompilerParams(vmem_limit_bytes=VMEM_LIMIT),
        name="ffn_step",
    )(x, cache[:, 0], cache[:, 1], p["wup"], p["cw"], p["cb"], p["wdown"], ln_g, ln_b)


def _ssd_in_step_kernel(x_ref, c0_ref, c1_ref, c2_ref, wz_ref, wxbc_ref, wdt_ref, cw_ref, cb_ref, a_ref, dtb_ref,
                        edt_ref, eac_ref, z_ref, xs_ref, bm_ref, cm_ref, xdt_ref, dec_ref, pre_ref):
    xb = x_ref[...].astype(BF16)
    n = xb.shape[0]
    lane = lax.broadcasted_iota(jnp.int32, (n, LANES), 1)
    dt = _softplus(jnp.dot(xb, wdt_ref[...], preferred_element_type=F32) + dtb_ref[...])
    dt = jnp.where(lane < 2 * SSD_HEADS, dt, 0.0)
    slab = jnp.where(lane < SSD_HEADS, dt, dt * a_ref[...])
    z_ref[...] = jnp.dot(xb, wz_ref[...], preferred_element_type=F32)
    pre = jnp.dot(xb, wxbc_ref[...], preferred_element_type=F32)
    pre_ref[...] = pre
    act = _silu(_step_conv(pre, (c0_ref, c1_ref, c2_ref), cw_ref, cb_ref))
    xs = act[:, 0:SSD_INNER]
    xs_ref[...] = xs
    bm_ref[...] = act[:, SSD_INNER:SSD_INNER + SSD_GN]
    cm_ref[...] = act[:, SSD_INNER + SSD_GN:]
    xdt_ref[...] = xs * _dot_exact_rhs(slab, edt_ref[...])
    dec_ref[...] = jnp.exp(_dot_exact_rhs(slab, eac_ref[...]))


def _ssd_in_step(x, cache, p):
    n = x.shape[0]
    ff = lambda w: jax.ShapeDtypeStruct((n, w), F32)
    return pl.pallas_call(
        _ssd_in_step_kernel,
        out_shape=(ff(SSD_INNER), ff(SSD_INNER), ff(SSD_GN), ff(SSD_GN), ff(SSD_INNER), ff(SSD_INNER),
                   ff(SSD_CONV_DIM)),
        compiler_params=pltpu.CompilerParams(vmem_limit_bytes=VMEM_LIMIT),
        name="ssd_in_step",
    )(x, cache[:, 0], cache[:, 1], cache[:, 2], p["wz"], p["wxbc"], p["wdt"], p["cw"], p["cb"], p["a"], p["dtb"],
      p["edt"], p["eac"])


def _ssd_step_kernel(xs_ref, bm_ref, cm_ref, xdt_ref, dec_ref, dskip_ref, s_ref, y_ref, sout_ref, *, tb):
    hp = SSD_INNER
    rid = lax.broadcasted_iota(jnp.int32, (SUBLANES, hp), 0)
    gid = lax.broadcasted_iota(jnp.int32, (SUBLANES, hp), 1) // SSD_GROUP_W
    zpad_r = jnp.zeros((LANES - SUBLANES, hp), F32)

    def token_body(t, carry):
        row = pl.ds(t, 1)
        xdt = xdt_ref[row, :]
        dec = dec_ref[row, :]
        stack = jnp.where(rid == gid, jnp.broadcast_to(xdt, (SUBLANES, hp)), 0.0)
        stack = jnp.where(rid == SSD_GROUPS, jnp.broadcast_to(dec, (SUBLANES, hp)), stack)
        cols = jnp.concatenate([stack, zpad_r], axis=0).T
        s = s_ref[t].reshape(hp, SSD_STATE)
        bm = bm_ref[row, :]
        upd = jnp.concatenate(
            [_col(cols[g * SSD_GROUP_W:(g + 1) * SSD_GROUP_W, :], g) * bm[:, g * SSD_STATE:(g + 1) * SSD_STATE]
             for g in range(SSD_GROUPS)], axis=0)
        s_new = s * _col(cols, SSD_GROUPS) + upd
        sout_ref[t] = s_new.reshape(SSD_HEADS, SSD_HEADDIM, SSD_STATE)
        cm = cm_ref[row, :]
        cmat = jnp.concatenate([cm[:, g * SSD_STATE:(g + 1) * SSD_STATE] for g in range(SSD_GROUPS)]
                               + [jnp.zeros((SUBLANES - SSD_GROUPS, SSD_STATE), F32)], axis=0)
        yall = _dot_nt(cmat, s_new)
        y = jnp.sum(jnp.where(rid == gid, yall, 0.0), axis=0, keepdims=True)
        y_ref[row, :] = y + dskip_ref[...] * xs_ref[row, :]
        return carry

    lax.fori_loop(0, tb, token_body, 0)


def _ssd_step(xs, bm, cm, xdt, dec, state, p, tb):
    n = xs.shape[0]
    tok = lambda w: pl.BlockSpec((tb, w), lambda i: (i, 0))
    st = pl.BlockSpec((tb, SSD_HEADS, SSD_HEADDIM, SSD_STATE), lambda i: (i, 0, 0, 0))
    return pl.pallas_call(
        functools.partial(_ssd_step_kernel, tb=tb),
        grid=(n // tb,),
        in_specs=[tok(SSD_INNER), tok(SSD_GN), tok(SSD_GN), tok(SSD_INNER), tok(SSD_INNER),
                  _const_spec(p["dskip"].shape), st],
        out_specs=(tok(SSD_INNER), st),
        out_shape=(jax.ShapeDtypeStruct((n, SSD_INNER), F32), jax.ShapeDtypeStruct(state.shape, F32)),
        compiler_params=_params(1),
        name="ssd_step",
    )(xs, bm, cm, xdt, dec, p["dskip"], state)


def _row(v, width=None, offset=0):
    v = v.astype(F32).reshape(1, -1)
    if width is None:
        return v
    return jnp.pad(v, ((0, 0), (offset, width - offset - v.shape[1])))


def _prep_gdn(w_in, conv_w, conv_b, a_log, dt_bias, norm_w, w_out):
    ba = w_in[:, GDN_CONV_DIM + GDN_VD:]
    return {
        "wqkv": w_in[:, :GDN_CONV_DIM].astype(BF16),
        "wz": w_in[:, GDN_CONV_DIM:GDN_CONV_DIM + GDN_VD].astype(BF16),
        "wba": jnp.pad(ba, ((0, 0), (0, LANES - ba.shape[1]))).astype(BF16),
        "cw": conv_w.astype(F32),
        "cb": _row(conv_b),
        "alog": _row(a_log, LANES, GDN_HEADS),
        "dtb": _row(dt_bias, LANES, GDN_HEADS),
        "nw": _row(norm_w),
        "wout": w_out.astype(BF16),
    }


def _prep_ssd(w_in, conv_w, conv_b, a_log, dt_bias, d_skip, norm_w, w_out):
    wdt = w_in[:, SSD_INNER + SSD_CONV_DIM:]
    wdt2 = jnp.concatenate([wdt, wdt], axis=1)
    head_of_lane = jnp.arange(SSD_INNER) // SSD_HEADDIM
    sel = jnp.arange(LANES)[:, None]
    return {
        "wz": w_in[:, :SSD_INNER].astype(BF16),
        "wxbc": w_in[:, SSD_INNER:SSD_INNER + SSD_CONV_DIM].astype(BF16),
        "wdt": jnp.pad(wdt2, ((0, 0), (0, LANES - 2 * SSD_HEADS))).astype(BF16),
        "cw": conv_w.astype(F32),
        "cb": _row(conv_b),
        "a": _row(-jnp.exp(a_log.astype(F32)), LANES, SSD_HEADS),
        "dtb": _row(jnp.concatenate([dt_bias, dt_bias]), LANES, 0),
        "edt": (sel == head_of_lane[None, :]).astype(BF16),
        "eac": (sel == head_of_lane[None, :] + SSD_HEADS).astype(BF16),
        "dskip": _row(jnp.repeat(d_skip, SSD_HEADDIM)),
        "nw": _row(norm_w),
        "wout": w_out.astype(BF16),
    }


def _prep_ffn(w_up, conv_w, conv_b, w_down):
    return {"wup": w_up.astype(BF16), "cw": conv_w.astype(F32), "cb": _row(conv_b), "wdown": w_down.astype(BF16)}


def _prompt_trunk(x, gdn, ssd, ffn, ln, tl):
    q, qd, k, kbg, kd, vb, z, slab, slabt, gdn_cache = _gdn_in(x, gdn, tl)
    o, gdn_state = _gdn_scan(q, qd, k, kbg, kd, vb, slab, slabt, tl)
    x = _gdn_out(o, z, x, gdn, ln[0][0], ln[0][1], tl)
    x, ffn_cache0 = _ffn(x, ffn[0], ln[0][2], ln[0][3], tl)
    z, xs, bm, cm, slab, slabt, ssd_cache = _ssd_in(x, ssd, tl)
    y, ssd_state = _ssd_scan(xs, bm, cm, slab, slabt, ssd, tl)
    x = _ssd_out(y, z, x, ssd, ln[1][0], ln[1][1], tl)
    x, ffn_cache1 = _ffn(x, ffn[1], ln[1][2], ln[1][3], tl)
    return x, gdn_cache, gdn_state, ssd_cache, ssd_state, jnp.stack([ffn_cache0, ffn_cache1])


def _sample_trunk(x, gdn_cache, gdn_state, ssd_cache, ssd_state, ffn_cache, gdn, ssd, ffn, ln, tb):
    n = x.shape[0]
    q, k, v, z, slab, pre = _gdn_in_step(x, gdn_cache, gdn)
    gdn_cache_new = jnp.concatenate([gdn_cache[:, 1:], pre[:, None]], axis=1)
    o, gdn_state_new = _gdn_step(q, k, v, slab, gdn_state, tb)
    x = _gdn_out(o[None], z[None], x[None], gdn, ln[0][0], ln[0][1], n)[0]
    x, pre = _ffn_step(x, ffn_cache[0], ffn[0], ln[0][2], ln[0][3])
    ffn_cache0 = jnp.concatenate([ffn_cache[0][:, 1:], pre[:, None]], axis=1)
    z, xs, bm, cm, xdt, dec, pre = _ssd_in_step(x, ssd_cache, ssd)
    ssd_cache_new = jnp.concatenate([ssd_cache[:, 1:], pre[:, None]], axis=1)
    y, ssd_state_new = _ssd_step(xs, bm, cm, xdt, dec, ssd_state, ssd, tb)
    x = _ssd_out(y[None], z[None], x[None], ssd, ln[1][0], ln[1][1], n)[0]
    x, pre = _ffn_step(x, ffn_cache[1], ffn[1], ln[1][2], ln[1][3])
    ffn_cache1 = jnp.concatenate([ffn_cache[1][:, 1:], pre[:, None]], axis=1)
    return x, gdn_cache_new, gdn_state_new, ssd_cache_new, ssd_state_new, jnp.stack([ffn_cache0, ffn_cache1])


PROMPT_TILE = 512
SAMPLE_TOKENS = 8


def kernel(x_prompt, x_sample, cache_gdn_conv, state_gdn, cache_ssd_conv, state_ssd, cache_ffn_conv, gdn_w_in, gdn_conv_w, gdn_conv_b, gdn_a_log, gdn_dt_bias, gdn_norm_w, gdn_w_out, ssd_w_in, ssd_conv_w, ssd_conv_b, ssd_a_log, ssd_dt_bias, ssd_d, ssd_norm_w, ssd_w_out, ffn_w_up, ffn_conv_w, ffn_conv_b, ffn_w_down, ln1_g, ln1_b, ln2_g, ln2_b):
    gdn = _prep_gdn(gdn_w_in[0], gdn_conv_w[0], gdn_conv_b[0], gdn_a_log[0], gdn_dt_bias[0], gdn_norm_w[0],
                    gdn_w_out[0])
    ssd = _prep_ssd(ssd_w_in[0], ssd_conv_w[0], ssd_conv_b[0], ssd_a_log[0], ssd_dt_bias[0], ssd_d[0],
                    ssd_norm_w[0], ssd_w_out[0])
    ffn = [_prep_ffn(ffn_w_up[i], ffn_conv_w[i], ffn_conv_b[i], ffn_w_down[i]) for i in range(DEPTH)]
    ln = [(_row(ln1_g[i]), _row(ln1_b[i]), _row(ln2_g[i]), _row(ln2_b[i])) for i in range(DEPTH)]

    tl = min(PROMPT_TILE, x_prompt.shape[1])
    y_p, gcp, gsp, scp, ssp, fcp = _prompt_trunk(x_prompt, gdn, ssd, ffn, ln, tl)
    y_s, gcs, gss, scs, sss, fcs = _sample_trunk(
        x_sample[:, 0], cache_gdn_conv[0], state_gdn[0], cache_ssd_conv[0], state_ssd[0], cache_ffn_conv,
        gdn, ssd, ffn, ln, min(SAMPLE_TOKENS, x_sample.shape[0]))
    return (y_p, y_s[:, None], gcp[None], gcs[None], gsp[None], gss[None], scp[None], scs[None],
            ssp[None], sss[None], fcp, fcs)
```

```python
import functools

import jax
import jax.numpy as jnp
from jax import lax
from jax.experimental import pallas as pl
from jax.experimental.pallas import tpu as pltpu

F32 = jnp.float32
BF16 = jnp.bfloat16

D_MODEL = 1024
DEPTH = 2
CONV_W = 4
CHUNK = 64

GDN_HEADS = 8
GDN_DK = 128
GDN_DV = 128
GDN_QK = GDN_HEADS * GDN_DK
GDN_VD = GDN_HEADS * GDN_DV
GDN_CONV_DIM = 2 * GDN_QK + GDN_VD

SSD_INNER = 2 * D_MODEL
SSD_HEADDIM = 64
SSD_HEADS = SSD_INNER // SSD_HEADDIM
SSD_GROUPS = 4
SSD_STATE = 128
SSD_HPG = SSD_HEADS // SSD_GROUPS
SSD_GN = SSD_GROUPS * SSD_STATE
SSD_CONV_DIM = SSD_INNER + 2 * SSD_GN
SSD_GROUP_W = SSD_INNER // SSD_GROUPS

D_FF = 2816
FFN_CONV_W = 3
FFN_BLOCK = D_FF // 2

DN_ALPHA = (2 * DEPTH) ** 0.25
LN_EPS = 1e-5
RMS_EPS = 1e-6
L2_EPS = 1e-6

LANES = 128
SUBLANES = 8
VMEM_LIMIT = 56 * 1024 * 1024

NEG_BIG = -1e30


def _dot(a, b):
    return jnp.dot(a.astype(BF16), b.astype(BF16), preferred_element_type=F32)


def _dot_nt(a, b):
    return lax.dot_general(a.astype(BF16), b.astype(BF16), (((1,), (1,)), ((), ())),
                           preferred_element_type=F32)


def _dot_tn(a, b):
    return lax.dot_general(a.astype(BF16), b.astype(BF16), (((0,), (0,)), ((), ())),
                           preferred_element_type=F32)


def _split3(x):
    hi = x.astype(BF16)
    r1 = x - hi.astype(F32)
    mid = r1.astype(BF16)
    lo = (r1 - mid.astype(F32)).astype(BF16)
    return hi, mid, lo


def _dot_exact_lhs(e, x):
    hi, mid, lo = _split3(x)
    return (jnp.dot(e, hi, preferred_element_type=F32) + jnp.dot(e, mid, preferred_element_type=F32)
            + jnp.dot(e, lo, preferred_element_type=F32))


def _dot_exact_rhs(x, e):
    hi, mid, lo = _split3(x)
    return (jnp.dot(hi, e, preferred_element_type=F32) + jnp.dot(mid, e, preferred_element_type=F32)
            + jnp.dot(lo, e, preferred_element_type=F32))


def _dot3(a, b):
    a_hi = a.astype(BF16)
    a_lo = (a - a_hi.astype(F32)).astype(BF16)
    b_hi = b.astype(BF16)
    b_lo = (b - b_hi.astype(F32)).astype(BF16)
    return (jnp.dot(a_hi, b_hi, preferred_element_type=F32) + jnp.dot(a_hi, b_lo, preferred_element_type=F32)
            + jnp.dot(a_lo, b_hi, preferred_element_type=F32))


def _silu(x):
    return x * jax.nn.sigmoid(x)


def _softplus(x):
    return jnp.maximum(x, 0.0) + jnp.log1p(jnp.exp(-jnp.abs(x)))


def _layer_norm(x, g, b):
    mu = jnp.mean(x, axis=-1, keepdims=True)
    xc = x - mu
    var = jnp.mean(xc * xc, axis=-1, keepdims=True)
    return xc * lax.rsqrt(var + LN_EPS) * g + b


def _l2norm(t):
    return t * lax.rsqrt(jnp.sum(t * t, axis=-1, keepdims=True) + L2_EPS)


def _col(a, i):
    return a[:, i:i + 1]


def _const_spec(shape):
    nd = len(shape)
    return pl.BlockSpec(shape, lambda *_: (0,) * nd)


def _params(n_grid):
    return pltpu.CompilerParams(dimension_semantics=("arbitrary",) * n_grid, vmem_limit_bytes=VMEM_LIMIT)


def _chunk_mats(tl):
    r = jnp.arange(tl)
    same = (r[:, None] // CHUNK) == (r[None, :] // CHUNK)
    tri = (same & (r[:, None] >= r[None, :])).astype(BF16)
    tot = same.astype(BF16)
    return tri, tot


def _causal_conv(xx_ref, cw_ref, cb_ref, cs, tl, width):
    base = SUBLANES - (width - 1)
    acc = cb_ref[:, cs] + xx_ref[base:base + tl, cs] * cw_ref[0:1, cs]
    for k in range(1, width):
        acc = acc + xx_ref[base + k:base + k + tl, cs] * cw_ref[k:k + 1, cs]
    return acc


def _gdn_in_kernel(x_ref, wqkv_ref, wz_ref, wba_ref, cw_ref, cb_ref, alog_ref, dtb_ref, tri_ref, tot_ref,
                   q_ref, qd_ref, k_ref, kbg_ref, kd_ref, vb_ref, z_ref, slab_ref, slabt_ref, cache_ref,
                   xx_ref, *, tl):
    l = pl.program_id(1)
    xb = x_ref[...].astype(BF16)
    lane = lax.broadcasted_iota(jnp.int32, (tl, LANES), 1)

    ba = jnp.dot(xb, wba_ref[...], preferred_element_type=F32)
    beta = jax.nn.sigmoid(ba)
    g = -jnp.exp(alog_ref[...]) * _softplus(ba + dtb_ref[...])
    g = jnp.where((lane >= GDN_HEADS) & (lane < 2 * GDN_HEADS), g, 0.0)
    gcum = _dot_exact_lhs(tri_ref[...], g)
    gtot = _dot_exact_lhs(tot_ref[...], g)
    slab = jnp.where(lane < GDN_HEADS, beta, gcum)
    slab_ref[...] = slab
    slab_t = slab.T
    for c in range(tl // CHUNK):
        slabt_ref[c] = slab_t[0:2 * GDN_HEADS, c * CHUNK:(c + 1) * CHUNK]
    eg = jnp.exp(gcum)
    ed = jnp.exp(gtot - gcum)

    z_ref[...] = jnp.dot(xb, wz_ref[...], preferred_element_type=F32)

    @pl.when(l == 0)
    def _():
        xx_ref[0:SUBLANES, :] = jnp.zeros((SUBLANES, GDN_CONV_DIM), F32)

    for j in range(3):
        cs = slice(j * GDN_QK, (j + 1) * GDN_QK)
        xx_ref[SUBLANES:SUBLANES + tl, cs] = jnp.dot(xb, wqkv_ref[:, cs], preferred_element_type=F32)
    cache_ref[...] = xx_ref[SUBLANES + tl - (CONV_W - 1):SUBLANES + tl, :]

    for j in range(3):
        cs = slice(j * GDN_QK, (j + 1) * GDN_QK)
        act = _silu(_causal_conv(xx_ref, cw_ref, cb_ref, cs, tl, CONV_W))
        for h in range(GDN_HEADS):
            hs = slice(h * GDN_DK, (h + 1) * GDN_DK)
            t = act[:, hs]
            bcol = _col(slab, h)
            egc = _col(eg, GDN_HEADS + h)
            if j == 0:
                qn = _l2norm(t) * (GDN_DK ** -0.5)
                q_ref[:, hs] = qn.astype(BF16)
                qd_ref[:, hs] = (qn * egc).astype(BF16)
            elif j == 1:
                kn = _l2norm(t)
                k_ref[:, hs] = kn.astype(BF16)
                kbg_ref[:, hs] = (kn * (bcol * egc)).astype(BF16)
                kd_ref[:, hs] = (kn * _col(ed, GDN_HEADS + h)).astype(BF16)
            else:
                vb_ref[:, hs] = t * bcol

    xx_ref[0:SUBLANES, :] = xx_ref[tl:tl + SUBLANES, :]


def _gdn_in(x, p, tl):
    bsz, seq, _ = x.shape
    nl = seq // tl
    tri, tot = _chunk_mats(tl)
    row = lambda w: pl.BlockSpec((None, tl, w), lambda b, l: (b, l, 0))
    bf = lambda w: jax.ShapeDtypeStruct((bsz, seq, w), BF16)
    ff = lambda w: jax.ShapeDtypeStruct((bsz, seq, w), F32)
    out_shape = (bf(GDN_QK), bf(GDN_QK), bf(GDN_QK), bf(GDN_QK), bf(GDN_QK), ff(GDN_VD), ff(GDN_VD), ff(LANES),
                 jax.ShapeDtypeStruct((bsz, seq // CHUNK, 2 * GDN_HEADS, CHUNK), F32),
                 jax.ShapeDtypeStruct((bsz, CONV_W - 1, GDN_CONV_DIM), F32))
    out_specs = (row(GDN_QK),) * 5 + (row(GDN_VD), row(GDN_VD), row(LANES),
                 pl.BlockSpec((None, tl // CHUNK, 2 * GDN_HEADS, CHUNK), lambda b, l: (b, l, 0, 0)),
                 pl.BlockSpec((None, CONV_W - 1, GDN_CONV_DIM), lambda b, l: (b, 0, 0)))
    consts = (p["wqkv"], p["wz"], p["wba"], p["cw"], p["cb"], p["alog"], p["dtb"], tri, tot)
    return pl.pallas_call(
        functools.partial(_gdn_in_kernel, tl=tl),
        grid=(bsz, nl),
        in_specs=[row(D_MODEL)] + [_const_spec(c.shape) for c in consts],
        out_specs=out_specs,
        out_shape=out_shape,
        scratch_shapes=[pltpu.VMEM((tl + SUBLANES, GDN_CONV_DIM), F32)],
        compiler_params=_params(2),
        name="gdn_in",
    )(x, *consts)


def _split2(x):
    hi = x.astype(BF16)
    return hi, (x - hi.astype(F32)).astype(BF16)


def _pair_blockdiag(x, left):
    z = jnp.zeros_like(x)
    return jnp.concatenate([jnp.where(left, x, z), jnp.where(left, z, x)], axis=0)


def _lane_blockdiag(a, b):
    z = jnp.zeros_like(a)
    return jnp.concatenate([jnp.concatenate([a, z], axis=1), jnp.concatenate([z, b], axis=1)], axis=0)


def _pair_matmul(lhs_parts, w_parts, left):
    n = len(lhs_parts)
    his = [p[0] for p in lhs_parts]
    los = [p[1] for p in lhs_parts]
    r1 = jnp.dot(jnp.concatenate(his + los, axis=0), _pair_blockdiag(w_parts[0], left), preferred_element_type=F32)
    r2 = jnp.dot(jnp.concatenate(his, axis=0) if n > 1 else his[0], _pair_blockdiag(w_parts[1], left),
                 preferred_element_type=F32)
    c = CHUNK
    return [r1[i * c:(i + 1) * c] + r1[(n + i) * c:(n + i + 1) * c] + r2[i * c:(i + 1) * c] for i in range(n)]


def _gdn_prep_stages(c, slot, k_ref, q_ref, slab_ref, slabt_ref, tinv_ref, qkg_ref, masks):
    incl, strict, eye2, left = masks
    npair = GDN_HEADS // 2
    rows = pl.ds(pl.multiple_of(c * CHUNK, CHUNK), CHUNK)
    sl = slab_ref[rows, :]
    st = slabt_ref[c]
    ns = []
    for p in range(npair):
        ha, hb = 2 * p, 2 * p + 1
        ka = k_ref[rows, ha * GDN_DK:(ha + 1) * GDN_DK]
        kb = k_ref[rows, hb * GDN_DK:(hb + 1) * GDN_DK]
        kk = jnp.concatenate([_dot_nt(ka, ka), _dot_nt(kb, kb)], axis=1)
        qk = jnp.concatenate([_dot_nt(q_ref[rows, ha * GDN_DK:(ha + 1) * GDN_DK], ka),
                              _dot_nt(q_ref[rows, hb * GDN_DK:(hb + 1) * GDN_DK], kb)], axis=1)
        gcol = jnp.where(left, _col(sl, GDN_HEADS + ha), _col(sl, GDN_HEADS + hb))
        bcol = jnp.where(left, _col(sl, ha), _col(sl, hb))
        grow = jnp.concatenate([st[GDN_HEADS + ha:GDN_HEADS + ha + 1, :], st[GDN_HEADS + hb:GDN_HEADS + hb + 1, :]],
                               axis=1)
        gam = jnp.exp(jnp.where(incl, gcol - grow, NEG_BIG))
        qkg_ref[slot, p] = (qk * gam).astype(BF16)
        ns.append(jnp.where(strict, kk * gam * bcol, 0.0) * -1.0)
    yield
    ps = [eye2 + n for n in ns]
    nparts = [_split2(n) for n in ns]
    ns = [_pair_matmul([nparts[p]], nparts[p], left)[0] for p in range(npair)]
    yield
    k = 2
    while 2 * k < CHUNK:
        nparts = [_split2(n) for n in ns]
        outs = [_pair_matmul([_split2(ps[p]), nparts[p]], nparts[p], left) for p in range(npair)]
        ps = [ps[p] + outs[p][0] for p in range(npair)]
        ns = [outs[p][1] for p in range(npair)]
        k *= 2
        yield
    for p in range(npair):
        t_inv = ps[p] + _pair_matmul([_split2(ps[p])], _split2(ns[p]), left)[0]
        tinv_ref[slot, p] = t_inv.astype(BF16)
    yield


def _gdn_state_stages(c, slot, qd_ref, kbg_ref, kd_ref, vb_ref, slabt_ref, tinv_ref, qkg_ref, o_ref, s_ref):
    npair = GDN_HEADS // 2
    rows = pl.ds(pl.multiple_of(c * CHUNK, CHUNK), CHUNK)
    st = slabt_ref[c]
    hcols = lambda h: slice(h * GDN_DK, (h + 1) * GDN_DK)
    r1 = []
    for h in range(GDN_HEADS):
        lhs = jnp.concatenate([kbg_ref[rows, hcols(h)], qd_ref[rows, hcols(h)]], axis=0)
        r1.append(jnp.dot(lhs, s_ref[h].astype(BF16), preferred_element_type=F32))
    yield
    rhs = [(vb_ref[rows, hcols(h)] - r1[h][0:CHUNK]).astype(BF16) for h in range(GDN_HEADS)]
    v_new = [jnp.dot(tinv_ref[slot, p], _lane_blockdiag(rhs[2 * p], rhs[2 * p + 1]),
                     preferred_element_type=F32).astype(BF16) for p in range(npair)]
    yield
    for p in range(npair):
        va, vb = v_new[p][:, 0:GDN_DV], v_new[p][:, GDN_DV:2 * GDN_DV]
        qs = jnp.concatenate([r1[2 * p][CHUNK:2 * CHUNK], r1[2 * p + 1][CHUNK:2 * CHUNK]], axis=1)
        o_ref[rows, 2 * p * GDN_DV:(2 * p + 2) * GDN_DV] = qs + jnp.dot(
            qkg_ref[slot, p], _lane_blockdiag(va, vb), preferred_element_type=F32)
        for h, vh in ((2 * p, va), (2 * p + 1, vb)):
            decay = jnp.exp(st[GDN_HEADS + h:GDN_HEADS + h + 1, CHUNK - 1:CHUNK])
            s_ref[h] = s_ref[h] * decay + _dot_tn(kd_ref[rows, hcols(h)], vh)
    yield


def _interleave(order, streams):
    for ch in order:
        next(streams[ch])


def _gdn_scan_kernel(q_ref, qd_ref, k_ref, kbg_ref, kd_ref, vb_ref, slab_ref, slabt_ref,
                     o_ref, sout_ref, s_ref, tinv_ref, qkg_ref, *, tl):
    l = pl.program_id(1)
    nchunk = tl // CHUNK

    @pl.when(l == 0)
    def _():
        s_ref[...] = jnp.zeros(s_ref.shape, F32)

    ri = lax.broadcasted_iota(jnp.int32, (CHUNK, 2 * CHUNK), 0)
    li = lax.broadcasted_iota(jnp.int32, (CHUNK, 2 * CHUNK), 1)
    ci = li & (CHUNK - 1)
    masks = (ri >= ci, ri > ci, (ri == ci).astype(F32), li < CHUNK)

    prep = functools.partial(_gdn_prep_stages, k_ref=k_ref, q_ref=q_ref, slab_ref=slab_ref, slabt_ref=slabt_ref,
                             tinv_ref=tinv_ref, qkg_ref=qkg_ref, masks=masks)
    state = functools.partial(_gdn_state_stages, qd_ref=qd_ref, kbg_ref=kbg_ref, kd_ref=kd_ref, vb_ref=vb_ref,
                              slabt_ref=slabt_ref, tinv_ref=tinv_ref, qkg_ref=qkg_ref, o_ref=o_ref, s_ref=s_ref)

    _interleave("A" * 7, {"A": prep(0, 0)})

    def chunk_body(c, carry):
        slot = c & 1
        nxt = jnp.minimum(c + 1, nchunk - 1)
        _interleave("ABAABAABAA", {"A": prep(nxt, 1 - slot), "B": state(c, slot)})
        return carry

    lax.fori_loop(0, nchunk, chunk_body, 0)

    @pl.when(l == pl.num_programs(1) - 1)
    def _():
        sout_ref[...] = s_ref[...]


def _gdn_scan(q, qd, k, kbg, kd, vb, slab, slabt, tl):
    bsz, seq, _ = q.shape
    row = lambda w: pl.BlockSpec((None, tl, w), lambda b, l: (b, l, 0))
    pair_scratch = pltpu.VMEM((2, GDN_HEADS // 2, CHUNK, 2 * CHUNK), BF16)
    return pl.pallas_call(
        functools.partial(_gdn_scan_kernel, tl=tl),
        grid=(bsz, seq // tl),
        in_specs=[row(GDN_QK)] * 5 + [row(GDN_VD), row(LANES),
                  pl.BlockSpec((None, tl // CHUNK, 2 * GDN_HEADS, CHUNK), lambda b, l: (b, l, 0, 0))],
        out_specs=(row(GDN_VD), pl.BlockSpec((None, GDN_HEADS, GDN_DK, GDN_DV), lambda b, l: (b, 0, 0, 0))),
        out_shape=(jax.ShapeDtypeStruct((bsz, seq, GDN_VD), F32),
                   jax.ShapeDtypeStruct((bsz, GDN_HEADS, GDN_DK, GDN_DV), F32)),
        scratch_shapes=[pltpu.VMEM((GDN_HEADS, GDN_DK, GDN_DV), F32), pair_scratch, pair_scratch],
        compiler_params=_params(2),
        name="gdn_scan",
    )(q, qd, k, kbg, kd, vb, slab, slabt)


def _gdn_out_kernel(o_ref, z_ref, x_ref, nw_ref, wout_ref, g_ref, b_ref, y_ref):
    o = o_ref[...]
    z = z_ref[...]
    parts = []
    for h in range(GDN_HEADS):
        hs = slice(h * GDN_DV, (h + 1) * GDN_DV)
        oh = o[:, hs]
        rn = oh * lax.rsqrt(jnp.mean(oh * oh, axis=-1, keepdims=True) + RMS_EPS) * nw_ref[...]
        parts.append((rn * _silu(z[:, hs])).astype(BF16))
    mix = jnp.dot(jnp.concatenate(parts, axis=1), wout_ref[...], preferred_element_type=F32)
    y_ref[...] = _layer_norm(DN_ALPHA * x_ref[...] + mix, g_ref[...], b_ref[...])


def _gdn_out(o, z, x, p, ln_g, ln_b, tl):
    bsz, seq, _ = x.shape
    row = lambda w: pl.BlockSpec((None, tl, w), lambda b, l: (b, l, 0))
    consts = (p["nw"], p["wout"], ln_g, ln_b)
    return pl.pallas_call(
        _gdn_out_kernel,
        grid=(bsz, seq // tl),
        in_specs=[row(GDN_VD), row(GDN_VD), row(D_MODEL)] + [_const_spec(c.shape) for c in consts],
        out_specs=row(D_MODEL),
        out_shape=jax.ShapeDtypeStruct((bsz, seq, D_MODEL), F32),
        compiler_params=_params(2),
        name="gdn_out",
    )(o, z, x, *consts)


def _ffn_kernel(x_ref, wup_ref, cw_ref, cb_ref, wdown_ref, g_ref, b_ref, y_ref, cache_ref, gg_ref, *, tl):
    l = pl.program_id(1)

    @pl.when(l == 0)
    def _():
        gg_ref[0:SUBLANES, :] = jnp.zeros((SUBLANES, D_FF), F32)

    x = x_ref[...]
    xb = x.astype(BF16)
    acc = jnp.zeros((tl, D_MODEL), F32)
    for j in range(D_FF // FFN_BLOCK):
        cs = slice(j * FFN_BLOCK, (j + 1) * FFN_BLOCK)
        vs = slice(D_FF + j * FFN_BLOCK, D_FF + (j + 1) * FFN_BLOCK)
        gg_ref[SUBLANES:SUBLANES + tl, cs] = jnp.dot(xb, wup_ref[:, cs], preferred_element_type=F32)
        val = jnp.dot(xb, wup_ref[:, vs], preferred_element_type=F32)
        hid = _silu(_causal_conv(gg_ref, cw_ref, cb_ref, cs, tl, FFN_CONV_W)) * val
        acc = acc + jnp.dot(hid.astype(BF16), wdown_ref[cs, :], preferred_element_type=F32)
    cache_ref[...] = gg_ref[SUBLANES + tl - (FFN_CONV_W - 1):SUBLANES + tl, :]
    gg_ref[0:SUBLANES, :] = gg_ref[tl:tl + SUBLANES, :]
    y_ref[...] = _layer_norm(DN_ALPHA * x + acc, g_ref[...], b_ref[...])


def _ffn(x, p, ln_g, ln_b, tl):
    bsz, seq, _ = x.shape
    row = lambda w: pl.BlockSpec((None, tl, w), lambda b, l: (b, l, 0))
    consts = (p["wup"], p["cw"], p["cb"], p["wdown"], ln_g, ln_b)
    return pl.pallas_call(
        functools.partial(_ffn_kernel, tl=tl),
        grid=(bsz, seq // tl),
        in_specs=[row(D_MODEL)] + [_const_spec(c.shape) for c in consts],
        out_specs=(row(D_MODEL), pl.BlockSpec((None, FFN_CONV_W - 1, D_FF), lambda b, l: (b, 0, 0))),
        out_shape=(jax.ShapeDtypeStruct((bsz, seq, D_MODEL), F32),
                   jax.ShapeDtypeStruct((bsz, FFN_CONV_W - 1, D_FF), F32)),
        scratch_shapes=[pltpu.VMEM((tl + SUBLANES, D_FF), F32)],
        compiler_params=_params(2),
        name="ffn",
    )(x, *consts)


def _ssd_in_kernel(x_ref, wz_ref, wxbc_ref, wdt_ref, cw_ref, cb_ref, a_ref, dtb_ref, tri_ref,
                   z_ref, xs_ref, bm_ref, cm_ref, slab_ref, slabt_ref, cache_ref, xx_ref, *, tl):
    l = pl.program_id(1)
    xb = x_ref[...].astype(BF16)
    lane = lax.broadcasted_iota(jnp.int32, (tl, LANES), 1)

    dt = _softplus(jnp.dot(xb, wdt_ref[...], preferred_element_type=F32) + dtb_ref[...])
    dt = jnp.where(lane < 2 * SSD_HEADS, dt, 0.0)
    acum = _dot_exact_lhs(tri_ref[...], dt * a_ref[...])
    slab = jnp.where(lane < SSD_HEADS, dt, acum)
    slab_ref[...] = slab
    slab_t = slab.T
    for c in range(tl // CHUNK):
        slabt_ref[c] = slab_t[0:2 * SSD_HEADS, c * CHUNK:(c + 1) * CHUNK]

    z_ref[...] = jnp.dot(xb, wz_ref[...], preferred_element_type=F32)

    @pl.when(l == 0)
    def _():
        xx_ref[0:SUBLANES, :] = jnp.zeros((SUBLANES, SSD_CONV_DIM), F32)

    nblk = SSD_CONV_DIM // D_MODEL
    for j in range(nblk):
        cs = slice(j * D_MODEL, (j + 1) * D_MODEL)
        xx_ref[SUBLANES:SUBLANES + tl, cs] = jnp.dot(xb, wxbc_ref[:, cs], preferred_element_type=F32)
    cache_ref[...] = xx_ref[SUBLANES + tl - (CONV_W - 1):SUBLANES + tl, :]

    for j in range(nblk):
        cs = slice(j * D_MODEL, (j + 1) * D_MODEL)
        act = _silu(_causal_conv(xx_ref, cw_ref, cb_ref, cs, tl, CONV_W))
        if j < 2:
            xs_ref[:, cs] = act
        else:
            bm_ref[...] = act[:, 0:SSD_GN].astype(BF16)
            cm_ref[...] = act[:, SSD_GN:2 * SSD_GN].astype(BF16)

    xx_ref[0:SUBLANES, :] = xx_ref[tl:tl + SUBLANES, :]


def _ssd_in(x, p, tl):
    bsz, seq, _ = x.shape
    tri, _ = _chunk_mats(tl)
    row = lambda w: pl.BlockSpec((None, tl, w), lambda b, l: (b, l, 0))
    out_shape = (jax.ShapeDtypeStruct((bsz, seq, SSD_INNER), F32), jax.ShapeDtypeStruct((bsz, seq, SSD_INNER), F32),
                 jax.ShapeDtypeStruct((bsz, seq, SSD_GN), BF16), jax.ShapeDtypeStruct((bsz, seq, SSD_GN), BF16),
                 jax.ShapeDtypeStruct((bsz, seq, LANES), F32),
                 jax.ShapeDtypeStruct((bsz, seq // CHUNK, 2 * SSD_HEADS, CHUNK), F32),
                 jax.ShapeDtypeStruct((bsz, CONV_W - 1, SSD_CONV_DIM), F32))
    out_specs = (row(SSD_INNER), row(SSD_INNER), row(SSD_GN), row(SSD_GN), row(LANES),
                 pl.BlockSpec((None, tl // CHUNK, 2 * SSD_HEADS, CHUNK), lambda b, l: (b, l, 0, 0)),
                 pl.BlockSpec((None, CONV_W - 1, SSD_CONV_DIM), lambda b, l: (b, 0, 0)))
    consts = (p["wz"], p["wxbc"], p["wdt"], p["cw"], p["cb"], p["a"], p["dtb"], tri)
    return pl.pallas_call(
        functools.partial(_ssd_in_kernel, tl=tl),
        grid=(bsz, seq // tl),
        in_specs=[row(D_MODEL)] + [_const_spec(c.shape) for c in consts],
        out_specs=out_specs,
        out_shape=out_shape,
        scratch_shapes=[pltpu.VMEM((tl + SUBLANES, SSD_CONV_DIM), F32)],
        compiler_params=_params(2),
        name="ssd_in",
    )(x, *consts)


def _ssd_scan_kernel(xs_ref, bm_ref, cm_ref, slab_ref, slabt_ref, edt_ref, eac_ref, dskip_ref,
                     y_ref, sout_ref, st_ref, *, tl):
    l = pl.program_id(1)

    @pl.when(l == 0)
    def _():
        st_ref[...] = jnp.zeros(st_ref.shape, F32)

    ri = lax.broadcasted_iota(jnp.int32, (CHUNK, CHUNK), 0)
    ci = lax.broadcasted_iota(jnp.int32, (CHUNK, CHUNK), 1)
    incl = ri >= ci

    def chunk_body(c, carry):
        off = pl.multiple_of(c * CHUNK, CHUNK)
        rows = pl.ds(off, CHUNK)
        sl = slab_ref[rows, :]
        st = slabt_ref[c]
        dt_x = _dot_exact_rhs(sl, edt_ref[...])
        ac_x = _dot_exact_rhs(sl, eac_ref[...])
        xs = xs_ref[rows, :]
        xdt = xs * dt_x
        alast_x = ac_x[CHUNK - 1:CHUNK, :]
        xdec = (xdt * jnp.exp(alast_x - ac_x)).astype(BF16)
        xdt_b = xdt.astype(BF16)
        ea = jnp.exp(ac_x)
        sdecay = jnp.exp(alast_x)
        for g in range(SSD_GROUPS):
            gs = slice(g * SSD_GROUP_W, (g + 1) * SSD_GROUP_W)
            ns = slice(g * SSD_STATE, (g + 1) * SSD_STATE)
            cg = cm_ref[rows, ns]
            bg = bm_ref[rows, ns]
            cb = _dot_nt(cg, bg)
            st_g = st_ref[:, gs]
            y_off = jnp.dot(cg, st_g.astype(BF16), preferred_element_type=F32) * ea[:, gs]
            diag = []
            for hh in range(SSD_HPG):
                h = g * SSD_HPG + hh
                acol = _col(sl, SSD_HEADS + h)
                arow = st[SSD_HEADS + h:SSD_HEADS + h + 1, :]
                seg = jnp.exp(jnp.where(incl, acol - arow, NEG_BIG))
                ps = slice(h * SSD_HEADDIM, (h + 1) * SSD_HEADDIM)
                diag.append(jnp.dot((seg * cb).astype(BF16), xdt_b[:, ps], preferred_element_type=F32))
            y_ref[rows, gs] = jnp.concatenate(diag, axis=1) + y_off + dskip_ref[:, gs] * xs[:, gs]
            st_ref[:, gs] = st_g * sdecay[:, gs] + _dot_tn(bg, xdec[:, gs])
        return carry

    lax.fori_loop(0, tl // CHUNK, chunk_body, 0)

    @pl.when(l == pl.num_programs(1) - 1)
    def _():
        sout_ref[...] = st_ref[...].T.reshape(SSD_HEADS, SSD_HEADDIM, SSD_STATE)


def _ssd_scan(xs, bm, cm, slab, slabt, p, tl):
    bsz, seq, _ = xs.shape
    row = lambda w: pl.BlockSpec((None, tl, w), lambda b, l: (b, l, 0))
    consts = (p["edt"], p["eac"], p["dskip"])
    return pl.pallas_call(
        functools.partial(_ssd_scan_kernel, tl=tl),
        grid=(bsz, seq // tl),
        in_specs=[row(SSD_INNER), row(SSD_GN), row(SSD_GN), row(LANES),
                  pl.BlockSpec((None, tl // CHUNK, 2 * SSD_HEADS, CHUNK), lambda b, l: (b, l, 0, 0))]
                 + [_const_spec(c.shape) for c in consts],
        out_specs=(row(SSD_INNER),
                   pl.BlockSpec((None, SSD_HEADS, SSD_HEADDIM, SSD_STATE), lambda b, l: (b, 0, 0, 0))),
        out_shape=(jax.ShapeDtypeStruct((bsz, seq, SSD_INNER), F32),
                   jax.ShapeDtypeStruct((bsz, SSD_HEADS, SSD_HEADDIM, SSD_STATE), F32)),
        scratch_shapes=[pltpu.VMEM((SSD_STATE, SSD_INNER), F32)],
        compiler_params=_params(2),
        name="ssd_scan",
    )(xs, bm, cm, slab, slabt, *consts)


def _ssd_out_kernel(y_ref, z_ref, x_ref, nw_ref, wout_ref, g_ref, b_ref, o_ref):
    t = y_ref[...] * _silu(z_ref[...])
    parts = []
    for g in range(SSD_GROUPS):
        gs = slice(g * SSD_GROUP_W, (g + 1) * SSD_GROUP_W)
        tg = t[:, gs]
        parts.append((tg * lax.rsqrt(jnp.mean(tg * tg, axis=-1, keepdims=True) + RMS_EPS) * nw_ref[:, gs]).astype(BF16))
    mix = jnp.dot(jnp.concatenate(parts, axis=1), wout_ref[...], preferred_element_type=F32)
    o_ref[...] = _layer_norm(DN_ALPHA * x_ref[...] + mix, g_ref[...], b_ref[...])


def _ssd_out(y, z, x, p, ln_g, ln_b, tl):
    bsz, seq, _ = x.shape
    row = lambda w: pl.BlockSpec((None, tl, w), lambda b, l: (b, l, 0))
    consts = (p["nw"], p["wout"], ln_g, ln_b)
    return pl.pallas_call(
        _ssd_out_kernel,
        grid=(bsz, seq // tl),
        in_specs=[row(SSD_INNER), row(SSD_INNER), row(D_MODEL)] + [_const_spec(c.shape) for c in consts],
        out_specs=row(D_MODEL),
        out_shape=jax.ShapeDtypeStruct((bsz, seq, D_MODEL), F32),
        compiler_params=_params(2),
        name="ssd_out",
    )(y, z, x, *consts)


def _step_conv(pre, c_refs, cw_ref, cb_ref):
    width = len(c_refs) + 1
    acc = cb_ref[...] + c_refs[0][...] * cw_ref[0:1, :]
    for k in range(1, width - 1):
        acc = acc + c_refs[k][...] * cw_ref[k:k + 1, :]
    return acc + pre * cw_ref[width - 1:width, :]


def _gdn_in_step_kernel(x_ref, c0_ref, c1_ref, c2_ref, wqkv_ref, wz_ref, wba_ref, cw_ref, cb_ref, alog_ref, dtb_ref,
                        q_ref, k_ref, v_ref, z_ref, slab_ref, pre_ref):
    xb = x_ref[...].astype(BF16)
    n = xb.shape[0]
    lane = lax.broadcasted_iota(jnp.int32, (n, LANES), 1)
    ba = jnp.dot(xb, wba_ref[...], preferred_element_type=F32)
    g = -jnp.exp(alog_ref[...]) * _softplus(ba + dtb_ref[...])
    slab_ref[...] = jnp.where(lane < GDN_HEADS, jax.nn.sigmoid(ba), g)
    z_ref[...] = jnp.dot(xb, wz_ref[...], preferred_element_type=F32)
    pre = jnp.dot(xb, wqkv_ref[...], preferred_element_type=F32)
    pre_ref[...] = pre
    act = _silu(_step_conv(pre, (c0_ref, c1_ref, c2_ref), cw_ref, cb_ref))
    for h in range(GDN_HEADS):
        hs = slice(h * GDN_DK, (h + 1) * GDN_DK)
        q_ref[:, hs] = _l2norm(act[:, hs]) * (GDN_DK ** -0.5)
        k_ref[:, hs] = _l2norm(act[:, GDN_QK + h * GDN_DK:GDN_QK + (h + 1) * GDN_DK])
    v_ref[...] = act[:, 2 * GDN_QK:]


def _gdn_in_step(x, cache, p):
    n = x.shape[0]
    ff = lambda w: jax.ShapeDtypeStruct((n, w), F32)
    return pl.pallas_call(
        _gdn_in_step_kernel,
        out_shape=(ff(GDN_QK), ff(GDN_QK), ff(GDN_VD), ff(GDN_VD), ff(LANES), ff(GDN_CONV_DIM)),
        compiler_params=pltpu.CompilerParams(vmem_limit_bytes=VMEM_LIMIT),
        name="gdn_in_step",
    )(x, cache[:, 0], cache[:, 1], cache[:, 2], p["wqkv"], p["wz"], p["wba"], p["cw"], p["cb"], p["alog"], p["dtb"])


def _gdn_step_kernel(q_ref, k_ref, v_ref, slab_ref, s_ref, o_ref, sout_ref, *, tb):
    zpad = jnp.zeros((LANES - GDN_HEADS, GDN_DK), F32)

    def token_body(t, carry):
        kt = jnp.concatenate([k_ref[t], zpad], axis=0).T
        qt = jnp.concatenate([q_ref[t], zpad], axis=0).T
        v = v_ref[t]
        sl = slab_ref[pl.ds(t, 1), :]
        alpha = jnp.exp(sl)
        for h in range(GDN_HEADS):
            s = s_ref[t, h]
            kc = _col(kt, h)
            a_h = alpha[:, GDN_HEADS + h:GDN_HEADS + h + 1]
            ks = jnp.sum(s * kc, axis=0, keepdims=True)
            v_new = sl[:, h:h + 1] * (v[h:h + 1, :] - a_h * ks)
            s_new = s * a_h + kc * v_new
            sout_ref[t, h] = s_new
            o_ref[t, h:h + 1, :] = jnp.sum(s_new * _col(qt, h), axis=0, keepdims=True)
        return carry

    lax.fori_loop(0, tb, token_body, 0)


def _gdn_step(q, k, v, slab, state, tb):
    n = q.shape[0]
    q3, k3, v3 = (t.reshape(n, GDN_HEADS, GDN_DK) for t in (q, k, v))
    tok = pl.BlockSpec((tb, GDN_HEADS, GDN_DK), lambda i: (i, 0, 0))
    st = pl.BlockSpec((tb, GDN_HEADS, GDN_DK, GDN_DV), lambda i: (i, 0, 0, 0))
    o, s_new = pl.pallas_call(
        functools.partial(_gdn_step_kernel, tb=tb),
        grid=(n // tb,),
        in_specs=[tok, tok, tok, pl.BlockSpec((tb, LANES), lambda i: (i, 0)), st],
        out_specs=(tok, st),
        out_shape=(jax.ShapeDtypeStruct((n, GDN_HEADS, GDN_DV), F32), jax.ShapeDtypeStruct(state.shape, F32)),
        compiler_params=_params(1),
        name="gdn_step",
    )(q3, k3, v3, slab, state)
    return o.reshape(n, GDN_VD), s_new


def _ffn_step_kernel(x_ref, c0_ref, c1_ref, wup_ref, cw_ref, cb_ref, wdown_ref, g_ref, b_ref, y_ref, pre_ref):
    x = x_ref[...]
    xb = x.astype(BF16)
    gate = jnp.dot(xb, wup_ref[:, 0:D_FF], preferred_element_type=F32)
    val = jnp.dot(xb, wup_ref[:, D_FF:2 * D_FF], preferred_element_type=F32)
    pre_ref[...] = gate
    hid = _silu(_step_conv(gate, (c0_ref, c1_ref), cw_ref, cb_ref)) * val
    y_ref[...] = _layer_norm(DN_ALPHA * x + jnp.dot(hid.astype(BF16), wdown_ref[...], preferred_element_type=F32),
                             g_ref[...], b_ref[...])


def _ffn_step(x, cache, p, ln_g, ln_b):
    n = x.shape[0]
    return pl.pallas_call(
        _ffn_step_kernel,
        out_shape=(jax.ShapeDtypeStruct((n, D_MODEL), F32), jax.ShapeDtypeStruct((n, D_FF), F32)),
        compiler_params=pltpu.CompilerParams(vmem_limit_bytes=VMEM_LIMIT),
        name="ffn_step",
    )(x, cache[:, 0], cache[:, 1], p["wup"], p["cw"], p["cb"], p["wdown"], ln_g, ln_b)


def _ssd_in_step_kernel(x_ref, c0_ref, c1_ref, c2_ref, wz_ref, wxbc_ref, wdt_ref, cw_ref, cb_ref, a_ref, dtb_ref,
                        edt_ref, eac_ref, z_ref, xs_ref, bm_ref, cm_ref, xdt_ref, dec_ref, pre_ref):
    xb = x_ref[...].astype(BF16)
    n = xb.shape[0]
    lane = lax.broadcasted_iota(jnp.int32, (n, LANES), 1)
    dt = _softplus(jnp.dot(xb, wdt_ref[...], preferred_element_type=F32) + dtb_ref[...])
    dt = jnp.where(lane < 2 * SSD_HEADS, dt, 0.0)
    slab = jnp.where(lane < SSD_HEADS, dt, dt * a_ref[...])
    z_ref[...] = jnp.dot(xb, wz_ref[...], preferred_element_type=F32)
    pre = jnp.dot(xb, wxbc_ref[...], preferred_element_type=F32)
    pre_ref[...] = pre
    act = _silu(_step_conv(pre, (c0_ref, c1_ref, c2_ref), cw_ref, cb_ref))
    xs = act[:, 0:SSD_INNER]
    xs_ref[...] = xs
    bm_ref[...] = act[:, SSD_INNER:SSD_INNER + SSD_GN]
    cm_ref[...] = act[:, SSD_INNER + SSD_GN:]
    xdt_ref[...] = xs * _dot_exact_rhs(slab, edt_ref[...])
    dec_ref[...] = jnp.exp(_dot_exact_rhs(slab, eac_ref[...]))


def _ssd_in_step(x, cache, p):
    n = x.shape[0]
    ff = lambda w: jax.ShapeDtypeStruct((n, w), F32)
    return pl.pallas_call(
        _ssd_in_step_kernel,
        out_shape=(ff(SSD_INNER), ff(SSD_INNER), ff(SSD_GN), ff(SSD_GN), ff(SSD_INNER), ff(SSD_INNER),
                   ff(SSD_CONV_DIM)),
        compiler_params=pltpu.CompilerParams(vmem_limit_bytes=VMEM_LIMIT),
        name="ssd_in_step",
    )(x, cache[:, 0], cache[:, 1], cache[:, 2], p["wz"], p["wxbc"], p["wdt"], p["cw"], p["cb"], p["a"], p["dtb"],
      p["edt"], p["eac"])


def _ssd_step_kernel(xs_ref, bm_ref, cm_ref, xdt_ref, dec_ref, dskip_ref, s_ref, y_ref, sout_ref, *, tb):
    hp = SSD_INNER
    rid = lax.broadcasted_iota(jnp.int32, (SUBLANES, hp), 0)
    gid = lax.broadcasted_iota(jnp.int32, (SUBLANES, hp), 1) // SSD_GROUP_W
    zpad_r = jnp.zeros((LANES - SUBLANES, hp), F32)

    def token_body(t, carry):
        row = pl.ds(t, 1)
        xdt = xdt_ref[row, :]
        dec = dec_ref[row, :]
        stack = jnp.where(rid == gid, jnp.broadcast_to(xdt, (SUBLANES, hp)), 0.0)
        stack = jnp.where(rid == SSD_GROUPS, jnp.broadcast_to(dec, (SUBLANES, hp)), stack)
        cols = jnp.concatenate([stack, zpad_r], axis=0).T
        s = s_ref[t].reshape(hp, SSD_STATE)
        bm = bm_ref[row, :]
        upd = jnp.concatenate(
            [_col(cols[g * SSD_GROUP_W:(g + 1) * SSD_GROUP_W, :], g) * bm[:, g * SSD_STATE:(g + 1) * SSD_STATE]
             for g in range(SSD_GROUPS)], axis=0)
        s_new = s * _col(cols, SSD_GROUPS) + upd
        sout_ref[t] = s_new.reshape(SSD_HEADS, SSD_HEADDIM, SSD_STATE)
        cm = cm_ref[row, :]
        cmat = jnp.concatenate([cm[:, g * SSD_STATE:(g + 1) * SSD_STATE] for g in range(SSD_GROUPS)]
                               + [jnp.zeros((SUBLANES - SSD_GROUPS, SSD_STATE), F32)], axis=0)
        yall = _dot_nt(cmat, s_new)
        y = jnp.sum(jnp.where(rid == gid, yall, 0.0), axis=0, keepdims=True)
        y_ref[row, :] = y + dskip_ref[...] * xs_ref[row, :]
        return carry

    lax.fori_loop(0, tb, token_body, 0)


def _ssd_step(xs, bm, cm, xdt, dec, state, p, tb):
    n = xs.shape[0]
    tok = lambda w: pl.BlockSpec((tb, w), lambda i: (i, 0))
    st = pl.BlockSpec((tb, SSD_HEADS, SSD_HEADDIM, SSD_STATE), lambda i: (i, 0, 0, 0))
    return pl.pallas_call(
        functools.partial(_ssd_step_kernel, tb=tb),
        grid=(n // tb,),
        in_specs=[tok(SSD_INNER), tok(SSD_GN), tok(SSD_GN), tok(SSD_INNER), tok(SSD_INNER),
                  _const_spec(p["dskip"].shape), st],
        out_specs=(tok(SSD_INNER), st),
        out_shape=(jax.ShapeDtypeStruct((n, SSD_INNER), F32), jax.ShapeDtypeStruct(state.shape, F32)),
        compiler_params=_params(1),
        name="ssd_step",
    )(xs, bm, cm, xdt, dec, p["dskip"], state)


def _row(v, width=None, offset=0):
    v = v.astype(F32).reshape(1, -1)
    if width is None:
        return v
    return jnp.pad(v, ((0, 0), (offset, width - offset - v.shape[1])))


def _prep_gdn(w_in, conv_w, conv_b, a_log, dt_bias, norm_w, w_out):
    ba = w_in[:, GDN_CONV_DIM + GDN_VD:]
    return {
        "wqkv": w_in[:, :GDN_CONV_DIM].astype(BF16),
        "wz": w_in[:, GDN_CONV_DIM:GDN_CONV_DIM + GDN_VD].astype(BF16),
        "wba": jnp.pad(ba, ((0, 0), (0, LANES - ba.shape[1]))).astype(BF16),
        "cw": conv_w.astype(F32),
        "cb": _row(conv_b),
        "alog": _row(a_log, LANES, GDN_HEADS),
        "dtb": _row(dt_bias, LANES, GDN_HEADS),
        "nw": _row(norm_w),
        "wout": w_out.astype(BF16),
    }


def _prep_ssd(w_in, conv_w, conv_b, a_log, dt_bias, d_skip, norm_w, w_out):
    wdt = w_in[:, SSD_INNER + SSD_CONV_DIM:]
    wdt2 = jnp.concatenate([wdt, wdt], axis=1)
    head_of_lane = jnp.arange(SSD_INNER) // SSD_HEADDIM
    sel = jnp.arange(LANES)[:, None]
    return {
        "wz": w_in[:, :SSD_INNER].astype(BF16),
        "wxbc": w_in[:, SSD_INNER:SSD_INNER + SSD_CONV_DIM].astype(BF16),
        "wdt": jnp.pad(wdt2, ((0, 0), (0, LANES - 2 * SSD_HEADS))).astype(BF16),
        "cw": conv_w.astype(F32),
        "cb": _row(conv_b),
        "a": _row(-jnp.exp(a_log.astype(F32)), LANES, SSD_HEADS),
        "dtb": _row(jnp.concatenate([dt_bias, dt_bias]), LANES, 0),
        "edt": (sel == head_of_lane[None, :]).astype(BF16),
        "eac": (sel == head_of_lane[None, :] + SSD_HEADS).astype(BF16),
        "dskip": _row(jnp.repeat(d_skip, SSD_HEADDIM)),
        "nw": _row(norm_w),
        "wout": w_out.astype(BF16),
    }


def _prep_ffn(w_up, conv_w, conv_b, w_down):
    return {"wup": w_up.astype(BF16), "cw": conv_w.astype(F32), "cb": _row(conv_b), "wdown": w_down.astype(BF16)}


def _prompt_trunk(x, gdn, ssd, ffn, ln, tl):
    q, qd, k, kbg, kd, vb, z, slab, slabt, gdn_cache = _gdn_in(x, gdn, tl)
    o, gdn_state = _gdn_scan(q, qd, k, kbg, kd, vb, slab, slabt, tl)
    x = _gdn_out(o, z, x, gdn, ln[0][0], ln[0][1], tl)
    x, ffn_cache0 = _ffn(x, ffn[0], ln[0][2], ln[0][3], tl)
    z, xs, bm, cm, slab, slabt, ssd_cache = _ssd_in(x, ssd, tl)
    y, ssd_state = _ssd_scan(xs, bm, cm, slab, slabt, ssd, tl)
    x = _ssd_out(y, z, x, ssd, ln[1][0], ln[1][1], tl)
    x, ffn_cache1 = _ffn(x, ffn[1], ln[1][2], ln[1][3], tl)
    return x, gdn_cache, gdn_state, ssd_cache, ssd_state, jnp.stack([ffn_cache0, ffn_cache1])


def _sample_trunk(x, gdn_cache, gdn_state, ssd_cache, ssd_state, ffn_cache, gdn, ssd, ffn, ln, tb):
    n = x.shape[0]
    q, k, v, z, slab, pre = _gdn_in_step(x, gdn_cache, gdn)
    gdn_cache_new = jnp.concatenate([gdn_cache[:, 1:], pre[:, None]], axis=1)
    o, gdn_state_new = _gdn_step(q, k, v, slab, gdn_state, tb)
    x = _gdn_out(o[None], z[None], x[None], gdn, ln[0][0], ln[0][1], n)[0]
    x, pre = _ffn_step(x, ffn_cache[0], ffn[0], ln[0][2], ln[0][3])
    ffn_cache0 = jnp.concatenate([ffn_cache[0][:, 1:], pre[:, None]], axis=1)
    z, xs, bm, cm, xdt, dec, pre = _ssd_in_step(x, ssd_cache, ssd)
    ssd_cache_new = jnp.concatenate([ssd_cache[:, 1:], pre[:, None]], axis=1)
    y, ssd_state_new = _ssd_step(xs, bm, cm, xdt, dec, ssd_state, ssd, tb)
    x = _ssd_out(y[None], z[None], x[None], ssd, ln[1][0], ln[1][1], n)[0]
    x, pre = _ffn_step(x, ffn_cache[1], ffn[1], ln[1][2], ln[1][3])
    ffn_cache1 = jnp.concatenate([ffn_cache[1][:, 1:], pre[:, None]], axis=1)
    return x, gdn_cache_new, gdn_state_new, ssd_cache_new, ssd_state_new, jnp.stack([ffn_cache0, ffn_cache1])


PROMPT_TILE = 512
SAMPLE_TOKENS = 8


def kernel(x_prompt, x_sample, cache_gdn_conv, state_gdn, cache_ssd_conv, state_ssd, cache_ffn_conv, gdn_w_in, gdn_conv_w, gdn_conv_b, gdn_a_log, gdn_dt_bias, gdn_norm_w, gdn_w_out, ssd_w_in, ssd_conv_w, ssd_conv_b, ssd_a_log, ssd_dt_bias, ssd_d, ssd_norm_w, ssd_w_out, ffn_w_up, ffn_conv_w, ffn_conv_b, ffn_w_down, ln1_g, ln1_b, ln2_g, ln2_b):
    gdn = _prep_gdn(gdn_w_in[0], gdn_conv_w[0], gdn_conv_b[0], gdn_a_log[0], gdn_dt_bias[0], gdn_norm_w[0],
                    gdn_w_out[0])
    ssd = _prep_ssd(ssd_w_in[0], ssd_conv_w[0], ssd_conv_b[0], ssd_a_log[0], ssd_dt_bias[0], ssd_d[0],
                    ssd_norm_w[0], ssd_w_out[0])
    ffn = [_prep_ffn(ffn_w_up[i], ffn_conv_w[i], ffn_conv_b[i], ffn_w_down[i]) for i in range(DEPTH)]
    ln = [(_row(ln1_g[i]), _row(ln1_b[i]), _row(ln2_g[i]), _row(ln2_b[i])) for i in range(DEPTH)]

    tl = min(PROMPT_TILE, x_prompt.shape[1])
    y_p, gcp, gsp, scp, ssp, fcp = _prompt_trunk(x_prompt, gdn, ssd, ffn, ln, tl)
    y_s, gcs, gss, scs, sss, fcs = _sample_trunk(
        x_sample[:, 0], cache_gdn_conv[0], state_gdn[0], cache_ssd_conv[0], state_ssd[0], cache_ffn_conv,
        gdn, ssd, ffn, ln, min(SAMPLE_TOKENS, x_sample.shape[0]))
    return (y_p, y_s[:, None], gcp[None], gcs[None], gsp[None], gss[None], scp[None], scs[None],
            ssp[None], sss[None], fcp, fcs)
```

```python
import functools

import jax
import jax.numpy as jnp
from jax import lax
from jax.experimental import pallas as pl
from jax.experimental.pallas import tpu as pltpu

F32 = jnp.float32
BF16 = jnp.bfloat16

D_MODEL = 1024
DEPTH = 2
CONV_W = 4
CHUNK = 64

GDN_HEADS = 8
GDN_DK = 128
GDN_DV = 128
GDN_QK = GDN_HEADS * GDN_DK
GDN_VD = GDN_HEADS * GDN_DV
GDN_CONV_DIM = 2 * GDN_QK + GDN_VD

SSD_INNER = 2 * D_MODEL
SSD_HEADDIM = 64
SSD_HEADS = SSD_INNER // SSD_HEADDIM
SSD_GROUPS = 4
SSD_STATE = 128
SSD_HPG = SSD_HEADS // SSD_GROUPS
SSD_GN = SSD_GROUPS * SSD_STATE
SSD_CONV_DIM = SSD_INNER + 2 * SSD_GN
SSD_GROUP_W = SSD_INNER // SSD_GROUPS

D_FF = 2816
FFN_CONV_W = 3
FFN_BLOCK = D_FF // 2

DN_ALPHA = (2 * DEPTH) ** 0.25
LN_EPS = 1e-5
RMS_EPS = 1e-6
L2_EPS = 1e-6

LANES = 128
SUBLANES = 8
VMEM_LIMIT = 56 * 1024 * 1024

NEG_BIG = -1e30


def _dot_nt(a, b):
    return lax.dot_general(a.astype(BF16), b.astype(BF16), (((1,), (1,)), ((), ())),
                           preferred_element_type=F32)


def _dot_tn(a, b):
    return lax.dot_general(a.astype(BF16), b.astype(BF16), (((0,), (0,)), ((), ())),
                           preferred_element_type=F32)


def _split3(x):
    hi = x.astype(BF16)
    r1 = x - hi.astype(F32)
    mid = r1.astype(BF16)
    lo = (r1 - mid.astype(F32)).astype(BF16)
    return hi, mid, lo


def _dot_exact_rhs(x, e):
    hi, mid, lo = _split3(x)
    return (jnp.dot(hi, e, preferred_element_type=F32) + jnp.dot(mid, e, preferred_element_type=F32)
            + jnp.dot(lo, e, preferred_element_type=F32))


def _silu(x):
    return x * jax.nn.sigmoid(x)


def _softplus(x):
    return jnp.maximum(x, 0.0) + jnp.log1p(jnp.exp(-jnp.abs(x)))


def _layer_norm(x, g, b):
    mu = jnp.mean(x, axis=-1, keepdims=True)
    xc = x - mu
    var = jnp.mean(xc * xc, axis=-1, keepdims=True)
    return xc * lax.rsqrt(var + LN_EPS) * g + b


def _l2norm(t):
    return t * lax.rsqrt(jnp.sum(t * t, axis=-1, keepdims=True) + L2_EPS)


def _col(a, i):
    return a[:, i:i + 1]


def _const_spec(shape):
    nd = len(shape)
    return pl.BlockSpec(shape, lambda *_: (0,) * nd)


def _params(n_grid):
    return pltpu.CompilerParams(dimension_semantics=("arbitrary",) * n_grid, vmem_limit_bytes=VMEM_LIMIT)


def _chunk_cumsum_matrix(tl):
    r = jnp.arange(tl)
    same = (r[:, None] // CHUNK) == (r[None, :] // CHUNK)
    return (same & (r[:, None] <= r[None, :])).astype(BF16)


def _gate_slabs(pre, n_keep, triu, slab_ref, slabt_ref, tl):
    pre_t = pre.T[0:2 * n_keep, :]
    cum_t = _dot_exact_rhs(pre_t, triu)
    row = lax.broadcasted_iota(jnp.int32, pre_t.shape, 0)
    st = jnp.where(row < n_keep, pre_t, cum_t)
    for c in range(tl // CHUNK):
        slabt_ref[c] = st[:, c * CHUNK:(c + 1) * CHUNK]
    slab_ref[...] = jnp.concatenate([st, jnp.zeros((LANES - 2 * n_keep, tl), F32)], axis=0).T


def _causal_conv(xx_ref, cw_ref, cb_ref, cs, tl, width):
    base = SUBLANES - (width - 1)
    acc = cb_ref[:, cs] + xx_ref[base:base + tl, cs] * cw_ref[0:1, cs]
    for k in range(1, width):
        acc = acc + xx_ref[base + k:base + k + tl, cs] * cw_ref[k:k + 1, cs]
    return acc


def _gdn_in_kernel(x_ref, wqkv_ref, wz_ref, wba_ref, cw_ref, cb_ref, alog_ref, dtb_ref, triu_ref,
                   q_ref, k_ref, v_ref, z_ref, slab_ref, slabt_ref, cache_ref, xx_ref, *, tl):
    l = pl.program_id(1)
    xb = x_ref[...].astype(BF16)
    lane = lax.broadcasted_iota(jnp.int32, (tl, LANES), 1)

    ba = jnp.dot(xb, wba_ref[...], preferred_element_type=F32)
    g = -jnp.exp(alog_ref[...]) * _softplus(ba + dtb_ref[...])
    _gate_slabs(jnp.where(lane < GDN_HEADS, jax.nn.sigmoid(ba), g), GDN_HEADS, triu_ref[...], slab_ref, slabt_ref, tl)

    z_ref[...] = jnp.dot(xb, wz_ref[...], preferred_element_type=F32)

    @pl.when(l == 0)
    def _():
        xx_ref[0:SUBLANES, :] = jnp.zeros((SUBLANES, GDN_CONV_DIM), F32)

    for j in range(3):
        cs = slice(j * GDN_QK, (j + 1) * GDN_QK)
        xx_ref[SUBLANES:SUBLANES + tl, cs] = jnp.dot(xb, wqkv_ref[:, cs], preferred_element_type=F32)
    cache_ref[...] = xx_ref[SUBLANES + tl - (CONV_W - 1):SUBLANES + tl, :]

    for j in range(3):
        cs = slice(j * GDN_QK, (j + 1) * GDN_QK)
        act = _silu(_causal_conv(xx_ref, cw_ref, cb_ref, cs, tl, CONV_W))
        if j == 2:
            v_ref[...] = act
        else:
            for h in range(GDN_HEADS):
                hs = slice(h * GDN_DK, (h + 1) * GDN_DK)
                if j == 0:
                    q_ref[:, hs] = (_l2norm(act[:, hs]) * (GDN_DK ** -0.5)).astype(BF16)
                else:
                    k_ref[:, hs] = _l2norm(act[:, hs]).astype(BF16)

    xx_ref[0:SUBLANES, :] = xx_ref[tl:tl + SUBLANES, :]


def _gdn_in(x, p, tl):
    bsz, seq, _ = x.shape
    nl = seq // tl
    triu = _chunk_cumsum_matrix(tl)
    row = lambda w: pl.BlockSpec((None, tl, w), lambda b, l: (b, l, 0))
    bf = lambda w: jax.ShapeDtypeStruct((bsz, seq, w), BF16)
    ff = lambda w: jax.ShapeDtypeStruct((bsz, seq, w), F32)
    out_shape = (bf(GDN_QK), bf(GDN_QK), ff(GDN_VD), ff(GDN_VD), ff(LANES),
                 jax.ShapeDtypeStruct((bsz, seq // CHUNK, 2 * GDN_HEADS, CHUNK), F32),
                 jax.ShapeDtypeStruct((bsz, CONV_W - 1, GDN_CONV_DIM), F32))
    out_specs = (row(GDN_QK), row(GDN_QK), row(GDN_VD), row(GDN_VD), row(LANES),
                 pl.BlockSpec((None, tl // CHUNK, 2 * GDN_HEADS, CHUNK), lambda b, l: (b, l, 0, 0)),
                 pl.BlockSpec((None, CONV_W - 1, GDN_CONV_DIM), lambda b, l: (b, 0, 0)))
    consts = (p["wqkv"], p["wz"], p["wba"], p["cw"], p["cb"], p["alog"], p["dtb"], triu)
    return pl.pallas_call(
        functools.partial(_gdn_in_kernel, tl=tl),
        grid=(bsz, nl),
        in_specs=[row(D_MODEL)] + [_const_spec(c.shape) for c in consts],
        out_specs=out_specs,
        out_shape=out_shape,
        scratch_shapes=[pltpu.VMEM((tl + SUBLANES, GDN_CONV_DIM), F32)],
        compiler_params=_params(2),
        name="gdn_in",
    )(x, *consts)


def _split2(x):
    hi = x.astype(BF16)
    return hi, (x - hi.astype(F32)).astype(BF16)


def _pair_blockdiag(x, left):
    z = jnp.zeros_like(x)
    return jnp.concatenate([jnp.where(left, x, z), jnp.where(left, z, x)], axis=0)


def _lane_blockdiag(a, b):
    z = jnp.zeros_like(a)
    return jnp.concatenate([jnp.concatenate([a, z], axis=1), jnp.concatenate([z, b], axis=1)], axis=0)


def _pair_matmul(lhs_parts, w_parts, left):
    n = len(lhs_parts)
    his = [p[0] for p in lhs_parts]
    los = [p[1] for p in lhs_parts]
    r1 = jnp.dot(jnp.concatenate(his + los, axis=0), _pair_blockdiag(w_parts[0], left), preferred_element_type=F32)
    r2 = jnp.dot(jnp.concatenate(his, axis=0) if n > 1 else his[0], _pair_blockdiag(w_parts[1], left),
                 preferred_element_type=F32)
    c = CHUNK
    return [r1[i * c:(i + 1) * c] + r1[(n + i) * c:(n + i + 1) * c] + r2[i * c:(i + 1) * c] for i in range(n)]


def _gdn_prep_stages(c, slot, k_ref, q_ref, slab_ref, slabt_ref, tinv_ref, qkg_ref, masks):
    incl, strict, eye2, left = masks
    npair = GDN_HEADS // 2
    rows = pl.ds(pl.multiple_of(c * CHUNK, CHUNK), CHUNK)
    sl = slab_ref[rows, :]
    st = slabt_ref[c]
    ns = []
    for p in range(npair):
        ha, hb = 2 * p, 2 * p + 1
        ka = k_ref[rows, ha * GDN_DK:(ha + 1) * GDN_DK]
        kb = k_ref[rows, hb * GDN_DK:(hb + 1) * GDN_DK]
        kk = jnp.concatenate([_dot_nt(ka, ka), _dot_nt(kb, kb)], axis=1)
        qk = jnp.concatenate([_dot_nt(q_ref[rows, ha * GDN_DK:(ha + 1) * GDN_DK], ka),
                              _dot_nt(q_ref[rows, hb * GDN_DK:(hb + 1) * GDN_DK], kb)], axis=1)
        gcol = jnp.where(left, _col(sl, GDN_HEADS + ha), _col(sl, GDN_HEADS + hb))
        bcol = jnp.where(left, _col(sl, ha), _col(sl, hb))
        grow = jnp.concatenate([st[GDN_HEADS + ha:GDN_HEADS + ha + 1, :], st[GDN_HEADS + hb:GDN_HEADS + hb + 1, :]],
                               axis=1)
        gam = jnp.exp(jnp.where(incl, gcol - grow, NEG_BIG))
        qkg_ref[slot, p] = (qk * gam).astype(BF16)
        ns.append(jnp.where(strict, kk * gam * bcol, 0.0) * -1.0)
    yield
    ps = [eye2 + n for n in ns]
    nparts = [_split2(n) for n in ns]
    ns = [_pair_matmul([nparts[p]], nparts[p], left)[0] for p in range(npair)]
    yield
    k = 2
    while 2 * k < CHUNK:
        nparts = [_split2(n) for n in ns]
        outs = [_pair_matmul([_split2(ps[p]), nparts[p]], nparts[p], left) for p in range(npair)]
        ps = [ps[p] + outs[p][0] for p in range(npair)]
        ns = [outs[p][1] for p in range(npair)]
        k *= 2
        yield
    for p in range(npair):
        t_inv = ps[p] + _pair_matmul([_split2(ps[p])], _split2(ns[p]), left)[0]
        tinv_ref[slot, p] = t_inv.astype(BF16)
    yield


def _gdn_state_stages(c, slot, q_ref, k_ref, v_ref, slab_ref, slabt_ref, tinv_ref, qkg_ref, o_ref, s_ref):
    npair = GDN_HEADS // 2
    rows = pl.ds(pl.multiple_of(c * CHUNK, CHUNK), CHUNK)
    sl = slab_ref[rows, :]
    st = slabt_ref[c]
    eg = jnp.exp(sl)
    ed = jnp.exp(sl[CHUNK - 1:CHUNK, :] - sl)
    hcols = lambda h: slice(h * GDN_DK, (h + 1) * GDN_DK)
    r1 = []
    for h in range(GDN_HEADS):
        lhs = jnp.concatenate([k_ref[rows, hcols(h)], q_ref[rows, hcols(h)]], axis=0)
        r1.append(jnp.dot(lhs, s_ref[h].astype(BF16), preferred_element_type=F32))
    yield
    egc = [_col(eg, GDN_HEADS + h) for h in range(GDN_HEADS)]
    rhs = [(_col(sl, h) * (v_ref[rows, hcols(h)] - egc[h] * r1[h][0:CHUNK])).astype(BF16) for h in range(GDN_HEADS)]
    v_new = [jnp.dot(tinv_ref[slot, p], _lane_blockdiag(rhs[2 * p], rhs[2 * p + 1]), preferred_element_type=F32)
             for p in range(npair)]
    yield
    for p in range(npair):
        va, vb = v_new[p][:, 0:GDN_DV], v_new[p][:, GDN_DV:2 * GDN_DV]
        qs = jnp.concatenate([egc[2 * p] * r1[2 * p][CHUNK:2 * CHUNK],
                              egc[2 * p + 1] * r1[2 * p + 1][CHUNK:2 * CHUNK]], axis=1)
        o_ref[rows, 2 * p * GDN_DV:(2 * p + 2) * GDN_DV] = qs + jnp.dot(
            qkg_ref[slot, p], _lane_blockdiag(va.astype(BF16), vb.astype(BF16)), preferred_element_type=F32)
        for h, vh in ((2 * p, va), (2 * p + 1, vb)):
            decay = jnp.exp(st[GDN_HEADS + h:GDN_HEADS + h + 1, CHUNK - 1:CHUNK])
            s_ref[h] = s_ref[h] * decay + _dot_tn(k_ref[rows, hcols(h)], vh * _col(ed, GDN_HEADS + h))
    yield


def _interleave(order, streams):
    for ch in order:
        next(streams[ch])


def _gdn_scan_kernel(q_ref, k_ref, v_ref, slab_ref, slabt_ref, o_ref, sout_ref, s_ref, tinv_ref, qkg_ref, *, tl):
    l = pl.program_id(1)
    nchunk = tl // CHUNK

    @pl.when(l == 0)
    def _():
        s_ref[...] = jnp.zeros(s_ref.shape, F32)

    ri = lax.broadcasted_iota(jnp.int32, (CHUNK, 2 * CHUNK), 0)
    li = lax.broadcasted_iota(jnp.int32, (CHUNK, 2 * CHUNK), 1)
    ci = li & (CHUNK - 1)
    masks = (ri >= ci, ri > ci, (ri == ci).astype(F32), li < CHUNK)

    prep = functools.partial(_gdn_prep_stages, k_ref=k_ref, q_ref=q_ref, slab_ref=slab_ref, slabt_ref=slabt_ref,
                             tinv_ref=tinv_ref, qkg_ref=qkg_ref, masks=masks)
    state = functools.partial(_gdn_state_stages, q_ref=q_ref, k_ref=k_ref, v_ref=v_ref, slab_ref=slab_ref,
                              slabt_ref=slabt_ref, tinv_ref=tinv_ref, qkg_ref=qkg_ref, o_ref=o_ref, s_ref=s_ref)

    _interleave("A" * 7, {"A": prep(0, 0)})

    def chunk_body(c, carry):
        slot = c & 1
        nxt = jnp.minimum(c + 1, nchunk - 1)
        _interleave("ABAABAABAA", {"A": prep(nxt, 1 - slot), "B": state(c, slot)})
        return carry

    lax.fori_loop(0, nchunk, chunk_body, 0)

    @pl.when(l == pl.num_programs(1) - 1)
    def _():
        sout_ref[...] = s_ref[...]


def _gdn_scan(q, k, v, slab, slabt, tl):
    bsz, seq, _ = q.shape
    row = lambda w: pl.BlockSpec((None, tl, w), lambda b, l: (b, l, 0))
    pair_scratch = pltpu.VMEM((2, GDN_HEADS // 2, CHUNK, 2 * CHUNK), BF16)
    return pl.pallas_call(
        functools.partial(_gdn_scan_kernel, tl=tl),
        grid=(bsz, seq // tl),
        in_specs=[row(GDN_QK), row(GDN_QK), row(GDN_VD), row(LANES),
                  pl.BlockSpec((None, tl // CHUNK, 2 * GDN_HEADS, CHUNK), lambda b, l: (b, l, 0, 0))],
        out_specs=(row(GDN_VD), pl.BlockSpec((None, GDN_HEADS, GDN_DK, GDN_DV), lambda b, l: (b, 0, 0, 0))),
        out_shape=(jax.ShapeDtypeStruct((bsz, seq, GDN_VD), F32),
                   jax.ShapeDtypeStruct((bsz, GDN_HEADS, GDN_DK, GDN_DV), F32)),
        scratch_shapes=[pltpu.VMEM((GDN_HEADS, GDN_DK, GDN_DV), F32), pair_scratch, pair_scratch],
        compiler_params=_params(2),
        name="gdn_scan",
    )(q, k, v, slab, slabt)


def _gdn_out_kernel(o_ref, z_ref, x_ref, nw_ref, wout_ref, g_ref, b_ref, y_ref):
    o = o_ref[...]
    z = z_ref[...]
    parts = []
    for h in range(GDN_HEADS):
        hs = slice(h * GDN_DV, (h + 1) * GDN_DV)
        oh = o[:, hs]
        rn = oh * lax.rsqrt(jnp.mean(oh * oh, axis=-1, keepdims=True) + RMS_EPS) * nw_ref[...]
        parts.append((rn * _silu(z[:, hs])).astype(BF16))
    mix = jnp.dot(jnp.concatenate(parts, axis=1), wout_ref[...], preferred_element_type=F32)
    y_ref[...] = _layer_norm(DN_ALPHA * x_ref[...] + mix, g_ref[...], b_ref[...])


def _gdn_out(o, z, x, p, ln_g, ln_b, tl):
    bsz, seq, _ = x.shape
    row = lambda w: pl.BlockSpec((None, tl, w), lambda b, l: (b, l, 0))
    consts = (p["nw"], p["wout"], ln_g, ln_b)
    return pl.pallas_call(
        _gdn_out_kernel,
        grid=(bsz, seq // tl),
        in_specs=[row(GDN_VD), row(GDN_VD), row(D_MODEL)] + [_const_spec(c.shape) for c in consts],
        out_specs=row(D_MODEL),
        out_shape=jax.ShapeDtypeStruct((bsz, seq, D_MODEL), F32),
        compiler_params=_params(2),
        name="gdn_out",
    )(o, z, x, *consts)


def _ffn_kernel(x_ref, wup_ref, cw_ref, cb_ref, wdown_ref, g_ref, b_ref, y_ref, cache_ref, gg_ref, *, tl):
    l = pl.program_id(1)

    @pl.when(l == 0)
    def _():
        gg_ref[0:SUBLANES, :] = jnp.zeros((SUBLANES, D_FF), F32)

    x = x_ref[...]
    xb = x.astype(BF16)
    acc = jnp.zeros((tl, D_MODEL), F32)
    for j in range(D_FF // FFN_BLOCK):
        cs = slice(j * FFN_BLOCK, (j + 1) * FFN_BLOCK)
        vs = slice(D_FF + j * FFN_BLOCK, D_FF + (j + 1) * FFN_BLOCK)
        gg_ref[SUBLANES:SUBLANES + tl, cs] = jnp.dot(xb, wup_ref[:, cs], preferred_element_type=F32)
        val = jnp.dot(xb, wup_ref[:, vs], preferred_element_type=F32)
        hid = _silu(_causal_conv(gg_ref, cw_ref, cb_ref, cs, tl, FFN_CONV_W)) * val
        acc = acc + jnp.dot(hid.astype(BF16), wdown_ref[cs, :], preferred_element_type=F32)
    cache_ref[...] = gg_ref[SUBLANES + tl - (FFN_CONV_W - 1):SUBLANES + tl, :]
    gg_ref[0:SUBLANES, :] = gg_ref[tl:tl + SUBLANES, :]
    y_ref[...] = _layer_norm(DN_ALPHA * x + acc, g_ref[...], b_ref[...])


def _ffn(x, p, ln_g, ln_b, tl):
    bsz, seq, _ = x.shape
    row = lambda w: pl.BlockSpec((None, tl, w), lambda b, l: (b, l, 0))
    consts = (p["wup"], p["cw"], p["cb"], p["wdown"], ln_g, ln_b)
    return pl.pallas_call(
        functools.partial(_ffn_kernel, tl=tl),
        grid=(bsz, seq // tl),
        in_specs=[row(D_MODEL)] + [_const_spec(c.shape) for c in consts],
        out_specs=(row(D_MODEL), pl.BlockSpec((None, FFN_CONV_W - 1, D_FF), lambda b, l: (b, 0, 0))),
        out_shape=(jax.ShapeDtypeStruct((bsz, seq, D_MODEL), F32),
                   jax.ShapeDtypeStruct((bsz, FFN_CONV_W - 1, D_FF), F32)),
        scratch_shapes=[pltpu.VMEM((tl + SUBLANES, D_FF), F32)],
        compiler_params=_params(2),
        name="ffn",
    )(x, *consts)


def _ssd_in_kernel(x_ref, wz_ref, wxbc_ref, wdt_ref, cw_ref, cb_ref, a_ref, dtb_ref, triu_ref,
                   z_ref, xs_ref, bm_ref, cm_ref, slab_ref, slabt_ref, cache_ref, xx_ref, *, tl):
    l = pl.program_id(1)
    xb = x_ref[...].astype(BF16)
    lane = lax.broadcasted_iota(jnp.int32, (tl, LANES), 1)

    dt = _softplus(jnp.dot(xb, wdt_ref[...], preferred_element_type=F32) + dtb_ref[...])
    dt = jnp.where(lane < 2 * SSD_HEADS, dt, 0.0)
    _gate_slabs(jnp.where(lane < SSD_HEADS, dt, dt * a_ref[...]), SSD_HEADS, triu_ref[...], slab_ref, slabt_ref, tl)

    z_ref[...] = jnp.dot(xb, wz_ref[...], preferred_element_type=F32)

    @pl.when(l == 0)
    def _():
        xx_ref[0:SUBLANES, :] = jnp.zeros((SUBLANES, SSD_CONV_DIM), F32)

    nblk = SSD_CONV_DIM // D_MODEL
    for j in range(nblk):
        cs = slice(j * D_MODEL, (j + 1) * D_MODEL)
        xx_ref[SUBLANES:SUBLANES + tl, cs] = jnp.dot(xb, wxbc_ref[:, cs], preferred_element_type=F32)
    cache_ref[...] = xx_ref[SUBLANES + tl - (CONV_W - 1):SUBLANES + tl, :]

    for j in range(nblk):
        cs = slice(j * D_MODEL, (j + 1) * D_MODEL)
        act = _silu(_causal_conv(xx_ref, cw_ref, cb_ref, cs, tl, CONV_W))
        if j < 2:
            xs_ref[:, cs] = act
        else:
            bm_ref[...] = act[:, 0:SSD_GN].astype(BF16)
            cm_ref[...] = act[:, SSD_GN:2 * SSD_GN].astype(BF16)

    xx_ref[0:SUBLANES, :] = xx_ref[tl:tl + SUBLANES, :]


def _ssd_in(x, p, tl):
    bsz, seq, _ = x.shape
    triu = _chunk_cumsum_matrix(tl)
    row = lambda w: pl.BlockSpec((None, tl, w), lambda b, l: (b, l, 0))
    out_shape = (jax.ShapeDtypeStruct((bsz, seq, SSD_INNER), F32), jax.ShapeDtypeStruct((bsz, seq, SSD_INNER), F32),
                 jax.ShapeDtypeStruct((bsz, seq, SSD_GN), BF16), jax.ShapeDtypeStruct((bsz, seq, SSD_GN), BF16),
                 jax.ShapeDtypeStruct((bsz, seq, LANES), F32),
                 jax.ShapeDtypeStruct((bsz, seq // CHUNK, 2 * SSD_HEADS, CHUNK), F32),
                 jax.ShapeDtypeStruct((bsz, CONV_W - 1, SSD_CONV_DIM), F32))
    out_specs = (row(SSD_INNER), row(SSD_INNER), row(SSD_GN), row(SSD_GN), row(LANES),
                 pl.BlockSpec((None, tl // CHUNK, 2 * SSD_HEADS, CHUNK), lambda b, l: (b, l, 0, 0)),
                 pl.BlockSpec((None, CONV_W - 1, SSD_CONV_DIM), lambda b, l: (b, 0, 0)))
    consts = (p["wz"], p["wxbc"], p["wdt"], p["cw"], p["cb"], p["a"], p["dtb"], triu)
    return pl.pallas_call(
        functools.partial(_ssd_in_kernel, tl=tl),
        grid=(bsz, seq // tl),
        in_specs=[row(D_MODEL)] + [_const_spec(c.shape) for c in consts],
        out_specs=out_specs,
        out_shape=out_shape,
        scratch_shapes=[pltpu.VMEM((tl + SUBLANES, SSD_CONV_DIM), F32)],
        compiler_params=_params(2),
        name="ssd_in",
    )(x, *consts)


def _ssd_scan_kernel(xs_ref, bm_ref, cm_ref, slab_ref, slabt_ref, edt_ref, eac_ref, dskip_ref,
                     y_ref, sout_ref, st_ref, dtx_ref, acx_ref, *, tl):
    l = pl.program_id(1)

    @pl.when(l == 0)
    def _():
        st_ref[...] = jnp.zeros(st_ref.shape, F32)

    hi, mid, lo = _split3(slab_ref[...])
    dtx_ref[...] = jnp.dot(jnp.concatenate([hi, mid], axis=1), edt_ref[...], preferred_element_type=F32)
    acx_ref[...] = jnp.dot(jnp.concatenate([hi, mid, lo], axis=1), eac_ref[...], preferred_element_type=F32)

    ri = lax.broadcasted_iota(jnp.int32, (CHUNK, 2 * CHUNK), 0)
    li = lax.broadcasted_iota(jnp.int32, (CHUNK, 2 * CHUNK), 1)
    incl = ri >= (li & (CHUNK - 1))
    left = li < CHUNK

    def chunk_body(c, carry):
        rows = pl.ds(pl.multiple_of(c * CHUNK, CHUNK), CHUNK)
        sl = slab_ref[rows, :]
        st = slabt_ref[c]
        ac_x = acx_ref[rows, :]
        xs = xs_ref[rows, :]
        xdt = xs * dtx_ref[rows, :]
        alast_x = ac_x[CHUNK - 1:CHUNK, :]
        xdec = (xdt * jnp.exp(alast_x - ac_x)).astype(BF16)
        xdt_b = xdt.astype(BF16)
        ea = jnp.exp(ac_x)
        sdecay = jnp.exp(alast_x)
        cb2, y_off = [], []
        for g in range(SSD_GROUPS):
            gs = slice(g * SSD_GROUP_W, (g + 1) * SSD_GROUP_W)
            ns = slice(g * SSD_STATE, (g + 1) * SSD_STATE)
            cg = cm_ref[rows, ns]
            bg = bm_ref[rows, ns]
            cb2.append(_dot_nt(cg, jnp.concatenate([bg, bg], axis=0)))
            st_g = st_ref[:, gs]
            y_off.append(jnp.dot(cg, st_g.astype(BF16), preferred_element_type=F32))
            st_ref[:, gs] = st_g * sdecay[:, gs] + _dot_tn(bg, xdec[:, gs])
        for g in range(SSD_GROUPS):
            gs = slice(g * SSD_GROUP_W, (g + 1) * SSD_GROUP_W)
            diag = []
            for pp in range(SSD_HPG // 2):
                ha = g * SSD_HPG + 2 * pp
                hb = ha + 1
                acol = jnp.where(left, _col(sl, SSD_HEADS + ha), _col(sl, SSD_HEADS + hb))
                arow = jnp.concatenate([st[SSD_HEADS + ha:SSD_HEADS + ha + 1, :],
                                        st[SSD_HEADS + hb:SSD_HEADS + hb + 1, :]], axis=1)
                seg = jnp.exp(jnp.where(incl, acol - arow, NEG_BIG))
                w = _pair_blockdiag(xdt_b[:, ha * SSD_HEADDIM:(ha + 2) * SSD_HEADDIM], left)
                diag.append(jnp.dot((seg * cb2[g]).astype(BF16), w, preferred_element_type=F32))
            y_ref[rows, gs] = jnp.concatenate(diag, axis=1) + y_off[g] * ea[:, gs] + dskip_ref[:, gs] * xs[:, gs]
        return carry

    lax.fori_loop(0, tl // CHUNK, chunk_body, 0)

    @pl.when(l == pl.num_programs(1) - 1)
    def _():
        sout_ref[...] = st_ref[...].T.reshape(SSD_HEADS, SSD_HEADDIM, SSD_STATE)


def _ssd_scan(xs, bm, cm, slab, slabt, p, tl):
    bsz, seq, _ = xs.shape
    row = lambda w: pl.BlockSpec((None, tl, w), lambda b, l: (b, l, 0))
    consts = (jnp.tile(p["edt"], (2, 1)), jnp.tile(p["eac"], (3, 1)), p["dskip"])
    return pl.pallas_call(
        functools.partial(_ssd_scan_kernel, tl=tl),
        grid=(bsz, seq // tl),
        in_specs=[row(SSD_INNER), row(SSD_GN), row(SSD_GN), row(LANES),
                  pl.BlockSpec((None, tl // CHUNK, 2 * SSD_HEADS, CHUNK), lambda b, l: (b, l, 0, 0))]
                 + [_const_spec(c.shape) for c in consts],
        out_specs=(row(SSD_INNER),
                   pl.BlockSpec((None, SSD_HEADS, SSD_HEADDIM, SSD_STATE), lambda b, l: (b, 0, 0, 0))),
        out_shape=(jax.ShapeDtypeStruct((bsz, seq, SSD_INNER), F32),
                   jax.ShapeDtypeStruct((bsz, SSD_HEADS, SSD_HEADDIM, SSD_STATE), F32)),
        scratch_shapes=[pltpu.VMEM((SSD_STATE, SSD_INNER), F32), pltpu.VMEM((tl, SSD_INNER), F32),
                        pltpu.VMEM((tl, SSD_INNER), F32)],
        compiler_params=_params(2),
        name="ssd_scan",
    )(xs, bm, cm, slab, slabt, *consts)


def _ssd_out_kernel(y_ref, z_ref, x_ref, nw_ref, wout_ref, g_ref, b_ref, o_ref):
    t = y_ref[...] * _silu(z_ref[...])
    parts = []
    for g in range(SSD_GROUPS):
        gs = slice(g * SSD_GROUP_W, (g + 1) * SSD_GROUP_W)
        tg = t[:, gs]
        parts.append((tg * lax.rsqrt(jnp.mean(tg * tg, axis=-1, keepdims=True) + RMS_EPS) * nw_ref[:, gs]).astype(BF16))
    mix = jnp.dot(jnp.concatenate(parts, axis=1), wout_ref[...], preferred_element_type=F32)
    o_ref[...] = _layer_norm(DN_ALPHA * x_ref[...] + mix, g_ref[...], b_ref[...])


def _ssd_out(y, z, x, p, ln_g, ln_b, tl):
    bsz, seq, _ = x.shape
    row = lambda w: pl.BlockSpec((None, tl, w), lambda b, l: (b, l, 0))
    consts = (p["nw"], p["wout"], ln_g, ln_b)
    return pl.pallas_call(
        _ssd_out_kernel,
        grid=(bsz, seq // tl),
        in_specs=[row(SSD_INNER), row(SSD_INNER), row(D_MODEL)] + [_const_spec(c.shape) for c in consts],
        out_specs=row(D_MODEL),
        out_shape=jax.ShapeDtypeStruct((bsz, seq, D_MODEL), F32),
        compiler_params=_params(2),
        name="ssd_out",
    )(y, z, x, *consts)


def _step_conv(pre, c_refs, cw_ref, cb_ref):
    width = len(c_refs) + 1
    acc = cb_ref[...] + c_refs[0][...] * cw_ref[0:1, :]
    for k in range(1, width - 1):
        acc = acc + c_refs[k][...] * cw_ref[k:k + 1, :]
    return acc + pre * cw_ref[width - 1:width, :]


def _gdn_in_step_kernel(x_ref, c0_ref, c1_ref, c2_ref, wqkv_ref, wz_ref, wba_ref, cw_ref, cb_ref, alog_ref, dtb_ref,
                        q_ref, k_ref, v_ref, z_ref, slab_ref, pre_ref):
    xb = x_ref[...].astype(BF16)
    n = xb.shape[0]
    lane = lax.broadcasted_iota(jnp.int32, (n, LANES), 1)
    ba = jnp.dot(xb, wba_ref[...], preferred_element_type=F32)
    g = -jnp.exp(alog_ref[...]) * _softplus(ba + dtb_ref[...])
    slab_ref[...] = jnp.where(lane < GDN_HEADS, jax.nn.sigmoid(ba), g)
    z_ref[...] = jnp.dot(xb, wz_ref[...], preferred_element_type=F32)
    pre = jnp.dot(xb, wqkv_ref[...], preferred_element_type=F32)
    pre_ref[...] = pre
    act = _silu(_step_conv(pre, (c0_ref, c1_ref, c2_ref), cw_ref, cb_ref))
    for h in range(GDN_HEADS):
        hs = slice(h * GDN_DK, (h + 1) * GDN_DK)
        q_ref[:, hs] = _l2norm(act[:, hs]) * (GDN_DK ** -0.5)
        k_ref[:, hs] = _l2norm(act[:, GDN_QK + h * GDN_DK:GDN_QK + (h + 1) * GDN_DK])
    v_ref[...] = act[:, 2 * GDN_QK:]


def _gdn_in_step(x, cache, p):
    n = x.shape[0]
    ff = lambda w: jax.ShapeDtypeStruct((n, w), F32)
    return pl.pallas_call(
        _gdn_in_step_kernel,
        out_shape=(ff(GDN_QK), ff(GDN_QK), ff(GDN_VD), ff(GDN_VD), ff(LANES), ff(GDN_CONV_DIM)),
        compiler_params=pltpu.CompilerParams(vmem_limit_bytes=VMEM_LIMIT),
        name="gdn_in_step",
    )(x, cache[:, 0], cache[:, 1], cache[:, 2], p["wqkv"], p["wz"], p["wba"], p["cw"], p["cb"], p["alog"], p["dtb"])


def _gdn_step_kernel(q_ref, k_ref, v_ref, slab_ref, s_ref, o_ref, sout_ref, *, tb):
    zpad = jnp.zeros((LANES - GDN_HEADS, GDN_DK), F32)

    def token_body(t, carry):
        kt = jnp.concatenate([k_ref[t], zpad], axis=0).T
        qt = jnp.concatenate([q_ref[t], zpad], axis=0).T
        v = v_ref[t]
        sl = slab_ref[pl.ds(t, 1), :]
        alpha = jnp.exp(sl)
        for h in range(GDN_HEADS):
            s = s_ref[t, h]
            kc = _col(kt, h)
            a_h = alpha[:, GDN_HEADS + h:GDN_HEADS + h + 1]
            ks = jnp.sum(s * kc, axis=0, keepdims=True)
            v_new = sl[:, h:h + 1] * (v[h:h + 1, :] - a_h * ks)
            s_new = s * a_h + kc * v_new
            sout_ref[t, h] = s_new
            o_ref[t, h:h + 1, :] = jnp.sum(s_new * _col(qt, h), axis=0, keepdims=True)
        return carry

    lax.fori_loop(0, tb, token_body, 0)


def _gdn_step(q, k, v, slab, state, tb):
    n = q.shape[0]
    q3, k3, v3 = (t.reshape(n, GDN_HEADS, GDN_DK) for t in (q, k, v))
    tok = pl.BlockSpec((tb, GDN_HEADS, GDN_DK), lambda i: (i, 0, 0))
    st = pl.BlockSpec((tb, GDN_HEADS, GDN_DK, GDN_DV), lambda i: (i, 0, 0, 0))
    o, s_new = pl.pallas_call(
        functools.partial(_gdn_step_kernel, tb=tb),
        grid=(n // tb,),
        in_specs=[tok, tok, tok, pl.BlockSpec((tb, LANES), lambda i: (i, 0)), st],
        out_specs=(tok, st),
        out_shape=(jax.ShapeDtypeStruct((n, GDN_HEADS, GDN_DV), F32), jax.ShapeDtypeStruct(state.shape, F32)),
        compiler_params=_params(1),
        name="gdn_step",
    )(q3, k3, v3, slab, state)
    return o.reshape(n, GDN_VD), s_new


def _ffn_step_kernel(x_ref, c0_ref, c1_ref, wup_ref, cw_ref, cb_ref, wdown_ref, g_ref, b_ref, y_ref, pre_ref):
    x = x_ref[...]
    xb = x.astype(BF16)
    gate = jnp.dot(xb, wup_ref[:, 0:D_FF], preferred_element_type=F32)
    val = jnp.dot(xb, wup_ref[:, D_FF:2 * D_FF], preferred_element_type=F32)
    pre_ref[...] = gate
    hid = _silu(_step_conv(gate, (c0_ref, c1_ref), cw_ref, cb_ref)) * val
    y_ref[...] = _layer_norm(DN_ALPHA * x + jnp.dot(hid.astype(BF16), wdown_ref[...], preferred_element_type=F32),
                             g_ref[...], b_ref[...])


def _ffn_step(x, cache, p, ln_g, ln_b):
    n = x.shape[0]
    return pl.pallas_call(
        _ffn_step_kernel,
        out_shape=(jax.ShapeDtypeStruct((n, D_MODEL), F32), jax.ShapeDtypeStruct((n, D_FF), F32)),
        compiler_params=pltpu.CompilerParams(vmem_limit_bytes=VMEM_LIMIT),
        name="ffn_step",
    )(x, cache[:, 0], cache[:, 1], p["wup"], p["cw"], p["cb"], p["wdown"], ln_g, ln_b)


def _ssd_in_step_kernel(x_ref, c0_ref, c1_ref, c2_ref, wz_ref, wxbc_ref, wdt_ref, cw_ref, cb_ref, a_ref, dtb_ref,
                        edt_ref, eac_ref, z_ref, xs_ref, bm_ref, cm_ref, xdt_ref, dec_ref, pre_ref):
    xb = x_ref[...].astype(BF16)
    n = xb.shape[0]
    lane = lax.broadcasted_iota(jnp.int32, (n, LANES), 1)
    dt = _softplus(jnp.dot(xb, wdt_ref[...], preferred_element_type=F32) + dtb_ref[...])
    dt = jnp.where(lane < 2 * SSD_HEADS, dt, 0.0)
    slab = jnp.where(lane < SSD_HEADS, dt, dt * a_ref[...])
    z_ref[...] = jnp.dot(xb, wz_ref[...], preferred_element_type=F32)
    pre = jnp.dot(xb, wxbc_ref[...], preferred_element_type=F32)
    pre_ref[...] = pre
    act = _silu(_step_conv(pre, (c0_ref, c1_ref, c2_ref), cw_ref, cb_ref))
    xs = act[:, 0:SSD_INNER]
    xs_ref[...] = xs
    bm_ref[...] = act[:, SSD_INNER:SSD_INNER + SSD_GN]
    cm_ref[...] = act[:, SSD_INNER + SSD_GN:]
    xdt_ref[...] = xs * _dot_exact_rhs(slab, edt_ref[...])
    dec_ref[...] = jnp.exp(_dot_exact_rhs(slab, eac_ref[...]))


def _ssd_in_step(x, cache, p):
    n = x.shape[0]
    ff = lambda w: jax.ShapeDtypeStruct((n, w), F32)
    return pl.pallas_call(
        _ssd_in_step_kernel,
        out_shape=(ff(SSD_INNER), ff(SSD_INNER), ff(SSD_GN), ff(SSD_GN), ff(SSD_INNER), ff(SSD_INNER),
                   ff(SSD_CONV_DIM)),
        compiler_params=pltpu.CompilerParams(vmem_limit_bytes=VMEM_LIMIT),
        name="ssd_in_step",
    )(x, cache[:, 0], cache[:, 1], cache[:, 2], p["wz"], p["wxbc"], p["wdt"], p["cw"], p["cb"], p["a"], p["dtb"],
      p["edt"], p["eac"])


def _ssd_step_kernel(xs_ref, bm_ref, cm_ref, xdt_ref, dec_ref, dskip_ref, s_ref, y_ref, sout_ref, *, tb):
    hp = SSD_INNER
    rid = lax.broadcasted_iota(jnp.int32, (SUBLANES, hp), 0)
    gid = lax.broadcasted_iota(jnp.int32, (SUBLANES, hp), 1) // SSD_GROUP_W
    zpad_r = jnp.zeros((LANES - SUBLANES, hp), F32)

    def token_body(t, carry):
        row = pl.ds(t, 1)
        xdt = xdt_ref[row, :]
        dec = dec_ref[row, :]
        stack = jnp.where(rid == gid, jnp.broadcast_to(xdt, (SUBLANES, hp)), 0.0)
        stack = jnp.where(rid == SSD_GROUPS, jnp.broadcast_to(dec, (SUBLANES, hp)), stack)
        cols = jnp.concatenate([stack, zpad_r], axis=0).T
        s = s_ref[t].reshape(hp, SSD_STATE)
        bm = bm_ref[row, :]
        upd = jnp.concatenate(
            [_col(cols[g * SSD_GROUP_W:(g + 1) * SSD_GROUP_W, :], g) * bm[:, g * SSD_STATE:(g + 1) * SSD_STATE]
             for g in range(SSD_GROUPS)], axis=0)
        s_new = s * _col(cols, SSD_GROUPS) + upd
        sout_ref[t] = s_new.reshape(SSD_HEADS, SSD_HEADDIM, SSD_STATE)
        cm = cm_ref[row, :]
        cmat = jnp.concatenate([cm[:, g * SSD_STATE:(g + 1) * SSD_STATE] for g in range(SSD_GROUPS)]
                               + [jnp.zeros((SUBLANES - SSD_GROUPS, SSD_STATE), F32)], axis=0)
        yall = _dot_nt(cmat, s_new)
        y = jnp.sum(jnp.where(rid == gid, yall, 0.0), axis=0, keepdims=True)
        y_ref[row, :] = y + dskip_ref[...] * xs_ref[row, :]
        return carry

    lax.fori_loop(0, tb, token_body, 0)


def _ssd_step(xs, bm, cm, xdt, dec, state, p, tb):
    n = xs.shape[0]
    tok = lambda w: pl.BlockSpec((tb, w), lambda i: (i, 0))
    st = pl.BlockSpec((tb, SSD_HEADS, SSD_HEADDIM, SSD_STATE), lambda i: (i, 0, 0, 0))
    return pl.pallas_call(
        functools.partial(_ssd_step_kernel, tb=tb),
        grid=(n // tb,),
        in_specs=[tok(SSD_INNER), tok(SSD_GN), tok(SSD_GN), tok(SSD_INNER), tok(SSD_INNER),
                  _const_spec(p["dskip"].shape), st],
        out_specs=(tok(SSD_INNER), st),
        out_shape=(jax.ShapeDtypeStruct((n, SSD_INNER), F32), jax.ShapeDtypeStruct(state.shape, F32)),
        compiler_params=_params(1),
        name="ssd_step",
    )(xs, bm, cm, xdt, dec, p["dskip"], state)


def _row(v, width=None, offset=0):
    v = v.astype(F32).reshape(1, -1)
    if width is None:
        return v
    return jnp.pad(v, ((0, 0), (offset, width - offset - v.shape[1])))


def _prep_gdn(w_in, conv_w, conv_b, a_log, dt_bias, norm_w, w_out):
    ba = w_in[:, GDN_CONV_DIM + GDN_VD:]
    return {
        "wqkv": w_in[:, :GDN_CONV_DIM].astype(BF16),
        "wz": w_in[:, GDN_CONV_DIM:GDN_CONV_DIM + GDN_VD].astype(BF16),
        "wba": jnp.pad(ba, ((0, 0), (0, LANES - ba.shape[1]))).astype(BF16),
        "cw": conv_w.astype(F32),
        "cb": _row(conv_b),
        "alog": _row(a_log, LANES, GDN_HEADS),
        "dtb": _row(dt_bias, LANES, GDN_HEADS),
        "nw": _row(norm_w),
        "wout": w_out.astype(BF16),
    }


def _prep_ssd(w_in, conv_w, conv_b, a_log, dt_bias, d_skip, norm_w, w_out):
    wdt = w_in[:, SSD_INNER + SSD_CONV_DIM:]
    wdt2 = jnp.concatenate([wdt, wdt], axis=1)
    head_of_lane = jnp.arange(SSD_INNER) // SSD_HEADDIM
    sel = jnp.arange(LANES)[:, None]
    return {
        "wz": w_in[:, :SSD_INNER].astype(BF16),
        "wxbc": w_in[:, SSD_INNER:SSD_INNER + SSD_CONV_DIM].astype(BF16),
        "wdt": jnp.pad(wdt2, ((0, 0), (0, LANES - 2 * SSD_HEADS))).astype(BF16),
        "cw": conv_w.astype(F32),
        "cb": _row(conv_b),
        "a": _row(-jnp.exp(a_log.astype(F32)), LANES, SSD_HEADS),
        "dtb": _row(jnp.concatenate([dt_bias, dt_bias]), LANES, 0),
        "edt": (sel == head_of_lane[None, :]).astype(BF16),
        "eac": (sel == head_of_lane[None, :] + SSD_HEADS).astype(BF16),
        "dskip": _row(jnp.repeat(d_skip, SSD_HEADDIM)),
        "nw": _row(norm_w),
        "wout": w_out.astype(BF16),
    }


def _prep_ffn(w_up, conv_w, conv_b, w_down):
    return {"wup": w_up.astype(BF16), "cw": conv_w.astype(F32), "cb": _row(conv_b), "wdown": w_down.astype(BF16)}


def _prompt_trunk(x, gdn, ssd, ffn, ln, tl):
    q, k, v, z, slab, slabt, gdn_cache = _gdn_in(x, gdn, tl)
    o, gdn_state = _gdn_scan(q, k, v, slab, slabt, tl)
    x = _gdn_out(o, z, x, gdn, ln[0][0], ln[0][1], tl)
    x, ffn_cache0 = _ffn(x, ffn[0], ln[0][2], ln[0][3], tl)
    z, xs, bm, cm, slab, slabt, ssd_cache = _ssd_in(x, ssd, tl)
    y, ssd_state = _ssd_scan(xs, bm, cm, slab, slabt, ssd, tl)
    x = _ssd_out(y, z, x, ssd, ln[1][0], ln[1][1], tl)
    x, ffn_cache1 = _ffn(x, ffn[1], ln[1][2], ln[1][3], tl)
    return x, gdn_cache, gdn_state, ssd_cache, ssd_state, jnp.stack([ffn_cache0, ffn_cache1])


def _sample_trunk(x, gdn_cache, gdn_state, ssd_cache, ssd_state, ffn_cache, gdn, ssd, ffn, ln, tb):
    n = x.shape[0]
    q, k, v, z, slab, pre = _gdn_in_step(x, gdn_cache, gdn)
    gdn_cache_new = jnp.concatenate([gdn_cache[:, 1:], pre[:, None]], axis=1)
    o, gdn_state_new = _gdn_step(q, k, v, slab, gdn_state, tb)
    x = _gdn_out(o[None], z[None], x[None], gdn, ln[0][0], ln[0][1], n)[0]
    x, pre = _ffn_step(x, ffn_cache[0], ffn[0], ln[0][2], ln[0][3])
    ffn_cache0 = jnp.concatenate([ffn_cache[0][:, 1:], pre[:, None]], axis=1)
    z, xs, bm, cm, xdt, dec, pre = _ssd_in_step(x, ssd_cache, ssd)
    ssd_cache_new = jnp.concatenate([ssd_cache[:, 1:], pre[:, None]], axis=1)
    y, ssd_state_new = _ssd_step(xs, bm, cm, xdt, dec, ssd_state, ssd, tb)
    x = _ssd_out(y[None], z[None], x[None], ssd, ln[1][0], ln[1][1], n)[0]
    x, pre = _ffn_step(x, ffn_cache[1], ffn[1], ln[1][2], ln[1][3])
    ffn_cache1 = jnp.concatenate([ffn_cache[1][:, 1:], pre[:, None]], axis=1)
    return x, gdn_cache_new, gdn_state_new, ssd_cache_new, ssd_state_new, jnp.stack([ffn_cache0, ffn_cache1])


PROMPT_TILE = 512
SAMPLE_TOKENS = 8


def kernel(x_prompt, x_sample, cache_gdn_conv, state_gdn, cache_ssd_conv, state_ssd, cache_ffn_conv, gdn_w_in, gdn_conv_w, gdn_conv_b, gdn_a_log, gdn_dt_bias, gdn_norm_w, gdn_w_out, ssd_w_in, ssd_conv_w, ssd_conv_b, ssd_a_log, ssd_dt_bias, ssd_d, ssd_norm_w, ssd_w_out, ffn_w_up, ffn_conv_w, ffn_conv_b, ffn_w_down, ln1_g, ln1_b, ln2_g, ln2_b):
    gdn = _prep_gdn(gdn_w_in[0], gdn_conv_w[0], gdn_conv_b[0], gdn_a_log[0], gdn_dt_bias[0], gdn_norm_w[0],
                    gdn_w_out[0])
    ssd = _prep_ssd(ssd_w_in[0], ssd_conv_w[0], ssd_conv_b[0], ssd_a_log[0], ssd_dt_bias[0], ssd_d[0],
                    ssd_norm_w[0], ssd_w_out[0])
    ffn = [_prep_ffn(ffn_w_up[i], ffn_conv_w[i], ffn_conv_b[i], ffn_w_down[i]) for i in range(DEPTH)]
    ln = [(_row(ln1_g[i]), _row(ln1_b[i]), _row(ln2_g[i]), _row(ln2_b[i])) for i in range(DEPTH)]

    tl = min(PROMPT_TILE, x_prompt.shape[1])
    y_p, gcp, gsp, scp, ssp, fcp = _prompt_trunk(x_prompt, gdn, ssd, ffn, ln, tl)
    y_s, gcs, gss, scs, sss, fcs = _sample_trunk(
        x_sample[:, 0], cache_gdn_conv[0], state_gdn[0], cache_ssd_conv[0], state_ssd[0], cache_ffn_conv,
        gdn, ssd, ffn, ln, min(SAMPLE_TOKENS, x_sample.shape[0]))
    return (y_p, y_s[:, None], gcp[None], gcs[None], gsp[None], gss[None], scp[None], scs[None],
            ssp[None], sss[None], fcp, fcs)
```

```python
import functools

import jax
import jax.numpy as jnp
from jax import lax
from jax.experimental import pallas as pl
from jax.experimental.pallas import tpu as pltpu

F32 = jnp.float32
BF16 = jnp.bfloat16

D_MODEL = 1024
DEPTH = 2
CONV_W = 4
CHUNK = 64

GDN_HEADS = 8
GDN_DK = 128
GDN_DV = 128
GDN_QK = GDN_HEADS * GDN_DK
GDN_VD = GDN_HEADS * GDN_DV
GDN_CONV_DIM = 2 * GDN_QK + GDN_VD

SSD_INNER = 2 * D_MODEL
SSD_HEADDIM = 64
SSD_HEADS = SSD_INNER // SSD_HEADDIM
SSD_GROUPS = 4
SSD_STATE = 128
SSD_HPG = SSD_HEADS // SSD_GROUPS
SSD_GN = SSD_GROUPS * SSD_STATE
SSD_CONV_DIM = SSD_INNER + 2 * SSD_GN
SSD_GROUP_W = SSD_INNER // SSD_GROUPS

D_FF = 2816
FFN_CONV_W = 3
FFN_BLOCK = D_FF // 2

DN_ALPHA = (2 * DEPTH) ** 0.25
LN_EPS = 1e-5
RMS_EPS = 1e-6
L2_EPS = 1e-6

LANES = 128
SUBLANES = 8
VMEM_LIMIT = 56 * 1024 * 1024

NEG_BIG = -1e30


def _dot_nt(a, b):
    return lax.dot_general(a.astype(BF16), b.astype(BF16), (((1,), (1,)), ((), ())),
                           preferred_element_type=F32)


def _dot_tn(a, b):
    return lax.dot_general(a.astype(BF16), b.astype(BF16), (((0,), (0,)), ((), ())),
                           preferred_element_type=F32)


def _split3(x):
    hi = x.astype(BF16)
    r1 = x - hi.astype(F32)
    mid = r1.astype(BF16)
    lo = (r1 - mid.astype(F32)).astype(BF16)
    return hi, mid, lo


def _dot_exact_rhs(x, e):
    hi, mid, lo = _split3(x)
    return (jnp.dot(hi, e, preferred_element_type=F32) + jnp.dot(mid, e, preferred_element_type=F32)
            + jnp.dot(lo, e, preferred_element_type=F32))


def _silu(x):
    return x * jax.nn.sigmoid(x)


def _softplus(x):
    return jnp.maximum(x, 0.0) + jnp.log1p(jnp.exp(-jnp.abs(x)))


def _layer_norm(x, g, b):
    mu = jnp.mean(x, axis=-1, keepdims=True)
    xc = x - mu
    var = jnp.mean(xc * xc, axis=-1, keepdims=True)
    return xc * lax.rsqrt(var + LN_EPS) * g + b


def _l2norm(t):
    return t * lax.rsqrt(jnp.sum(t * t, axis=-1, keepdims=True) + L2_EPS)


def _col(a, i):
    return a[:, i:i + 1]


def _const_spec(shape):
    nd = len(shape)
    return pl.BlockSpec(shape, lambda *_: (0,) * nd)


def _params(n_grid):
    return pltpu.CompilerParams(dimension_semantics=("arbitrary",) * n_grid, vmem_limit_bytes=VMEM_LIMIT)


def _chunk_cumsum_matrix(tl):
    r = jnp.arange(tl)
    same = (r[:, None] // CHUNK) == (r[None, :] // CHUNK)
    return (same & (r[:, None] <= r[None, :])).astype(BF16)


def _gate_slabs(pre, n_keep, triu, slab_ref, slabt_ref, tl):
    pre_t = pre.T[0:2 * n_keep, :]
    cum_t = _dot_exact_rhs(pre_t, triu)
    row = lax.broadcasted_iota(jnp.int32, pre_t.shape, 0)
    st = jnp.where(row < n_keep, pre_t, cum_t)
    for c in range(tl // CHUNK):
        slabt_ref[c] = st[:, c * CHUNK:(c + 1) * CHUNK]
    slab_ref[...] = jnp.concatenate([st, jnp.zeros((LANES - 2 * n_keep, tl), F32)], axis=0).T


def _causal_conv(xx_ref, cw_ref, cb_ref, cs, tl, width):
    base = SUBLANES - (width - 1)
    acc = cb_ref[:, cs] + xx_ref[base:base + tl, cs] * cw_ref[0:1, cs]
    for k in range(1, width):
        acc = acc + xx_ref[base + k:base + k + tl, cs] * cw_ref[k:k + 1, cs]
    return acc


CONV_HEAD = (CONV_W - 1) * SUBLANES
CONV_ROWS = 64


def _permuted_conv(xb, w_ref, cw_ref, cb_ref, cache_ref, q3_ref, carry_ref, a3_ref, first, tl, emit, extra):
    n = tl // SUBLANES
    per = D_MODEL // LANES
    ngroup = w_ref.shape[1] // D_MODEL
    sub0 = lax.broadcasted_iota(jnp.int32, (SUBLANES, LANES), 0) == 0

    @pl.when(first)
    def _():
        carry_ref[...] = jnp.zeros(carry_ref.shape, F32)

    def project(j):
        cs = slice(j * D_MODEL, (j + 1) * D_MODEL)
        pre = jnp.dot(xb, w_ref[:, cs], preferred_element_type=F32)
        cache_ref[:, cs] = pre[tl - (CONV_W - 1):tl, :]
        for c in range(per):
            blk = j * per + c
            for a in range(SUBLANES):
                q3_ref[blk, pl.ds(CONV_HEAD + a, n, stride=SUBLANES), :] = (
                    pre[a * n:(a + 1) * n, c * LANES:(c + 1) * LANES])
            for m in range(CONV_W - 1):
                src = CONV_HEAD + SUBLANES * (n - (CONV_W - 1) + m)
                dst = slice(m * SUBLANES, (m + 1) * SUBLANES)
                rolled = pltpu.roll(q3_ref[blk, src:src + SUBLANES, :], 1, 0)
                q3_ref[blk, dst, :] = jnp.where(sub0, carry_ref[blk, dst, :], rolled)
                carry_ref[blk, dst, :] = rolled

    def convolve(j):
        for c in range(per):
            blk = j * per + c
            cs = slice(blk * LANES, (blk + 1) * LANES)
            for r0 in range(0, tl, CONV_ROWS):
                acc = cb_ref[:, cs] + q3_ref[blk, r0:r0 + CONV_ROWS, :] * cw_ref[0:1, cs]
                for k in range(1, CONV_W):
                    acc = acc + q3_ref[blk, r0 + k * SUBLANES:r0 + k * SUBLANES + CONV_ROWS, :] * cw_ref[k:k + 1, cs]
                a3_ref[c, r0:r0 + CONV_ROWS, :] = _silu(acc)
            for a in range(SUBLANES):
                emit(blk, slice(a * n, (a + 1) * n), a3_ref[c, pl.ds(a, n, stride=SUBLANES), :])

    project(0)
    for j in range(ngroup):
        if j + 1 < ngroup:
            project(j + 1)
        else:
            extra()
        convolve(j)


def _conv_scratch(ncols, tl):
    nblk = ncols // LANES
    return [pltpu.VMEM((nblk, CONV_HEAD + tl, LANES), F32), pltpu.VMEM((nblk, CONV_HEAD, LANES), F32),
            pltpu.VMEM((D_MODEL // LANES, tl, LANES), F32)]


def _gdn_in_kernel(x_ref, wqkv_ref, wz_ref, wba_ref, cw_ref, cb_ref, alog_ref, dtb_ref, triu_ref,
                   q_ref, k_ref, v_ref, z_ref, slab_ref, slabt_ref, cache_ref, q3_ref, carry_ref, a3_ref, *, tl):
    l = pl.program_id(1)
    xb = x_ref[...].astype(BF16)
    lane = lax.broadcasted_iota(jnp.int32, (tl, LANES), 1)

    ba = jnp.dot(xb, wba_ref[...], preferred_element_type=F32)
    g = -jnp.exp(alog_ref[...]) * _softplus(ba + dtb_ref[...])
    _gate_slabs(jnp.where(lane < GDN_HEADS, jax.nn.sigmoid(ba), g), GDN_HEADS, triu_ref[...], slab_ref, slabt_ref, tl)

    def z_proj():
        z_ref[...] = jnp.dot(xb, wz_ref[...], preferred_element_type=F32)

    def emit(blk, rows, piece):
        j, h = divmod(blk, GDN_HEADS)
        hs = slice(h * GDN_DK, (h + 1) * GDN_DK)
        if j == 0:
            q_ref[rows, hs] = (_l2norm(piece) * (GDN_DK ** -0.5)).astype(BF16)
        elif j == 1:
            k_ref[rows, hs] = _l2norm(piece).astype(BF16)
        else:
            v_ref[rows, hs] = piece

    _permuted_conv(xb, wqkv_ref, cw_ref, cb_ref, cache_ref, q3_ref, carry_ref, a3_ref, l == 0, tl, emit, z_proj)


def _gdn_in(x, p, tl):
    bsz, seq, _ = x.shape
    nl = seq // tl
    triu = _chunk_cumsum_matrix(tl)
    row = lambda w: pl.BlockSpec((None, tl, w), lambda b, l: (b, l, 0))
    bf = lambda w: jax.ShapeDtypeStruct((bsz, seq, w), BF16)
    ff = lambda w: jax.ShapeDtypeStruct((bsz, seq, w), F32)
    out_shape = (bf(GDN_QK), bf(GDN_QK), ff(GDN_VD), ff(GDN_VD), ff(LANES),
                 jax.ShapeDtypeStruct((bsz, seq // CHUNK, 2 * GDN_HEADS, CHUNK), F32),
                 jax.ShapeDtypeStruct((bsz, CONV_W - 1, GDN_CONV_DIM), F32))
    out_specs = (row(GDN_QK), row(GDN_QK), row(GDN_VD), row(GDN_VD), row(LANES),
                 pl.BlockSpec((None, tl // CHUNK, 2 * GDN_HEADS, CHUNK), lambda b, l: (b, l, 0, 0)),
                 pl.BlockSpec((None, CONV_W - 1, GDN_CONV_DIM), lambda b, l: (b, 0, 0)))
    consts = (p["wqkv"], p["wz"], p["wba"], p["cw"], p["cb"], p["alog"], p["dtb"], triu)
    return pl.pallas_call(
        functools.partial(_gdn_in_kernel, tl=tl),
        grid=(bsz, nl),
        in_specs=[row(D_MODEL)] + [_const_spec(c.shape) for c in consts],
        out_specs=out_specs,
        out_shape=out_shape,
        scratch_shapes=_conv_scratch(GDN_CONV_DIM, tl),
        compiler_params=_params(2),
        name="gdn_in",
    )(x, *consts)


def _split2(x):
    hi = x.astype(BF16)
    return hi, (x - hi.astype(F32)).astype(BF16)


def _pair_blockdiag(x, left):
    z = jnp.zeros_like(x)
    return jnp.concatenate([jnp.where(left, x, z), jnp.where(left, z, x)], axis=0)


def _lane_blockdiag(a, b):
    z = jnp.zeros_like(a)
    return jnp.concatenate([jnp.concatenate([a, z], axis=1), jnp.concatenate([z, b], axis=1)], axis=0)


def _pair_matmul(lhs_parts, w_parts, left):
    w_hi = _pair_blockdiag(w_parts[0], left)
    w = jnp.concatenate([w_hi, w_hi, _pair_blockdiag(w_parts[1], left)], axis=0)
    lhs = [jnp.concatenate([hi, lo, hi], axis=1) for hi, lo in lhs_parts]
    r = jnp.dot(jnp.concatenate(lhs, axis=0) if len(lhs) > 1 else lhs[0], w, preferred_element_type=F32)
    return [r[i * CHUNK:(i + 1) * CHUNK] for i in range(len(lhs))]


def _gdn_prep_stages(c, slot, k_ref, q_ref, slab_ref, slabt_ref, tinv_ref, qkg_ref, masks):
    incl, strict, eye2, left = masks
    npair = GDN_HEADS // 2
    rows = pl.ds(pl.multiple_of(c * CHUNK, CHUNK), CHUNK)
    sl = slab_ref[rows, :]
    st = slabt_ref[c]
    ns = []
    for p in range(npair):
        ha, hb = 2 * p, 2 * p + 1
        ka = k_ref[rows, ha * GDN_DK:(ha + 1) * GDN_DK]
        kb = k_ref[rows, hb * GDN_DK:(hb + 1) * GDN_DK]
        kk = jnp.concatenate([_dot_nt(ka, ka), _dot_nt(kb, kb)], axis=1)
        qk = jnp.concatenate([_dot_nt(q_ref[rows, ha * GDN_DK:(ha + 1) * GDN_DK], ka),
                              _dot_nt(q_ref[rows, hb * GDN_DK:(hb + 1) * GDN_DK], kb)], axis=1)
        gcol = jnp.where(left, _col(sl, GDN_HEADS + ha), _col(sl, GDN_HEADS + hb))
        bcol = jnp.where(left, _col(sl, ha), _col(sl, hb))
        grow = jnp.concatenate([st[GDN_HEADS + ha:GDN_HEADS + ha + 1, :], st[GDN_HEADS + hb:GDN_HEADS + hb + 1, :]],
                               axis=1)
        gam = jnp.exp(jnp.where(incl, gcol - grow, NEG_BIG))
        qkg_ref[slot, p] = (qk * gam).astype(BF16)
        ns.append(jnp.where(strict, kk * gam * bcol, 0.0) * -1.0)
    yield
    ps = [eye2 + n for n in ns]
    nparts = [_split2(n) for n in ns]
    ns = [_pair_matmul([nparts[p]], nparts[p], left)[0] for p in range(npair)]
    yield
    k = 2
    while 2 * k < CHUNK:
        nparts = [_split2(n) for n in ns]
        outs = [_pair_matmul([_split2(ps[p]), nparts[p]], nparts[p], left) for p in range(npair)]
        ps = [ps[p] + outs[p][0] for p in range(npair)]
        ns = [outs[p][1] for p in range(npair)]
        k *= 2
        yield
    for p in range(npair):
        t_inv = ps[p] + _pair_matmul([_split2(ps[p])], _split2(ns[p]), left)[0]
        tinv_ref[slot, p] = t_inv.astype(BF16)
    yield


def _gdn_state_stages(c, slot, q_ref, k_ref, v_ref, slab_ref, slabt_ref, tinv_ref, qkg_ref, o_ref, s_ref):
    npair = GDN_HEADS // 2
    rows = pl.ds(pl.multiple_of(c * CHUNK, CHUNK), CHUNK)
    sl = slab_ref[rows, :]
    st = slabt_ref[c]
    eg = jnp.exp(sl)
    ed = jnp.exp(sl[CHUNK - 1:CHUNK, :] - sl)
    hcols = lambda h: slice(h * GDN_DK, (h + 1) * GDN_DK)
    r1 = []
    for h in range(GDN_HEADS):
        lhs = jnp.concatenate([k_ref[rows, hcols(h)], q_ref[rows, hcols(h)]], axis=0)
        r1.append(jnp.dot(lhs, s_ref[h].astype(BF16), preferred_element_type=F32))
    yield
    egc = [_col(eg, GDN_HEADS + h) for h in range(GDN_HEADS)]
    rhs = [(_col(sl, h) * (v_ref[rows, hcols(h)] - egc[h] * r1[h][0:CHUNK])).astype(BF16) for h in range(GDN_HEADS)]
    v_new = [jnp.dot(tinv_ref[slot, p], _lane_blockdiag(rhs[2 * p], rhs[2 * p + 1]), preferred_element_type=F32)
             for p in range(npair)]
    yield
    for p in range(npair):
        va, vb = v_new[p][:, 0:GDN_DV], v_new[p][:, GDN_DV:2 * GDN_DV]
        qs = jnp.concatenate([egc[2 * p] * r1[2 * p][CHUNK:2 * CHUNK],
                              egc[2 * p + 1] * r1[2 * p + 1][CHUNK:2 * CHUNK]], axis=1)
        o_ref[rows, 2 * p * GDN_DV:(2 * p + 2) * GDN_DV] = qs + jnp.dot(
            qkg_ref[slot, p], _lane_blockdiag(va.astype(BF16), vb.astype(BF16)), preferred_element_type=F32)
        for h, vh in ((2 * p, va), (2 * p + 1, vb)):
            decay = jnp.exp(st[GDN_HEADS + h:GDN_HEADS + h + 1, CHUNK - 1:CHUNK])
            s_ref[h] = s_ref[h] * decay + _dot_tn(k_ref[rows, hcols(h)], vh * _col(ed, GDN_HEADS + h))
    yield


def _interleave(order, streams):
    for ch in order:
        next(streams[ch])


def _gdn_scan_kernel(q_ref, k_ref, v_ref, slab_ref, slabt_ref, o_ref, sout_ref, s_ref, tinv_ref, qkg_ref, *, tl):
    l = pl.program_id(1)
    nchunk = tl // CHUNK

    @pl.when(l == 0)
    def _():
        s_ref[...] = jnp.zeros(s_ref.shape, F32)

    ri = lax.broadcasted_iota(jnp.int32, (CHUNK, 2 * CHUNK), 0)
    li = lax.broadcasted_iota(jnp.int32, (CHUNK, 2 * CHUNK), 1)
    ci = li & (CHUNK - 1)
    masks = (ri >= ci, ri > ci, (ri == ci).astype(F32), li < CHUNK)

    prep = functools.partial(_gdn_prep_stages, k_ref=k_ref, q_ref=q_ref, slab_ref=slab_ref, slabt_ref=slabt_ref,
                             tinv_ref=tinv_ref, qkg_ref=qkg_ref, masks=masks)
    state = functools.partial(_gdn_state_stages, q_ref=q_ref, k_ref=k_ref, v_ref=v_ref, slab_ref=slab_ref,
                              slabt_ref=slabt_ref, tinv_ref=tinv_ref, qkg_ref=qkg_ref, o_ref=o_ref, s_ref=s_ref)

    _interleave("A" * 7, {"A": prep(0, 0)})

    def chunk_body(c, carry):
        slot = c & 1
        nxt = jnp.minimum(c + 1, nchunk - 1)
        _interleave("ABAABAABAA", {"A": prep(nxt, 1 - slot), "B": state(c, slot)})
        return carry

    lax.fori_loop(0, nchunk, chunk_body, 0)

    @pl.when(l == pl.num_programs(1) - 1)
    def _():
        sout_ref[...] = s_ref[...]


def _gdn_scan(q, k, v, slab, slabt, tl):
    bsz, seq, _ = q.shape
    row = lambda w: pl.BlockSpec((None, tl, w), lambda b, l: (b, l, 0))
    pair_scratch = pltpu.VMEM((2, GDN_HEADS // 2, CHUNK, 2 * CHUNK), BF16)
    return pl.pallas_call(
        functools.partial(_gdn_scan_kernel, tl=tl),
        grid=(bsz, seq // tl),
        in_specs=[row(GDN_QK), row(GDN_QK), row(GDN_VD), row(LANES),
                  pl.BlockSpec((None, tl // CHUNK, 2 * GDN_HEADS, CHUNK), lambda b, l: (b, l, 0, 0))],
        out_specs=(row(GDN_VD), pl.BlockSpec((None, GDN_HEADS, GDN_DK, GDN_DV), lambda b, l: (b, 0, 0, 0))),
        out_shape=(jax.ShapeDtypeStruct((bsz, seq, GDN_VD), F32),
                   jax.ShapeDtypeStruct((bsz, GDN_HEADS, GDN_DK, GDN_DV), F32)),
        scratch_shapes=[pltpu.VMEM((GDN_HEADS, GDN_DK, GDN_DV), F32), pair_scratch, pair_scratch],
        compiler_params=_params(2),
        name="gdn_scan",
    )(q, k, v, slab, slabt)


def _gdn_out_kernel(o_ref, z_ref, x_ref, nw_ref, wout_ref, g_ref, b_ref, y_ref):
    o = o_ref[...]
    z = z_ref[...]
    parts = []
    for h in range(GDN_HEADS):
        hs = slice(h * GDN_DV, (h + 1) * GDN_DV)
        oh = o[:, hs]
        rn = oh * lax.rsqrt(jnp.mean(oh * oh, axis=-1, keepdims=True) + RMS_EPS) * nw_ref[...]
        parts.append((rn * _silu(z[:, hs])).astype(BF16))
    mix = jnp.dot(jnp.concatenate(parts, axis=1), wout_ref[...], preferred_element_type=F32)
    y_ref[...] = _layer_norm(DN_ALPHA * x_ref[...] + mix, g_ref[...], b_ref[...])


def _gdn_out(o, z, x, p, ln_g, ln_b, tl):
    bsz, seq, _ = x.shape
    row = lambda w: pl.BlockSpec((None, tl, w), lambda b, l: (b, l, 0))
    consts = (p["nw"], p["wout"], ln_g, ln_b)
    return pl.pallas_call(
        _gdn_out_kernel,
        grid=(bsz, seq // tl),
        in_specs=[row(GDN_VD), row(GDN_VD), row(D_MODEL)] + [_const_spec(c.shape) for c in consts],
        out_specs=row(D_MODEL),
        out_shape=jax.ShapeDtypeStruct((bsz, seq, D_MODEL), F32),
        compiler_params=_params(2),
        name="gdn_out",
    )(o, z, x, *consts)


def _ffn_kernel(x_ref, wup_ref, cw_ref, cb_ref, wdown_ref, g_ref, b_ref, y_ref, cache_ref, gg_ref, *, tl):
    l = pl.program_id(1)

    @pl.when(l == 0)
    def _():
        gg_ref[0:SUBLANES, :] = jnp.zeros((SUBLANES, D_FF), F32)

    x = x_ref[...]
    xb = x.astype(BF16)
    acc = jnp.zeros((tl, D_MODEL), F32)
    for j in range(D_FF // FFN_BLOCK):
        cs = slice(j * FFN_BLOCK, (j + 1) * FFN_BLOCK)
        vs = slice(D_FF + j * FFN_BLOCK, D_FF + (j + 1) * FFN_BLOCK)
        gg_ref[SUBLANES:SUBLANES + tl, cs] = jnp.dot(xb, wup_ref[:, cs], preferred_element_type=F32)
        val = jnp.dot(xb, wup_ref[:, vs], preferred_element_type=F32)
        hid = _silu(_causal_conv(gg_ref, cw_ref, cb_ref, cs, tl, FFN_CONV_W)) * val
        acc = acc + jnp.dot(hid.astype(BF16), wdown_ref[cs, :], preferred_element_type=F32)
    cache_ref[...] = gg_ref[SUBLANES + tl - (FFN_CONV_W - 1):SUBLANES + tl, :]
    gg_ref[0:SUBLANES, :] = gg_ref[tl:tl + SUBLANES, :]
    y_ref[...] = _layer_norm(DN_ALPHA * x + acc, g_ref[...], b_ref[...])


def _ffn(x, p, ln_g, ln_b, tl):
    bsz, seq, _ = x.shape
    row = lambda w: pl.BlockSpec((None, tl, w), lambda b, l: (b, l, 0))
    consts = (p["wup"], p["cw"], p["cb"], p["wdown"], ln_g, ln_b)
    return pl.pallas_call(
        functools.partial(_ffn_kernel, tl=tl),
        grid=(bsz, seq // tl),
        in_specs=[row(D_MODEL)] + [_const_spec(c.shape) for c in consts],
        out_specs=(row(D_MODEL), pl.BlockSpec((None, FFN_CONV_W - 1, D_FF), lambda b, l: (b, 0, 0))),
        out_shape=(jax.ShapeDtypeStruct((bsz, seq, D_MODEL), F32),
                   jax.ShapeDtypeStruct((bsz, FFN_CONV_W - 1, D_FF), F32)),
        scratch_shapes=[pltpu.VMEM((tl + SUBLANES, D_FF), F32)],
        compiler_params=_params(2),
        name="ffn",
    )(x, *consts)


def _ssd_in_kernel(x_ref, wz_ref, wxbc_ref, wdt_ref, cw_ref, cb_ref, a_ref, dtb_ref, triu_ref,
                   z_ref, xs_ref, bm_ref, cm_ref, slab_ref, slabt_ref, cache_ref, q3_ref, carry_ref, a3_ref, *, tl):
    l = pl.program_id(1)
    xb = x_ref[...].astype(BF16)
    lane = lax.broadcasted_iota(jnp.int32, (tl, LANES), 1)

    dt = _softplus(jnp.dot(xb, wdt_ref[...], preferred_element_type=F32) + dtb_ref[...])
    dt = jnp.where(lane < 2 * SSD_HEADS, dt, 0.0)
    _gate_slabs(jnp.where(lane < SSD_HEADS, dt, dt * a_ref[...]), SSD_HEADS, triu_ref[...], slab_ref, slabt_ref, tl)

    def z_proj():
        z_ref[...] = jnp.dot(xb, wz_ref[...], preferred_element_type=F32)

    def emit(blk, rows, piece):
        col = blk * LANES
        if col < SSD_INNER:
            xs_ref[rows, col:col + LANES] = piece
        elif col < SSD_INNER + SSD_GN:
            bm_ref[rows, col - SSD_INNER:col - SSD_INNER + LANES] = piece.astype(BF16)
        else:
            cm_ref[rows, col - SSD_INNER - SSD_GN:col - SSD_INNER - SSD_GN + LANES] = piece.astype(BF16)

    _permuted_conv(xb, wxbc_ref, cw_ref, cb_ref, cache_ref, q3_ref, carry_ref, a3_ref, l == 0, tl, emit, z_proj)


def _ssd_in(x, p, tl):
    bsz, seq, _ = x.shape
    triu = _chunk_cumsum_matrix(tl)
    row = lambda w: pl.BlockSpec((None, tl, w), lambda b, l: (b, l, 0))
    out_shape = (jax.ShapeDtypeStruct((bsz, seq, SSD_INNER), F32), jax.ShapeDtypeStruct((bsz, seq, SSD_INNER), F32),
                 jax.ShapeDtypeStruct((bsz, seq, SSD_GN), BF16), jax.ShapeDtypeStruct((bsz, seq, SSD_GN), BF16),
                 jax.ShapeDtypeStruct((bsz, seq, LANES), F32),
                 jax.ShapeDtypeStruct((bsz, seq // CHUNK, 2 * SSD_HEADS, CHUNK), F32),
                 jax.ShapeDtypeStruct((bsz, CONV_W - 1, SSD_CONV_DIM), F32))
    out_specs = (row(SSD_INNER), row(SSD_INNER), row(SSD_GN), row(SSD_GN), row(LANES),
                 pl.BlockSpec((None, tl // CHUNK, 2 * SSD_HEADS, CHUNK), lambda b, l: (b, l, 0, 0)),
                 pl.BlockSpec((None, CONV_W - 1, SSD_CONV_DIM), lambda b, l: (b, 0, 0)))
    consts = (p["wz"], p["wxbc"], p["wdt"], p["cw"], p["cb"], p["a"], p["dtb"], triu)
    return pl.pallas_call(
        functools.partial(_ssd_in_kernel, tl=tl),
        grid=(bsz, seq // tl),
        in_specs=[row(D_MODEL)] + [_const_spec(c.shape) for c in consts],
        out_specs=out_specs,
        out_shape=out_shape,
        scratch_shapes=_conv_scratch(SSD_CONV_DIM, tl),
        compiler_params=_params(2),
        name="ssd_in",
    )(x, *consts)


def _ssd_scan_kernel(xs_ref, bm_ref, cm_ref, slab_ref, slabt_ref, edt_ref, eac_ref, dskip_ref,
                     y_ref, sout_ref, st_ref, dtx_ref, acx_ref, *, tl):
    l = pl.program_id(1)

    @pl.when(l == 0)
    def _():
        st_ref[...] = jnp.zeros(st_ref.shape, F32)

    hi, mid, lo = _split3(slab_ref[...])
    dtx_ref[...] = jnp.dot(jnp.concatenate([hi, mid], axis=1), edt_ref[...], preferred_element_type=F32)
    acx_ref[...] = jnp.dot(jnp.concatenate([hi, mid, lo], axis=1), eac_ref[...], preferred_element_type=F32)

    ri = lax.broadcasted_iota(jnp.int32, (CHUNK, 2 * CHUNK), 0)
    li = lax.broadcasted_iota(jnp.int32, (CHUNK, 2 * CHUNK), 1)
    incl = ri >= (li & (CHUNK - 1))
    left = li < CHUNK

    def chunk_body(c, carry):
        rows = pl.ds(pl.multiple_of(c * CHUNK, CHUNK), CHUNK)
        sl = slab_ref[rows, :]
        st = slabt_ref[c]
        ac_x = acx_ref[rows, :]
        xs = xs_ref[rows, :]
        xdt = xs * dtx_ref[rows, :]
        alast_x = ac_x[CHUNK - 1:CHUNK, :]
        xdec = (xdt * jnp.exp(alast_x - ac_x)).astype(BF16)
        xdt_b = xdt.astype(BF16)
        ea = jnp.exp(ac_x)
        sdecay = jnp.exp(alast_x)
        cb2, y_off = [], []
        for g in range(SSD_GROUPS):
            gs = slice(g * SSD_GROUP_W, (g + 1) * SSD_GROUP_W)
            ns = slice(g * SSD_STATE, (g + 1) * SSD_STATE)
            cg = cm_ref[rows, ns]
            bg = bm_ref[rows, ns]
            cb2.append(_dot_nt(cg, jnp.concatenate([bg, bg], axis=0)))
            st_g = st_ref[:, gs]
            y_off.append(jnp.dot(cg, st_g.astype(BF16), preferred_element_type=F32))
            st_ref[:, gs] = st_g * sdecay[:, gs] + _dot_tn(bg, xdec[:, gs])
        for g in range(SSD_GROUPS):
            gs = slice(g * SSD_GROUP_W, (g + 1) * SSD_GROUP_W)
            diag = []
            for pp in range(SSD_HPG // 2):
                ha = g * SSD_HPG + 2 * pp
                hb = ha + 1
                acol = jnp.where(left, _col(sl, SSD_HEADS + ha), _col(sl, SSD_HEADS + hb))
                arow = jnp.concatenate([st[SSD_HEADS + ha:SSD_HEADS + ha + 1, :],
                                        st[SSD_HEADS + hb:SSD_HEADS + hb + 1, :]], axis=1)
                seg = jnp.exp(jnp.where(incl, acol - arow, NEG_BIG))
                w = _pair_blockdiag(xdt_b[:, ha * SSD_HEADDIM:(ha + 2) * SSD_HEADDIM], left)
                diag.append(jnp.dot((seg * cb2[g]).astype(BF16), w, preferred_element_type=F32))
            y_ref[rows, gs] = jnp.concatenate(diag, axis=1) + y_off[g] * ea[:, gs] + dskip_ref[:, gs] * xs[:, gs]
        return carry

    lax.fori_loop(0, tl // CHUNK, chunk_body, 0)

    @pl.when(l == pl.num_programs(1) - 1)
    def _():
        sout_ref[...] = st_ref[...].T.reshape(SSD_HEADS, SSD_HEADDIM, SSD_STATE)


def _ssd_scan(xs, bm, cm, slab, slabt, p, tl):
    bsz, seq, _ = xs.shape
    row = lambda w: pl.BlockSpec((None, tl, w), lambda b, l: (b, l, 0))
    consts = (jnp.tile(p["edt"], (2, 1)), jnp.tile(p["eac"], (3, 1)), p["dskip"])
    return pl.pallas_call(
        functools.partial(_ssd_scan_kernel, tl=tl),
        grid=(bsz, seq // tl),
        in_specs=[row(SSD_INNER), row(SSD_GN), row(SSD_GN), row(LANES),
                  pl.BlockSpec((None, tl // CHUNK, 2 * SSD_HEADS, CHUNK), lambda b, l: (b, l, 0, 0))]
                 + [_const_spec(c.shape) for c in consts],
        out_specs=(row(SSD_INNER),
                   pl.BlockSpec((None, SSD_HEADS, SSD_HEADDIM, SSD_STATE), lambda b, l: (b, 0, 0, 0))),
        out_shape=(jax.ShapeDtypeStruct((bsz, seq, SSD_INNER), F32),
                   jax.ShapeDtypeStruct((bsz, SSD_HEADS, SSD_HEADDIM, SSD_STATE), F32)),
        scratch_shapes=[pltpu.VMEM((SSD_STATE, SSD_INNER), F32), pltpu.VMEM((tl, SSD_INNER), F32),
                        pltpu.VMEM((tl, SSD_INNER), F32)],
        compiler_params=_params(2),
        name="ssd_scan",
    )(xs, bm, cm, slab, slabt, *consts)


def _ssd_out_kernel(y_ref, z_ref, x_ref, nw_ref, wout_ref, g_ref, b_ref, o_ref):
    t = y_ref[...] * _silu(z_ref[...])
    parts = []
    for g in range(SSD_GROUPS):
        gs = slice(g * SSD_GROUP_W, (g + 1) * SSD_GROUP_W)
        tg = t[:, gs]
        parts.append((tg * lax.rsqrt(jnp.mean(tg * tg, axis=-1, keepdims=True) + RMS_EPS) * nw_ref[:, gs]).astype(BF16))
    mix = jnp.dot(jnp.concatenate(parts, axis=1), wout_ref[...], preferred_element_type=F32)
    o_ref[...] = _layer_norm(DN_ALPHA * x_ref[...] + mix, g_ref[...], b_ref[...])


def _ssd_out(y, z, x, p, ln_g, ln_b, tl):
    bsz, seq, _ = x.shape
    row = lambda w: pl.BlockSpec((None, tl, w), lambda b, l: (b, l, 0))
    consts = (p["nw"], p["wout"], ln_g, ln_b)
    return pl.pallas_call(
        _ssd_out_kernel,
        grid=(bsz, seq // tl),
        in_specs=[row(SSD_INNER), row(SSD_INNER), row(D_MODEL)] + [_const_spec(c.shape) for c in consts],
        out_specs=row(D_MODEL),
        out_shape=jax.ShapeDtypeStruct((bsz, seq, D_MODEL), F32),
        compiler_params=_params(2),
        name="ssd_out",
    )(y, z, x, *consts)


def _step_conv(pre, c_refs, cw_ref, cb_ref):
    width = len(c_refs) + 1
    acc = cb_ref[...] + c_refs[0][...] * cw_ref[0:1, :]
    for k in range(1, width - 1):
        acc = acc + c_refs[k][...] * cw_ref[k:k + 1, :]
    return acc + pre * cw_ref[width - 1:width, :]


def _gdn_in_step_kernel(x_ref, c0_ref, c1_ref, c2_ref, wqkv_ref, wz_ref, wba_ref, cw_ref, cb_ref, alog_ref, dtb_ref,
                        q_ref, k_ref, v_ref, z_ref, slab_ref, pre_ref):
    xb = x_ref[...].astype(BF16)
    n = xb.shape[0]
    lane = lax.broadcasted_iota(jnp.int32, (n, LANES), 1)
    ba = jnp.dot(xb, wba_ref[...], preferred_element_type=F32)
    g = -jnp.exp(alog_ref[...]) * _softplus(ba + dtb_ref[...])
    slab_ref[...] = jnp.where(lane < GDN_HEADS, jax.nn.sigmoid(ba), g)
    z_ref[...] = jnp.dot(xb, wz_ref[...], preferred_element_type=F32)
    pre = jnp.dot(xb, wqkv_ref[...], preferred_element_type=F32)
    pre_ref[...] = pre
    act = _silu(_step_conv(pre, (c0_ref, c1_ref, c2_ref), cw_ref, cb_ref))
    for h in range(GDN_HEADS):
        hs = slice(h * GDN_DK, (h + 1) * GDN_DK)
        q_ref[:, hs] = _l2norm(act[:, hs]) * (GDN_DK ** -0.5)
        k_ref[:, hs] = _l2norm(act[:, GDN_QK + h * GDN_DK:GDN_QK + (h + 1) * GDN_DK])
    v_ref[...] = act[:, 2 * GDN_QK:]


def _gdn_in_step(x, cache, p):
    n = x.shape[0]
    ff = lambda w: jax.ShapeDtypeStruct((n, w), F32)
    return pl.pallas_call(
        _gdn_in_step_kernel,
        out_shape=(ff(GDN_QK), ff(GDN_QK), ff(GDN_VD), ff(GDN_VD), ff(LANES), ff(GDN_CONV_DIM)),
        compiler_params=pltpu.CompilerParams(vmem_limit_bytes=VMEM_LIMIT),
        name="gdn_in_step",
    )(x, cache[:, 0], cache[:, 1], cache[:, 2], p["wqkv"], p["wz"], p["wba"], p["cw"], p["cb"], p["alog"], p["dtb"])


def _gdn_step_kernel(q_ref, k_ref, v_ref, slab_ref, s_ref, o_ref, sout_ref, *, tb):
    zpad = jnp.zeros((LANES - GDN_HEADS, GDN_DK), F32)

    def token_body(t, carry):
        kt = jnp.concatenate([k_ref[t], zpad], axis=0).T
        qt = jnp.concatenate([q_ref[t], zpad], axis=0).T
        v = v_ref[t]
        sl = slab_ref[pl.ds(t, 1), :]
        alpha = jnp.exp(sl)
        for h in range(GDN_HEADS):
            s = s_ref[t, h]
            kc = _col(kt, h)
            a_h = alpha[:, GDN_HEADS + h:GDN_HEADS + h + 1]
            ks = jnp.sum(s * kc, axis=0, keepdims=True)
            v_new = sl[:, h:h + 1] * (v[h:h + 1, :] - a_h * ks)
            s_new = s * a_h + kc * v_new
            sout_ref[t, h] = s_new
            o_ref[t, h:h + 1, :] = jnp.sum(s_new * _col(qt, h), axis=0, keepdims=True)
        return carry

    lax.fori_loop(0, tb, token_body, 0)


def _gdn_step(q, k, v, slab, state, tb):
    n = q.shape[0]
    q3, k3, v3 = (t.reshape(n, GDN_HEADS, GDN_DK) for t in (q, k, v))
    tok = pl.BlockSpec((tb, GDN_HEADS, GDN_DK), lambda i: (i, 0, 0))
    st = pl.BlockSpec((tb, GDN_HEADS, GDN_DK, GDN_DV), lambda i: (i, 0, 0, 0))
    o, s_new = pl.pallas_call(
        functools.partial(_gdn_step_kernel, tb=tb),
        grid=(n // tb,),
        in_specs=[tok, tok, tok, pl.BlockSpec((tb, LANES), lambda i: (i, 0)), st],
        out_specs=(tok, st),
        out_shape=(jax.ShapeDtypeStruct((n, GDN_HEADS, GDN_DV), F32), jax.ShapeDtypeStruct(state.shape, F32)),
        compiler_params=_params(1),
        name="gdn_step",
    )(q3, k3, v3, slab, state)
    return o.reshape(n, GDN_VD), s_new


def _ffn_step_kernel(x_ref, c0_ref, c1_ref, wup_ref, cw_ref, cb_ref, wdown_ref, g_ref, b_ref, y_ref, pre_ref):
    x = x_ref[...]
    xb = x.astype(BF16)
    gate = jnp.dot(xb, wup_ref[:, 0:D_FF], preferred_element_type=F32)
    val = jnp.dot(xb, wup_ref[:, D_FF:2 * D_FF], preferred_element_type=F32)
    pre_ref[...] = gate
    hid = _silu(_step_conv(gate, (c0_ref, c1_ref), cw_ref, cb_ref)) * val
    y_ref[...] = _layer_norm(DN_ALPHA * x + jnp.dot(hid.astype(BF16), wdown_ref[...], preferred_element_type=F32),
                             g_ref[...], b_ref[...])


def _ffn_step(x, cache, p, ln_g, ln_b):
    n = x.shape[0]
    return pl.pallas_call(
        _ffn_step_kernel,
        out_shape=(jax.ShapeDtypeStruct((n, D_MODEL), F32), jax.ShapeDtypeStruct((n, D_FF), F32)),
        compiler_params=pltpu.CompilerParams(vmem_limit_bytes=VMEM_LIMIT),
        name="ffn_step",
    )(x, cache[:, 0], cache[:, 1], p["wup"], p["cw"], p["cb"], p["wdown"], ln_g, ln_b)


def _ssd_in_step_kernel(x_ref, c0_ref, c1_ref, c2_ref, wz_ref, wxbc_ref, wdt_ref, cw_ref, cb_ref, a_ref, dtb_ref,
                        edt_ref, eac_ref, z_ref, xs_ref, bm_ref, cm_ref, xdt_ref, dec_ref, pre_ref):
    xb = x_ref[...].astype(BF16)
    n = xb.shape[0]
    lane = lax.broadcasted_iota(jnp.int32, (n, LANES), 1)
    dt = _softplus(jnp.dot(xb, wdt_ref[...], preferred_element_type=F32) + dtb_ref[...])
    dt = jnp.where(lane < 2 * SSD_HEADS, dt, 0.0)
    slab = jnp.where(lane < SSD_HEADS, dt, dt * a_ref[...])
    z_ref[...] = jnp.dot(xb, wz_ref[...], preferred_element_type=F32)
    pre = jnp.dot(xb, wxbc_ref[...], preferred_element_type=F32)
    pre_ref[...] = pre
    act = _silu(_step_conv(pre, (c0_ref, c1_ref, c2_ref), cw_ref, cb_ref))
    xs = act[:, 0:SSD_INNER]
    xs_ref[...] = xs
    bm_ref[...] = act[:, SSD_INNER:SSD_INNER + SSD_GN]
    cm_ref[...] = act[:, SSD_INNER + SSD_GN:]
    xdt_ref[...] = xs * _dot_exact_rhs(slab, edt_ref[...])
    dec_ref[...] = jnp.exp(_dot_exact_rhs(slab, eac_ref[...]))


def _ssd_in_step(x, cache, p):
    n = x.shape[0]
    ff = lambda w: jax.ShapeDtypeStruct((n, w), F32)
    return pl.pallas_call(
        _ssd_in_step_kernel,
        out_shape=(ff(SSD_INNER), ff(SSD_INNER), ff(SSD_GN), ff(SSD_GN), ff(SSD_INNER), ff(SSD_INNER),
                   ff(SSD_CONV_DIM)),
        compiler_params=pltpu.CompilerParams(vmem_limit_bytes=VMEM_LIMIT),
        name="ssd_in_step",
    )(x, cache[:, 0], cache[:, 1], cache[:, 2], p["wz"], p["wxbc"], p["wdt"], p["cw"], p["cb"], p["a"], p["dtb"],
      p["edt"], p["eac"])


def _ssd_step_kernel(xs_ref, bm_ref, cm_ref, xdt_ref, dec_ref, dskip_ref, s_ref, y_ref, sout_ref, *, tb):
    hp = SSD_INNER
    rid = lax.broadcasted_iota(jnp.int32, (SUBLANES, hp), 0)
    gid = lax.broadcasted_iota(jnp.int32, (SUBLANES, hp), 1) // SSD_GROUP_W
    zpad_r = jnp.zeros((LANES - SUBLANES, hp), F32)

    def token_body(t, carry):
        row = pl.ds(t, 1)
        xdt = xdt_ref[row, :]
        dec = dec_ref[row, :]
        stack = jnp.where(rid == gid, jnp.broadcast_to(xdt, (SUBLANES, hp)), 0.0)
        stack = jnp.where(rid == SSD_GROUPS, jnp.broadcast_to(dec, (SUBLANES, hp)), stack)
        cols = jnp.concatenate([stack, zpad_r], axis=0).T
        s = s_ref[t].reshape(hp, SSD_STATE)
        bm = bm_ref[row, :]
        upd = jnp.concatenate(
            [_col(cols[g * SSD_GROUP_W:(g + 1) * SSD_GROUP_W, :], g) * bm[:, g * SSD_STATE:(g + 1) * SSD_STATE]
             for g in range(SSD_GROUPS)], axis=0)
        s_new = s * _col(cols, SSD_GROUPS) + upd
        sout_ref[t] = s_new.reshape(SSD_HEADS, SSD_HEADDIM, SSD_STATE)
        cm = cm_ref[row, :]
        cmat = jnp.concatenate([cm[:, g * SSD_STATE:(g + 1) * SSD_STATE] for g in range(SSD_GROUPS)]
                               + [jnp.zeros((SUBLANES - SSD_GROUPS, SSD_STATE), F32)], axis=0)
        yall = _dot_nt(cmat, s_new)
        y = jnp.sum(jnp.where(rid == gid, yall, 0.0), axis=0, keepdims=True)
        y_ref[row, :] = y + dskip_ref[...] * xs_ref[row, :]
        return carry

    lax.fori_loop(0, tb, token_body, 0)


def _ssd_step(xs, bm, cm, xdt, dec, state, p, tb):
    n = xs.shape[0]
    tok = lambda w: pl.BlockSpec((tb, w), lambda i: (i, 0))
    st = pl.BlockSpec((tb, SSD_HEADS, SSD_HEADDIM, SSD_STATE), lambda i: (i, 0, 0, 0))
    return pl.pallas_call(
        functools.partial(_ssd_step_kernel, tb=tb),
        grid=(n // tb,),
        in_specs=[tok(SSD_INNER), tok(SSD_GN), tok(SSD_GN), tok(SSD_INNER), tok(SSD_INNER),
                  _const_spec(p["dskip"].shape), st],
        out_specs=(tok(SSD_INNER), st),
        out_shape=(jax.ShapeDtypeStruct((n, SSD_INNER), F32), jax.ShapeDtypeStruct(state.shape, F32)),
        compiler_params=_params(1),
        name="ssd_step",
    )(xs, bm, cm, xdt, dec, p["dskip"], state)


def _row(v, width=None, offset=0):
    v = v.astype(F32).reshape(1, -1)
    if width is None:
        return v
    return jnp.pad(v, ((0, 0), (offset, width - offset - v.shape[1])))


def _prep_gdn(w_in, conv_w, conv_b, a_log, dt_bias, norm_w, w_out):
    ba = w_in[:, GDN_CONV_DIM + GDN_VD:]
    return {
        "wqkv": w_in[:, :GDN_CONV_DIM].astype(BF16),
        "wz": w_in[:, GDN_CONV_DIM:GDN_CONV_DIM + GDN_VD].astype(BF16),
        "wba": jnp.pad(ba, ((0, 0), (0, LANES - ba.shape[1]))).astype(BF16),
        "cw": conv_w.astype(F32),
        "cb": _row(conv_b),
        "alog": _row(a_log, LANES, GDN_HEADS),
        "dtb": _row(dt_bias, LANES, GDN_HEADS),
        "nw": _row(norm_w),
        "wout": w_out.astype(BF16),
    }


def _prep_ssd(w_in, conv_w, conv_b, a_log, dt_bias, d_skip, norm_w, w_out):
    wdt = w_in[:, SSD_INNER + SSD_CONV_DIM:]
    wdt2 = jnp.concatenate([wdt, wdt], axis=1)
    head_of_lane = jnp.arange(SSD_INNER) // SSD_HEADDIM
    sel = jnp.arange(LANES)[:, None]
    return {
        "wz": w_in[:, :SSD_INNER].astype(BF16),
        "wxbc": w_in[:, SSD_INNER:SSD_INNER + SSD_CONV_DIM].astype(BF16),
        "wdt": jnp.pad(wdt2, ((0, 0), (0, LANES - 2 * SSD_HEADS))).astype(BF16),
        "cw": conv_w.astype(F32),
        "cb": _row(conv_b),
        "a": _row(-jnp.exp(a_log.astype(F32)), LANES, SSD_HEADS),
        "dtb": _row(jnp.concatenate([dt_bias, dt_bias]), LANES, 0),
        "edt": (sel == head_of_lane[None, :]).astype(BF16),
        "eac": (sel == head_of_lane[None, :] + SSD_HEADS).astype(BF16),
        "dskip": _row(jnp.repeat(d_skip, SSD_HEADDIM)),
        "nw": _row(norm_w),
        "wout": w_out.astype(BF16),
    }


def _prep_ffn(w_up, conv_w, conv_b, w_down):
    return {"wup": w_up.astype(BF16), "cw": conv_w.astype(F32), "cb": _row(conv_b), "wdown": w_down.astype(BF16)}


def _prompt_trunk(x, gdn, ssd, ffn, ln, tl):
    q, k, v, z, slab, slabt, gdn_cache = _gdn_in(x, gdn, tl)
    o, gdn_state = _gdn_scan(q, k, v, slab, slabt, min(GDN_SCAN_TILE, x.shape[1]))
    x = _gdn_out(o, z, x, gdn, ln[0][0], ln[0][1], tl)
    x, ffn_cache0 = _ffn(x, ffn[0], ln[0][2], ln[0][3], tl)
    z, xs, bm, cm, slab, slabt, ssd_cache = _ssd_in(x, ssd, tl)
    y, ssd_state = _ssd_scan(xs, bm, cm, slab, slabt, ssd, tl)
    x = _ssd_out(y, z, x, ssd, ln[1][0], ln[1][1], tl)
    x, ffn_cache1 = _ffn(x, ffn[1], ln[1][2], ln[1][3], tl)
    return x, gdn_cache, gdn_state, ssd_cache, ssd_state, jnp.stack([ffn_cache0, ffn_cache1])


def _sample_trunk(x, gdn_cache, gdn_state, ssd_cache, ssd_state, ffn_cache, gdn, ssd, ffn, ln, tb):
    n = x.shape[0]
    q, k, v, z, slab, pre = _gdn_in_step(x, gdn_cache, gdn)
    gdn_cache_new = jnp.concatenate([gdn_cache[:, 1:], pre[:, None]], axis=1)
    o, gdn_state_new = _gdn_step(q, k, v, slab, gdn_state, tb)
    x = _gdn_out(o[None], z[None], x[None], gdn, ln[0][0], ln[0][1], n)[0]
    x, pre = _ffn_step(x, ffn_cache[0], ffn[0], ln[0][2], ln[0][3])
    ffn_cache0 = jnp.concatenate([ffn_cache[0][:, 1:], pre[:, None]], axis=1)
    z, xs, bm, cm, xdt, dec, pre = _ssd_in_step(x, ssd_cache, ssd)
    ssd_cache_new = jnp.concatenate([ssd_cache[:, 1:], pre[:, None]], axis=1)
    y, ssd_state_new = _ssd_step(xs, bm, cm, xdt, dec, ssd_state, ssd, tb)
    x = _ssd_out(y[None], z[None], x[None], ssd, ln[1][0], ln[1][1], n)[0]
    x, pre = _ffn_step(x, ffn_cache[1], ffn[1], ln[1][2], ln[1][3])
    ffn_cache1 = jnp.concatenate([ffn_cache[1][:, 1:], pre[:, None]], axis=1)
    return x, gdn_cache_new, gdn_state_new, ssd_cache_new, ssd_state_new, jnp.stack([ffn_cache0, ffn_cache1])


PROMPT_TILE = 512
GDN_SCAN_TILE = 1024
SAMPLE_TOKENS = 8


def kernel(x_prompt, x_sample, cache_gdn_conv, state_gdn, cache_ssd_conv, state_ssd, cache_ffn_conv, gdn_w_in, gdn_conv_w, gdn_conv_b, gdn_a_log, gdn_dt_bias, gdn_norm_w, gdn_w_out, ssd_w_in, ssd_conv_w, ssd_conv_b, ssd_a_log, ssd_dt_bias, ssd_d, ssd_norm_w, ssd_w_out, ffn_w_up, ffn_conv_w, ffn_conv_b, ffn_w_down, ln1_g, ln1_b, ln2_g, ln2_b):
    gdn = _prep_gdn(gdn_w_in[0], gdn_conv_w[0], gdn_conv_b[0], gdn_a_log[0], gdn_dt_bias[0], gdn_norm_w[0],
                    gdn_w_out[0])
    ssd = _prep_ssd(ssd_w_in[0], ssd_conv_w[0], ssd_conv_b[0], ssd_a_log[0], ssd_dt_bias[0], ssd_d[0],
                    ssd_norm_w[0], ssd_w_out[0])
    ffn = [_prep_ffn(ffn_w_up[i], ffn_conv_w[i], ffn_conv_b[i], ffn_w_down[i]) for i in range(DEPTH)]
    ln = [(_row(ln1_g[i]), _row(ln1_b[i]), _row(ln2_g[i]), _row(ln2_b[i])) for i in range(DEPTH)]

    tl = min(PROMPT_TILE, x_prompt.shape[1])
    y_p, gcp, gsp, scp, ssp, fcp = _prompt_trunk(x_prompt, gdn, ssd, ffn, ln, tl)
    y_s, gcs, gss, scs, sss, fcs = _sample_trunk(
        x_sample[:, 0], cache_gdn_conv[0], state_gdn[0], cache_ssd_conv[0], state_ssd[0], cache_ffn_conv,
        gdn, ssd, ffn, ln, min(SAMPLE_TOKENS, x_sample.shape[0]))
    return (y_p, y_s[:, None], gcp[None], gcs[None], gsp[None], gss[None], scp[None], scs[None],
            ssp[None], sss[None], fcp, fcs)
```

```python
import functools

import jax
import jax.numpy as jnp
from jax import lax
from jax.experimental import pallas as pl
from jax.experimental.pallas import tpu as pltpu

F32 = jnp.float32
BF16 = jnp.bfloat16

D_MODEL = 1024
DEPTH = 2
CONV_W = 4
CHUNK = 64

GDN_HEADS = 8
GDN_DK = 128
GDN_DV = 128
GDN_QK = GDN_HEADS * GDN_DK
GDN_VD = GDN_HEADS * GDN_DV
GDN_CONV_DIM = 2 * GDN_QK + GDN_VD

SSD_INNER = 2 * D_MODEL
SSD_HEADDIM = 64
SSD_HEADS = SSD_INNER // SSD_HEADDIM
SSD_GROUPS = 4
SSD_STATE = 128
SSD_HPG = SSD_HEADS // SSD_GROUPS
SSD_GN = SSD_GROUPS * SSD_STATE
SSD_CONV_DIM = SSD_INNER + 2 * SSD_GN
SSD_GROUP_W = SSD_INNER // SSD_GROUPS

D_FF = 2816
FFN_CONV_W = 3
FFN_BLOCK = D_FF // 2

DN_ALPHA = (2 * DEPTH) ** 0.25
LN_EPS = 1e-5
RMS_EPS = 1e-6
L2_EPS = 1e-6

LANES = 128
SUBLANES = 8
VMEM_LIMIT = 56 * 1024 * 1024

NEG_BIG = -1e30


def _dot_nt(a, b):
    return lax.dot_general(a.astype(BF16), b.astype(BF16), (((1,), (1,)), ((), ())),
                           preferred_element_type=F32)


def _dot_tn(a, b):
    return lax.dot_general(a.astype(BF16), b.astype(BF16), (((0,), (0,)), ((), ())),
                           preferred_element_type=F32)


def _split3(x):
    hi = x.astype(BF16)
    r1 = x - hi.astype(F32)
    mid = r1.astype(BF16)
    lo = (r1 - mid.astype(F32)).astype(BF16)
    return hi, mid, lo


def _dot_exact_rhs(x, e):
    hi, mid, lo = _split3(x)
    return (jnp.dot(hi, e, preferred_element_type=F32) + jnp.dot(mid, e, preferred_element_type=F32)
            + jnp.dot(lo, e, preferred_element_type=F32))


def _silu(x):
    return x * jax.nn.sigmoid(x)


def _softplus(x):
    return jnp.maximum(x, 0.0) + jnp.log1p(jnp.exp(-jnp.abs(x)))


def _layer_norm(x, g, b):
    mu = jnp.mean(x, axis=-1, keepdims=True)
    xc = x - mu
    var = jnp.mean(xc * xc, axis=-1, keepdims=True)
    return xc * lax.rsqrt(var + LN_EPS) * g + b


def _l2norm(t):
    return t * lax.rsqrt(jnp.sum(t * t, axis=-1, keepdims=True) + L2_EPS)


def _col(a, i):
    return a[:, i:i + 1]


def _const_spec(shape):
    nd = len(shape)
    return pl.BlockSpec(shape, lambda *_: (0,) * nd)


def _params(n_grid):
    return pltpu.CompilerParams(dimension_semantics=("arbitrary",) * n_grid, vmem_limit_bytes=VMEM_LIMIT)


def _chunk_cumsum_matrix(tl):
    r = jnp.arange(tl)
    same = (r[:, None] // CHUNK) == (r[None, :] // CHUNK)
    return (same & (r[:, None] <= r[None, :])).astype(BF16)


def _gate_slabs(pre, n_keep, triu, slab_ref, slabt_ref, tl):
    pre_t = pre.T[0:2 * n_keep, :]
    cum_t = _dot_exact_rhs(pre_t, triu)
    row = lax.broadcasted_iota(jnp.int32, pre_t.shape, 0)
    st = jnp.where(row < n_keep, pre_t, cum_t)
    for c in range(tl // CHUNK):
        slabt_ref[c] = st[:, c * CHUNK:(c + 1) * CHUNK]
    slab_ref[...] = jnp.concatenate([st, jnp.zeros((LANES - 2 * n_keep, tl), F32)], axis=0).T


def _causal_conv(xx_ref, cw_ref, cb_ref, cs, tl, width):
    base = SUBLANES - (width - 1)
    acc = cb_ref[:, cs] + xx_ref[base:base + tl, cs] * cw_ref[0:1, cs]
    for k in range(1, width):
        acc = acc + xx_ref[base + k:base + k + tl, cs] * cw_ref[k:k + 1, cs]
    return acc


CONV_HEAD = (CONV_W - 1) * SUBLANES
CONV_ROWS = 64


def _permuted_conv(xb, w_ref, cw_ref, cb_ref, cache_ref, q3_ref, carry_ref, a3_ref, first, tl, emit, extra):
    n = tl // SUBLANES
    per = D_MODEL // LANES
    ngroup = w_ref.shape[1] // D_MODEL
    sub0 = lax.broadcasted_iota(jnp.int32, (SUBLANES, LANES), 0) == 0

    @pl.when(first)
    def _():
        carry_ref[...] = jnp.zeros(carry_ref.shape, F32)

    def project(j):
        cs = slice(j * D_MODEL, (j + 1) * D_MODEL)
        pre = jnp.dot(xb, w_ref[:, cs], preferred_element_type=F32)
        cache_ref[:, cs] = pre[tl - (CONV_W - 1):tl, :]
        for c in range(per):
            blk = j * per + c
            for a in range(SUBLANES):
                q3_ref[blk, pl.ds(CONV_HEAD + a, n, stride=SUBLANES), :] = (
                    pre[a * n:(a + 1) * n, c * LANES:(c + 1) * LANES])
            for m in range(CONV_W - 1):
                src = CONV_HEAD + SUBLANES * (n - (CONV_W - 1) + m)
                dst = slice(m * SUBLANES, (m + 1) * SUBLANES)
                rolled = pltpu.roll(q3_ref[blk, src:src + SUBLANES, :], 1, 0)
                q3_ref[blk, dst, :] = jnp.where(sub0, carry_ref[blk, dst, :], rolled)
                carry_ref[blk, dst, :] = rolled

    def convolve(j):
        for c in range(per):
            blk = j * per + c
            cs = slice(blk * LANES, (blk + 1) * LANES)
            for r0 in range(0, tl, CONV_ROWS):
                acc = cb_ref[:, cs] + q3_ref[blk, r0:r0 + CONV_ROWS, :] * cw_ref[0:1, cs]
                for k in range(1, CONV_W):
                    acc = acc + q3_ref[blk, r0 + k * SUBLANES:r0 + k * SUBLANES + CONV_ROWS, :] * cw_ref[k:k + 1, cs]
                a3_ref[c, r0:r0 + CONV_ROWS, :] = _silu(acc)
            for a in range(SUBLANES):
                emit(blk, slice(a * n, (a + 1) * n), a3_ref[c, pl.ds(a, n, stride=SUBLANES), :])

    project(0)
    for j in range(ngroup):
        if j + 1 < ngroup:
            project(j + 1)
        else:
            extra()
        convolve(j)


def _conv_scratch(ncols, tl):
    nblk = ncols // LANES
    return [pltpu.VMEM((nblk, CONV_HEAD + tl, LANES), F32), pltpu.VMEM((nblk, CONV_HEAD, LANES), F32),
            pltpu.VMEM((D_MODEL // LANES, tl, LANES), F32)]


def _gdn_in_kernel(x_ref, wqkv_ref, wz_ref, wba_ref, cw_ref, cb_ref, alog_ref, dtb_ref, triu_ref,
                   q_ref, k_ref, v_ref, z_ref, slab_ref, slabt_ref, cache_ref, q3_ref, carry_ref, a3_ref, *, tl):
    l = pl.program_id(1)
    xb = x_ref[...].astype(BF16)
    lane = lax.broadcasted_iota(jnp.int32, (tl, LANES), 1)

    ba = jnp.dot(xb, wba_ref[...], preferred_element_type=F32)
    g = -jnp.exp(alog_ref[...]) * _softplus(ba + dtb_ref[...])
    _gate_slabs(jnp.where(lane < GDN_HEADS, jax.nn.sigmoid(ba), g), GDN_HEADS, triu_ref[...], slab_ref, slabt_ref, tl)

    def z_proj():
        z_ref[...] = jnp.dot(xb, wz_ref[...], preferred_element_type=F32).astype(BF16)

    def emit(blk, rows, piece):
        j, h = divmod(blk, GDN_HEADS)
        hs = slice(h * GDN_DK, (h + 1) * GDN_DK)
        if j == 0:
            q_ref[rows, hs] = (_l2norm(piece) * (GDN_DK ** -0.5)).astype(BF16)
        elif j == 1:
            k_ref[rows, hs] = _l2norm(piece).astype(BF16)
        else:
            v_ref[rows, hs] = piece

    _permuted_conv(xb, wqkv_ref, cw_ref, cb_ref, cache_ref, q3_ref, carry_ref, a3_ref, l == 0, tl, emit, z_proj)


def _gdn_in(x, p, tl):
    bsz, seq, _ = x.shape
    nl = seq // tl
    triu = _chunk_cumsum_matrix(tl)
    row = lambda w: pl.BlockSpec((None, tl, w), lambda b, l: (b, l, 0))
    bf = lambda w: jax.ShapeDtypeStruct((bsz, seq, w), BF16)
    ff = lambda w: jax.ShapeDtypeStruct((bsz, seq, w), F32)
    out_shape = (bf(GDN_QK), bf(GDN_QK), ff(GDN_VD), bf(GDN_VD), ff(LANES),
                 jax.ShapeDtypeStruct((bsz, seq // CHUNK, 2 * GDN_HEADS, CHUNK), F32),
                 jax.ShapeDtypeStruct((bsz, CONV_W - 1, GDN_CONV_DIM), F32))
    out_specs = (row(GDN_QK), row(GDN_QK), row(GDN_VD), row(GDN_VD), row(LANES),
                 pl.BlockSpec((None, tl // CHUNK, 2 * GDN_HEADS, CHUNK), lambda b, l: (b, l, 0, 0)),
                 pl.BlockSpec((None, CONV_W - 1, GDN_CONV_DIM), lambda b, l: (b, 0, 0)))
    consts = (p["wqkv"], p["wz"], p["wba"], p["cw"], p["cb"], p["alog"], p["dtb"], triu)
    return pl.pallas_call(
        functools.partial(_gdn_in_kernel, tl=tl),
        grid=(bsz, nl),
        in_specs=[row(D_MODEL)] + [_const_spec(c.shape) for c in consts],
        out_specs=out_specs,
        out_shape=out_shape,
        scratch_shapes=_conv_scratch(GDN_CONV_DIM, tl),
        compiler_params=_params(2),
        name="gdn_in",
    )(x, *consts)


def _split2(x):
    hi = x.astype(BF16)
    return hi, (x - hi.astype(F32)).astype(BF16)


def _pair_blockdiag(x, left):
    z = jnp.zeros_like(x)
    return jnp.concatenate([jnp.where(left, x, z), jnp.where(left, z, x)], axis=0)


def _lane_blockdiag(a, b):
    z = jnp.zeros_like(a)
    return jnp.concatenate([jnp.concatenate([a, z], axis=1), jnp.concatenate([z, b], axis=1)], axis=0)


def _pair_matmul(lhs_parts, w_parts, left):
    w_hi = _pair_blockdiag(w_parts[0], left)
    w = jnp.concatenate([w_hi, w_hi, _pair_blockdiag(w_parts[1], left)], axis=0)
    lhs = [jnp.concatenate([hi, lo, hi], axis=1) for hi, lo in lhs_parts]
    r = jnp.dot(jnp.concatenate(lhs, axis=0) if len(lhs) > 1 else lhs[0], w, preferred_element_type=F32)
    return [r[i * CHUNK:(i + 1) * CHUNK] for i in range(len(lhs))]


def _gdn_prep_stages(c, slot, k_ref, q_ref, slab_ref, slabt_ref, tinv_ref, qkg_ref, masks):
    incl, strict, eye2, left = masks
    npair = GDN_HEADS // 2
    rows = pl.ds(pl.multiple_of(c * CHUNK, CHUNK), CHUNK)
    sl = slab_ref[rows, :]
    st = slabt_ref[c]
    ns = []
    for p in range(npair):
        ha, hb = 2 * p, 2 * p + 1
        ka = k_ref[rows, ha * GDN_DK:(ha + 1) * GDN_DK]
        kb = k_ref[rows, hb * GDN_DK:(hb + 1) * GDN_DK]
        kk = jnp.concatenate([_dot_nt(ka, ka), _dot_nt(kb, kb)], axis=1)
        qk = jnp.concatenate([_dot_nt(q_ref[rows, ha * GDN_DK:(ha + 1) * GDN_DK], ka),
                              _dot_nt(q_ref[rows, hb * GDN_DK:(hb + 1) * GDN_DK], kb)], axis=1)
        gcol = jnp.where(left, _col(sl, GDN_HEADS + ha), _col(sl, GDN_HEADS + hb))
        bcol = jnp.where(left, _col(sl, ha), _col(sl, hb))
        grow = jnp.concatenate([st[GDN_HEADS + ha:GDN_HEADS + ha + 1, :], st[GDN_HEADS + hb:GDN_HEADS + hb + 1, :]],
                               axis=1)
        gam = jnp.exp(jnp.where(incl, gcol - grow, NEG_BIG))
        qkg_ref[slot, p] = (qk * gam).astype(BF16)
        ns.append(jnp.where(strict, kk * gam * bcol, 0.0) * -1.0)
    yield
    ps = [eye2 + n for n in ns]
    nparts = [_split2(n) for n in ns]
    ns = [_pair_matmul([nparts[p]], nparts[p], left)[0] for p in range(npair)]
    yield
    k = 2
    while 2 * k < CHUNK:
        nparts = [_split2(n) for n in ns]
        outs = [_pair_matmul([_split2(ps[p]), nparts[p]], nparts[p], left) for p in range(npair)]
        ps = [ps[p] + outs[p][0] for p in range(npair)]
        ns = [outs[p][1] for p in range(npair)]
        k *= 2
        yield
    for p in range(npair):
        t_inv = ps[p] + _pair_matmul([_split2(ps[p])], _split2(ns[p]), left)[0]
        tinv_ref[slot, p] = t_inv.astype(BF16)
    yield


def _gdn_state_stages(c, slot, q_ref, k_ref, v_ref, slab_ref, slabt_ref, tinv_ref, qkg_ref, o_ref, s_ref):
    npair = GDN_HEADS // 2
    rows = pl.ds(pl.multiple_of(c * CHUNK, CHUNK), CHUNK)
    sl = slab_ref[rows, :]
    st = slabt_ref[c]
    eg = jnp.exp(sl)
    ed = jnp.exp(sl[CHUNK - 1:CHUNK, :] - sl)
    hcols = lambda h: slice(h * GDN_DK, (h + 1) * GDN_DK)
    r1 = []
    for h in range(GDN_HEADS):
        lhs = jnp.concatenate([k_ref[rows, hcols(h)], q_ref[rows, hcols(h)]], axis=0)
        r1.append(jnp.dot(lhs, s_ref[h].astype(BF16), preferred_element_type=F32))
    yield
    egc = [_col(eg, GDN_HEADS + h) for h in range(GDN_HEADS)]
    rhs = [(_col(sl, h) * (v_ref[rows, hcols(h)] - egc[h] * r1[h][0:CHUNK])).astype(BF16) for h in range(GDN_HEADS)]
    v_new = [jnp.dot(tinv_ref[slot, p], _lane_blockdiag(rhs[2 * p], rhs[2 * p + 1]), preferred_element_type=F32)
             for p in range(npair)]
    yield
    for p in range(npair):
        va, vb = v_new[p][:, 0:GDN_DV], v_new[p][:, GDN_DV:2 * GDN_DV]
        qs = jnp.concatenate([egc[2 * p] * r1[2 * p][CHUNK:2 * CHUNK],
                              egc[2 * p + 1] * r1[2 * p + 1][CHUNK:2 * CHUNK]], axis=1)
        o_ref[rows, 2 * p * GDN_DV:(2 * p + 2) * GDN_DV] = (qs + jnp.dot(
            qkg_ref[slot, p], _lane_blockdiag(va.astype(BF16), vb.astype(BF16)), preferred_element_type=F32)
        ).astype(BF16)
        for h, vh in ((2 * p, va), (2 * p + 1, vb)):
            decay = jnp.exp(st[GDN_HEADS + h:GDN_HEADS + h + 1, CHUNK - 1:CHUNK])
            s_ref[h] = s_ref[h] * decay + _dot_tn(k_ref[rows, hcols(h)], vh * _col(ed, GDN_HEADS + h))
    yield


def _interleave(order, streams):
    for ch in order:
        next(streams[ch])


def _gdn_scan_kernel(q_ref, k_ref, v_ref, slab_ref, slabt_ref, o_ref, sout_ref, s_ref, tinv_ref, qkg_ref, *, tl):
    l = pl.program_id(1)
    nchunk = tl // CHUNK

    @pl.when(l == 0)
    def _():
        s_ref[...] = jnp.zeros(s_ref.shape, F32)

    ri = lax.broadcasted_iota(jnp.int32, (CHUNK, 2 * CHUNK), 0)
    li = lax.broadcasted_iota(jnp.int32, (CHUNK, 2 * CHUNK), 1)
    ci = li & (CHUNK - 1)
    masks = (ri >= ci, ri > ci, (ri == ci).astype(F32), li < CHUNK)

    prep = functools.partial(_gdn_prep_stages, k_ref=k_ref, q_ref=q_ref, slab_ref=slab_ref, slabt_ref=slabt_ref,
                             tinv_ref=tinv_ref, qkg_ref=qkg_ref, masks=masks)
    state = functools.partial(_gdn_state_stages, q_ref=q_ref, k_ref=k_ref, v_ref=v_ref, slab_ref=slab_ref,
                              slabt_ref=slabt_ref, tinv_ref=tinv_ref, qkg_ref=qkg_ref, o_ref=o_ref, s_ref=s_ref)

    _interleave("A" * 7, {"A": prep(0, 0)})

    def chunk_body(c, carry):
        slot = c & 1
        nxt = jnp.minimum(c + 1, nchunk - 1)
        _interleave("ABAABAABAA", {"A": prep(nxt, 1 - slot), "B": state(c, slot)})
        return carry

    lax.fori_loop(0, nchunk, chunk_body, 0)

    @pl.when(l == pl.num_programs(1) - 1)
    def _():
        sout_ref[...] = s_ref[...]


def _gdn_scan(q, k, v, slab, slabt, tl):
    bsz, seq, _ = q.shape
    row = lambda w: pl.BlockSpec((None, tl, w), lambda b, l: (b, l, 0))
    pair_scratch = pltpu.VMEM((2, GDN_HEADS // 2, CHUNK, 2 * CHUNK), BF16)
    return pl.pallas_call(
        functools.partial(_gdn_scan_kernel, tl=tl),
        grid=(bsz, seq // tl),
        in_specs=[row(GDN_QK), row(GDN_QK), row(GDN_VD), row(LANES),
                  pl.BlockSpec((None, tl // CHUNK, 2 * GDN_HEADS, CHUNK), lambda b, l: (b, l, 0, 0))],
        out_specs=(row(GDN_VD), pl.BlockSpec((None, GDN_HEADS, GDN_DK, GDN_DV), lambda b, l: (b, 0, 0, 0))),
        out_shape=(jax.ShapeDtypeStruct((bsz, seq, GDN_VD), BF16),
                   jax.ShapeDtypeStruct((bsz, GDN_HEADS, GDN_DK, GDN_DV), F32)),
        scratch_shapes=[pltpu.VMEM((GDN_HEADS, GDN_DK, GDN_DV), F32), pair_scratch, pair_scratch],
        compiler_params=_params(2),
        name="gdn_scan",
    )(q, k, v, slab, slabt)


def _gdn_out_kernel(o_ref, z_ref, x_ref, nw_ref, wout_ref, g_ref, b_ref, y_ref):
    o = o_ref[...].astype(F32)
    z = z_ref[...].astype(F32)
    parts = []
    for h in range(GDN_HEADS):
        hs = slice(h * GDN_DV, (h + 1) * GDN_DV)
        oh = o[:, hs]
        rn = oh * lax.rsqrt(jnp.mean(oh * oh, axis=-1, keepdims=True) + RMS_EPS) * nw_ref[...]
        parts.append((rn * _silu(z[:, hs])).astype(BF16))
    mix = jnp.dot(jnp.concatenate(parts, axis=1), wout_ref[...], preferred_element_type=F32)
    y_ref[...] = _layer_norm(DN_ALPHA * x_ref[...] + mix, g_ref[...], b_ref[...])


def _gdn_out(o, z, x, p, ln_g, ln_b, tl):
    bsz, seq, _ = x.shape
    row = lambda w: pl.BlockSpec((None, tl, w), lambda b, l: (b, l, 0))
    consts = (p["nw"], p["wout"], ln_g, ln_b)
    return pl.pallas_call(
        _gdn_out_kernel,
        grid=(bsz, seq // tl),
        in_specs=[row(GDN_VD), row(GDN_VD), row(D_MODEL)] + [_const_spec(c.shape) for c in consts],
        out_specs=row(D_MODEL),
        out_shape=jax.ShapeDtypeStruct((bsz, seq, D_MODEL), F32),
        compiler_params=_params(2),
        name="gdn_out",
    )(o, z, x, *consts)


def _ffn_kernel(x_ref, wup_ref, cw_ref, cb_ref, wdown_ref, g_ref, b_ref, y_ref, cache_ref, gg_ref, *, tl):
    l = pl.program_id(1)

    @pl.when(l == 0)
    def _():
        gg_ref[0:SUBLANES, :] = jnp.zeros((SUBLANES, D_FF), F32)

    x = x_ref[...]
    xb = x.astype(BF16)
    acc = jnp.zeros((tl, D_MODEL), F32)
    for j in range(D_FF // FFN_BLOCK):
        cs = slice(j * FFN_BLOCK, (j + 1) * FFN_BLOCK)
        vs = slice(D_FF + j * FFN_BLOCK, D_FF + (j + 1) * FFN_BLOCK)
        gg_ref[SUBLANES:SUBLANES + tl, cs] = jnp.dot(xb, wup_ref[:, cs], preferred_element_type=F32)
        val = jnp.dot(xb, wup_ref[:, vs], preferred_element_type=F32)
        hid = _silu(_causal_conv(gg_ref, cw_ref, cb_ref, cs, tl, FFN_CONV_W)) * val
        acc = acc + jnp.dot(hid.astype(BF16), wdown_ref[cs, :], preferred_element_type=F32)
    cache_ref[...] = gg_ref[SUBLANES + tl - (FFN_CONV_W - 1):SUBLANES + tl, :]
    gg_ref[0:SUBLANES, :] = gg_ref[tl:tl + SUBLANES, :]
    y_ref[...] = _layer_norm(DN_ALPHA * x + acc, g_ref[...], b_ref[...])


def _ffn(x, p, ln_g, ln_b, tl):
    bsz, seq, _ = x.shape
    row = lambda w: pl.BlockSpec((None, tl, w), lambda b, l: (b, l, 0))
    consts = (p["wup"], p["cw"], p["cb"], p["wdown"], ln_g, ln_b)
    return pl.pallas_call(
        functools.partial(_ffn_kernel, tl=tl),
        grid=(bsz, seq // tl),
        in_specs=[row(D_MODEL)] + [_const_spec(c.shape) for c in consts],
        out_specs=(row(D_MODEL), pl.BlockSpec((None, FFN_CONV_W - 1, D_FF), lambda b, l: (b, 0, 0))),
        out_shape=(jax.ShapeDtypeStruct((bsz, seq, D_MODEL), F32),
                   jax.ShapeDtypeStruct((bsz, FFN_CONV_W - 1, D_FF), F32)),
        scratch_shapes=[pltpu.VMEM((tl + SUBLANES, D_FF), F32)],
        compiler_params=_params(2),
        name="ffn",
    )(x, *consts)


def _ssd_in_kernel(x_ref, wz_ref, wxbc_ref, wdt_ref, cw_ref, cb_ref, a_ref, dtb_ref, triu_ref,
                   z_ref, xs_ref, bm_ref, cm_ref, slab_ref, slabt_ref, cache_ref, q3_ref, carry_ref, a3_ref, *, tl):
    l = pl.program_id(1)
    xb = x_ref[...].astype(BF16)
    lane = lax.broadcasted_iota(jnp.int32, (tl, LANES), 1)

    dt = _softplus(jnp.dot(xb, wdt_ref[...], preferred_element_type=F32) + dtb_ref[...])
    dt = jnp.where(lane < 2 * SSD_HEADS, dt, 0.0)
    _gate_slabs(jnp.where(lane < SSD_HEADS, dt, dt * a_ref[...]), SSD_HEADS, triu_ref[...], slab_ref, slabt_ref, tl)

    def z_proj():
        z_ref[...] = jnp.dot(xb, wz_ref[...], preferred_element_type=F32).astype(BF16)

    def emit(blk, rows, piece):
        col = blk * LANES
        if col < SSD_INNER:
            xs_ref[rows, col:col + LANES] = piece
        elif col < SSD_INNER + SSD_GN:
            bm_ref[rows, col - SSD_INNER:col - SSD_INNER + LANES] = piece.astype(BF16)
        else:
            cm_ref[rows, col - SSD_INNER - SSD_GN:col - SSD_INNER - SSD_GN + LANES] = piece.astype(BF16)

    _permuted_conv(xb, wxbc_ref, cw_ref, cb_ref, cache_ref, q3_ref, carry_ref, a3_ref, l == 0, tl, emit, z_proj)


def _ssd_in(x, p, tl):
    bsz, seq, _ = x.shape
    triu = _chunk_cumsum_matrix(tl)
    row = lambda w: pl.BlockSpec((None, tl, w), lambda b, l: (b, l, 0))
    out_shape = (jax.ShapeDtypeStruct((bsz, seq, SSD_INNER), BF16), jax.ShapeDtypeStruct((bsz, seq, SSD_INNER), F32),
                 jax.ShapeDtypeStruct((bsz, seq, SSD_GN), BF16), jax.ShapeDtypeStruct((bsz, seq, SSD_GN), BF16),
                 jax.ShapeDtypeStruct((bsz, seq, LANES), F32),
                 jax.ShapeDtypeStruct((bsz, seq // CHUNK, 2 * SSD_HEADS, CHUNK), F32),
                 jax.ShapeDtypeStruct((bsz, CONV_W - 1, SSD_CONV_DIM), F32))
    out_specs = (row(SSD_INNER), row(SSD_INNER), row(SSD_GN), row(SSD_GN), row(LANES),
                 pl.BlockSpec((None, tl // CHUNK, 2 * SSD_HEADS, CHUNK), lambda b, l: (b, l, 0, 0)),
                 pl.BlockSpec((None, CONV_W - 1, SSD_CONV_DIM), lambda b, l: (b, 0, 0)))
    consts = (p["wz"], p["wxbc"], p["wdt"], p["cw"], p["cb"], p["a"], p["dtb"], triu)
    return pl.pallas_call(
        functools.partial(_ssd_in_kernel, tl=tl),
        grid=(bsz, seq // tl),
        in_specs=[row(D_MODEL)] + [_const_spec(c.shape) for c in consts],
        out_specs=out_specs,
        out_shape=out_shape,
        scratch_shapes=_conv_scratch(SSD_CONV_DIM, tl),
        compiler_params=_params(2),
        name="ssd_in",
    )(x, *consts)


def _ssd_scan_kernel(xs_ref, bm_ref, cm_ref, slab_ref, slabt_ref, edt_ref, eac_ref, dskip_ref,
                     y_ref, sout_ref, st_ref, dtx_ref, acx_ref, *, tl):
    l = pl.program_id(1)

    @pl.when(l == 0)
    def _():
        st_ref[...] = jnp.zeros(st_ref.shape, F32)

    hi, mid, lo = _split3(slab_ref[...])
    dtx_ref[...] = jnp.dot(jnp.concatenate([hi, mid], axis=1), edt_ref[...], preferred_element_type=F32)
    acx_ref[...] = jnp.dot(jnp.concatenate([hi, mid, lo], axis=1), eac_ref[...], preferred_element_type=F32)

    ri = lax.broadcasted_iota(jnp.int32, (CHUNK, 2 * CHUNK), 0)
    li = lax.broadcasted_iota(jnp.int32, (CHUNK, 2 * CHUNK), 1)
    incl = ri >= (li & (CHUNK - 1))
    left = li < CHUNK

    def chunk_body(c, carry):
        rows = pl.ds(pl.multiple_of(c * CHUNK, CHUNK), CHUNK)
        sl = slab_ref[rows, :]
        st = slabt_ref[c]
        ac_x = acx_ref[rows, :]
        xs = xs_ref[rows, :]
        xdt = xs * dtx_ref[rows, :]
        alast_x = ac_x[CHUNK - 1:CHUNK, :]
        xdec = (xdt * jnp.exp(alast_x - ac_x)).astype(BF16)
        xdt_b = xdt.astype(BF16)
        ea = jnp.exp(ac_x)
        sdecay = jnp.exp(alast_x)
        cb2, y_off = [], []
        for g in range(SSD_GROUPS):
            gs = slice(g * SSD_GROUP_W, (g + 1) * SSD_GROUP_W)
            ns = slice(g * SSD_STATE, (g + 1) * SSD_STATE)
            cg = cm_ref[rows, ns]
            bg = bm_ref[rows, ns]
            cb2.append(_dot_nt(cg, jnp.concatenate([bg, bg], axis=0)))
            st_g = st_ref[:, gs]
            y_off.append(jnp.dot(cg, st_g.astype(BF16), preferred_element_type=F32))
            st_ref[:, gs] = st_g * sdecay[:, gs] + _dot_tn(bg, xdec[:, gs])
        for g in range(SSD_GROUPS):
            gs = slice(g * SSD_GROUP_W, (g + 1) * SSD_GROUP_W)
            diag = []
            for pp in range(SSD_HPG // 2):
                ha = g * SSD_HPG + 2 * pp
                hb = ha + 1
                acol = jnp.where(left, _col(sl, SSD_HEADS + ha), _col(sl, SSD_HEADS + hb))
                arow = jnp.concatenate([st[SSD_HEADS + ha:SSD_HEADS + ha + 1, :],
                                        st[SSD_HEADS + hb:SSD_HEADS + hb + 1, :]], axis=1)
                seg = jnp.exp(jnp.where(incl, acol - arow, NEG_BIG))
                w = _pair_blockdiag(xdt_b[:, ha * SSD_HEADDIM:(ha + 2) * SSD_HEADDIM], left)
                diag.append(jnp.dot((seg * cb2[g]).astype(BF16), w, preferred_element_type=F32))
            y_ref[rows, gs] = (jnp.concatenate(diag, axis=1) + y_off[g] * ea[:, gs]
                               + dskip_ref[:, gs] * xs[:, gs]).astype(BF16)
        return carry

    lax.fori_loop(0, tl // CHUNK, chunk_body, 0)

    @pl.when(l == pl.num_programs(1) - 1)
    def _():
        sout_ref[...] = st_ref[...].T.reshape(SSD_HEADS, SSD_HEADDIM, SSD_STATE)


def _ssd_scan(xs, bm, cm, slab, slabt, p, tl):
    bsz, seq, _ = xs.shape
    row = lambda w: pl.BlockSpec((None, tl, w), lambda b, l: (b, l, 0))
    consts = (jnp.tile(p["edt"], (2, 1)), jnp.tile(p["eac"], (3, 1)), p["dskip"])
    return pl.pallas_call(
        functools.partial(_ssd_scan_kernel, tl=tl),
        grid=(bsz, seq // tl),
        in_specs=[row(SSD_INNER), row(SSD_GN), row(SSD_GN), row(LANES),
                  pl.BlockSpec((None, tl // CHUNK, 2 * SSD_HEADS, CHUNK), lambda b, l: (b, l, 0, 0))]
                 + [_const_spec(c.shape) for c in consts],
        out_specs=(row(SSD_INNER),
                   pl.BlockSpec((None, SSD_HEADS, SSD_HEADDIM, SSD_STATE), lambda b, l: (b, 0, 0, 0))),
        out_shape=(jax.ShapeDtypeStruct((bsz, seq, SSD_INNER), BF16),
                   jax.ShapeDtypeStruct((bsz, SSD_HEADS, SSD_HEADDIM, SSD_STATE), F32)),
        scratch_shapes=[pltpu.VMEM((SSD_STATE, SSD_INNER), F32), pltpu.VMEM((tl, SSD_INNER), F32),
                        pltpu.VMEM((tl, SSD_INNER), F32)],
        compiler_params=_params(2),
        name="ssd_scan",
    )(xs, bm, cm, slab, slabt, *consts)


def _ssd_out_kernel(y_ref, z_ref, x_ref, nw_ref, wout_ref, g_ref, b_ref, o_ref):
    t = y_ref[...].astype(F32) * _silu(z_ref[...].astype(F32))
    parts = []
    for g in range(SSD_GROUPS):
        gs = slice(g * SSD_GROUP_W, (g + 1) * SSD_GROUP_W)
        tg = t[:, gs]
        parts.append((tg * lax.rsqrt(jnp.mean(tg * tg, axis=-1, keepdims=True) + RMS_EPS) * nw_ref[:, gs]).astype(BF16))
    mix = jnp.dot(jnp.concatenate(parts, axis=1), wout_ref[...], preferred_element_type=F32)
    o_ref[...] = _layer_norm(DN_ALPHA * x_ref[...] + mix, g_ref[...], b_ref[...])


def _ssd_out(y, z, x, p, ln_g, ln_b, tl):
    bsz, seq, _ = x.shape
    row = lambda w: pl.BlockSpec((None, tl, w), lambda b, l: (b, l, 0))
    consts = (p["nw"], p["wout"], ln_g, ln_b)
    return pl.pallas_call(
        _ssd_out_kernel,
        grid=(bsz, seq // tl),
        in_specs=[row(SSD_INNER), row(SSD_INNER), row(D_MODEL)] + [_const_spec(c.shape) for c in consts],
        out_specs=row(D_MODEL),
        out_shape=jax.ShapeDtypeStruct((bsz, seq, D_MODEL), F32),
        compiler_params=_params(2),
        name="ssd_out",
    )(y, z, x, *consts)


def _step_conv(pre, c_refs, cw_ref, cb_ref):
    width = len(c_refs) + 1
    acc = cb_ref[...] + c_refs[0][...] * cw_ref[0:1, :]
    for k in range(1, width - 1):
        acc = acc + c_refs[k][...] * cw_ref[k:k + 1, :]
    return acc + pre * cw_ref[width - 1:width, :]


def _gdn_in_step_kernel(x_ref, c0_ref, c1_ref, c2_ref, wqkv_ref, wz_ref, wba_ref, cw_ref, cb_ref, alog_ref, dtb_ref,
                        q_ref, k_ref, v_ref, z_ref, slab_ref, pre_ref):
    xb = x_ref[...].astype(BF16)
    n = xb.shape[0]
    lane = lax.broadcasted_iota(jnp.int32, (n, LANES), 1)
    ba = jnp.dot(xb, wba_ref[...], preferred_element_type=F32)
    g = -jnp.exp(alog_ref[...]) * _softplus(ba + dtb_ref[...])
    slab_ref[...] = jnp.where(lane < GDN_HEADS, jax.nn.sigmoid(ba), g)
    z_ref[...] = jnp.dot(xb, wz_ref[...], preferred_element_type=F32)
    pre = jnp.dot(xb, wqkv_ref[...], preferred_element_type=F32)
    pre_ref[...] = pre
    act = _silu(_step_conv(pre, (c0_ref, c1_ref, c2_ref), cw_ref, cb_ref))
    for h in range(GDN_HEADS):
        hs = slice(h * GDN_DK, (h + 1) * GDN_DK)
        q_ref[:, hs] = _l2norm(act[:, hs]) * (GDN_DK ** -0.5)
        k_ref[:, hs] = _l2norm(act[:, GDN_QK + h * GDN_DK:GDN_QK + (h + 1) * GDN_DK])
    v_ref[...] = act[:, 2 * GDN_QK:]


def _gdn_in_step(x, cache, p):
    n = x.shape[0]
    ff = lambda w: jax.ShapeDtypeStruct((n, w), F32)
    return pl.pallas_call(
        _gdn_in_step_kernel,
        out_shape=(ff(GDN_QK), ff(GDN_QK), ff(GDN_VD), ff(GDN_VD), ff(LANES), ff(GDN_CONV_DIM)),
        compiler_params=pltpu.CompilerParams(vmem_limit_bytes=VMEM_LIMIT),
        name="gdn_in_step",
    )(x, cache[:, 0], cache[:, 1], cache[:, 2], p["wqkv"], p["wz"], p["wba"], p["cw"], p["cb"], p["alog"], p["dtb"])


def _gdn_step_kernel(q_ref, k_ref, v_ref, slab_ref, s_ref, o_ref, sout_ref, *, tb):
    zpad = jnp.zeros((LANES - GDN_HEADS, GDN_DK), F32)

    def token_body(t, carry):
        kt = jnp.concatenate([k_ref[t], zpad], axis=0).T
        qt = jnp.concatenate([q_ref[t], zpad], axis=0).T
        v = v_ref[t]
        sl = slab_ref[pl.ds(t, 1), :]
        alpha = jnp.exp(sl)
        for h in range(GDN_HEADS):
            s = s_ref[t, h]
            kc = _col(kt, h)
            a_h = alpha[:, GDN_HEADS + h:GDN_HEADS + h + 1]
            ks = jnp.sum(s * kc, axis=0, keepdims=True)
            v_new = sl[:, h:h + 1] * (v[h:h + 1, :] - a_h * ks)
            s_new = s * a_h + kc * v_new
            sout_ref[t, h] = s_new
            o_ref[t, h:h + 1, :] = jnp.sum(s_new * _col(qt, h), axis=0, keepdims=True)
        return carry

    lax.fori_loop(0, tb, token_body, 0)


def _gdn_step(q, k, v, slab, state, tb):
    n = q.shape[0]
    q3, k3, v3 = (t.reshape(n, GDN_HEADS, GDN_DK) for t in (q, k, v))
    tok = pl.BlockSpec((tb, GDN_HEADS, GDN_DK), lambda i: (i, 0, 0))
    st = pl.BlockSpec((tb, GDN_HEADS, GDN_DK, GDN_DV), lambda i: (i, 0, 0, 0))
    o, s_new = pl.pallas_call(
        functools.partial(_gdn_step_kernel, tb=tb),
        grid=(n // tb,),
        in_specs=[tok, tok, tok, pl.BlockSpec((tb, LANES), lambda i: (i, 0)), st],
        out_specs=(tok, st),
        out_shape=(jax.ShapeDtypeStruct((n, GDN_HEADS, GDN_DV), F32), jax.ShapeDtypeStruct(state.shape, F32)),
        compiler_params=_params(1),
        name="gdn_step",
    )(q3, k3, v3, slab, state)
    return o.reshape(n, GDN_VD), s_new


def _ffn_step_kernel(x_ref, c0_ref, c1_ref, wup_ref, cw_ref, cb_ref, wdown_ref, g_ref, b_ref, y_ref, pre_ref):
    x = x_ref[...]
    xb = x.astype(BF16)
    gate = jnp.dot(xb, wup_ref[:, 0:D_FF], preferred_element_type=F32)
    val = jnp.dot(xb, wup_ref[:, D_FF:2 * D_FF], preferred_element_type=F32)
    pre_ref[...] = gate
    hid = _silu(_step_conv(gate, (c0_ref, c1_ref), cw_ref, cb_ref)) * val
    y_ref[...] = _layer_norm(DN_ALPHA * x + jnp.dot(hid.astype(BF16), wdown_ref[...], preferred_element_type=F32),
                             g_ref[...], b_ref[...])


def _ffn_step(x, cache, p, ln_g, ln_b):
    n = x.shape[0]
    return pl.pallas_call(
        _ffn_step_kernel,
        out_shape=(jax.ShapeDtypeStruct((n, D_MODEL), F32), jax.ShapeDtypeStruct((n, D_FF), F32)),
        compiler_params=pltpu.CompilerParams(vmem_limit_bytes=VMEM_LIMIT),
        name="ffn_step",
    )(x, cache[:, 0], cache[:, 1], p["wup"], p["cw"], p["cb"], p["wdown"], ln_g, ln_b)


def _ssd_in_step_kernel(x_ref, c0_ref, c1_ref, c2_ref, wz_ref, wxbc_ref, wdt_ref, cw_ref, cb_ref, a_ref, dtb_ref,
                        edt_ref, eac_ref, z_ref, xs_ref, bm_ref, cm_ref, xdt_ref, dec_ref, pre_ref):
    xb = x_ref[...].astype(BF16)
    n = xb.shape[0]
    lane = lax.broadcasted_iota(jnp.int32, (n, LANES), 1)
    dt = _softplus(jnp.dot(xb, wdt_ref[...], preferred_element_type=F32) + dtb_ref[...])
    dt = jnp.where(lane < 2 * SSD_HEADS, dt, 0.0)
    slab = jnp.where(lane < SSD_HEADS, dt, dt * a_ref[...])
    z_ref[...] = jnp.dot(xb, wz_ref[...], preferred_element_type=F32)
    pre = jnp.dot(xb, wxbc_ref[...], preferred_element_type=F32)
    pre_ref[...] = pre
    act = _silu(_step_conv(pre, (c0_ref, c1_ref, c2_ref), cw_ref, cb_ref))
    xs = act[:, 0:SSD_INNER]
    xs_ref[...] = xs
    bm_ref[...] = act[:, SSD_INNER:SSD_INNER + SSD_GN]
    cm_ref[...] = act[:, SSD_INNER + SSD_GN:]
    xdt_ref[...] = xs * _dot_exact_rhs(slab, edt_ref[...])
    dec_ref[...] = jnp.exp(_dot_exact_rhs(slab, eac_ref[...]))


def _ssd_in_step(x, cache, p):
    n = x.shape[0]
    ff = lambda w: jax.ShapeDtypeStruct((n, w), F32)
    return pl.pallas_call(
        _ssd_in_step_kernel,
        out_shape=(ff(SSD_INNER), ff(SSD_INNER), ff(SSD_GN), ff(SSD_GN), ff(SSD_INNER), ff(SSD_INNER),
                   ff(SSD_CONV_DIM)),
        compiler_params=pltpu.CompilerParams(vmem_limit_bytes=VMEM_LIMIT),
        name="ssd_in_step",
    )(x, cache[:, 0], cache[:, 1], cache[:, 2], p["wz"], p["wxbc"], p["wdt"], p["cw"], p["cb"], p["a"], p["dtb"],
      p["edt"], p["eac"])


def _ssd_step_kernel(xs_ref, bm_ref, cm_ref, xdt_ref, dec_ref, dskip_ref, s_ref, y_ref, sout_ref, *, tb):
    hp = SSD_INNER
    rid = lax.broadcasted_iota(jnp.int32, (SUBLANES, hp), 0)
    gid = lax.broadcasted_iota(jnp.int32, (SUBLANES, hp), 1) // SSD_GROUP_W
    zpad_r = jnp.zeros((LANES - SUBLANES, hp), F32)

    def token_body(t, carry):
        row = pl.ds(t, 1)
        xdt = xdt_ref[row, :]
        dec = dec_ref[row, :]
        stack = jnp.where(rid == gid, jnp.broadcast_to(xdt, (SUBLANES, hp)), 0.0)
        stack = jnp.where(rid == SSD_GROUPS, jnp.broadcast_to(dec, (SUBLANES, hp)), stack)
        cols = jnp.concatenate([stack, zpad_r], axis=0).T
        s = s_ref[t].reshape(hp, SSD_STATE)
        bm = bm_ref[row, :]
        upd = jnp.concatenate(
            [_col(cols[g * SSD_GROUP_W:(g + 1) * SSD_GROUP_W, :], g) * bm[:, g * SSD_STATE:(g + 1) * SSD_STATE]
             for g in range(SSD_GROUPS)], axis=0)
        s_new = s * _col(cols, SSD_GROUPS) + upd
        sout_ref[t] = s_new.reshape(SSD_HEADS, SSD_HEADDIM, SSD_STATE)
        cm = cm_ref[row, :]
        cmat = jnp.concatenate([cm[:, g * SSD_STATE:(g + 1) * SSD_STATE] for g in range(SSD_GROUPS)]
                               + [jnp.zeros((SUBLANES - SSD_GROUPS, SSD_STATE), F32)], axis=0)
        yall = _dot_nt(cmat, s_new)
        y = jnp.sum(jnp.where(rid == gid, yall, 0.0), axis=0, keepdims=True)
        y_ref[row, :] = y + dskip_ref[...] * xs_ref[row, :]
        return carry

    lax.fori_loop(0, tb, token_body, 0)


def _ssd_step(xs, bm, cm, xdt, dec, state, p, tb):
    n = xs.shape[0]
    tok = lambda w: pl.BlockSpec((tb, w), lambda i: (i, 0))
    st = pl.BlockSpec((tb, SSD_HEADS, SSD_HEADDIM, SSD_STATE), lambda i: (i, 0, 0, 0))
    return pl.pallas_call(
        functools.partial(_ssd_step_kernel, tb=tb),
        grid=(n // tb,),
        in_specs=[tok(SSD_INNER), tok(SSD_GN), tok(SSD_GN), tok(SSD_INNER), tok(SSD_INNER),
                  _const_spec(p["dskip"].shape), st],
        out_specs=(tok(SSD_INNER), st),
        out_shape=(jax.ShapeDtypeStruct((n, SSD_INNER), F32), jax.ShapeDtypeStruct(state.shape, F32)),
        compiler_params=_params(1),
        name="ssd_step",
    )(xs, bm, cm, xdt, dec, p["dskip"], state)


def _row(v, width=None, offset=0):
    v = v.astype(F32).reshape(1, -1)
    if width is None:
        return v
    return jnp.pad(v, ((0, 0), (offset, width - offset - v.shape[1])))


def _prep_gdn(w_in, conv_w, conv_b, a_log, dt_bias, norm_w, w_out):
    ba = w_in[:, GDN_CONV_DIM + GDN_VD:]
    return {
        "wqkv": w_in[:, :GDN_CONV_DIM].astype(BF16),
        "wz": w_in[:, GDN_CONV_DIM:GDN_CONV_DIM + GDN_VD].astype(BF16),
        "wba": jnp.pad(ba, ((0, 0), (0, LANES - ba.shape[1]))).astype(BF16),
        "cw": conv_w.astype(F32),
        "cb": _row(conv_b),
        "alog": _row(a_log, LANES, GDN_HEADS),
        "dtb": _row(dt_bias, LANES, GDN_HEADS),
        "nw": _row(norm_w),
        "wout": w_out.astype(BF16),
    }


def _prep_ssd(w_in, conv_w, conv_b, a_log, dt_bias, d_skip, norm_w, w_out):
    wdt = w_in[:, SSD_INNER + SSD_CONV_DIM:]
    wdt2 = jnp.concatenate([wdt, wdt], axis=1)
    head_of_lane = jnp.arange(SSD_INNER) // SSD_HEADDIM
    sel = jnp.arange(LANES)[:, None]
    return {
        "wz": w_in[:, :SSD_INNER].astype(BF16),
        "wxbc": w_in[:, SSD_INNER:SSD_INNER + SSD_CONV_DIM].astype(BF16),
        "wdt": jnp.pad(wdt2, ((0, 0), (0, LANES - 2 * SSD_HEADS))).astype(BF16),
        "cw": conv_w.astype(F32),
        "cb": _row(conv_b),
        "a": _row(-jnp.exp(a_log.astype(F32)), LANES, SSD_HEADS),
        "dtb": _row(jnp.concatenate([dt_bias, dt_bias]), LANES, 0),
        "edt": (sel == head_of_lane[None, :]).astype(BF16),
        "eac": (sel == head_of_lane[None, :] + SSD_HEADS).astype(BF16),
        "dskip": _row(jnp.repeat(d_skip, SSD_HEADDIM)),
        "nw": _row(norm_w),
        "wout": w_out.astype(BF16),
    }


def _prep_ffn(w_up, conv_w, conv_b, w_down):
    return {"wup": w_up.astype(BF16), "cw": conv_w.astype(F32), "cb": _row(conv_b), "wdown": w_down.astype(BF16)}


def _prompt_trunk(x, gdn, ssd, ffn, ln, tl):
    q, k, v, z, slab, slabt, gdn_cache = _gdn_in(x, gdn, tl)
    o, gdn_state = _gdn_scan(q, k, v, slab, slabt, min(GDN_SCAN_TILE, x.shape[1]))
    x = _gdn_out(o, z, x, gdn, ln[0][0], ln[0][1], tl)
    x, ffn_cache0 = _ffn(x, ffn[0], ln[0][2], ln[0][3], tl)
    z, xs, bm, cm, slab, slabt, ssd_cache = _ssd_in(x, ssd, tl)
    y, ssd_state = _ssd_scan(xs, bm, cm, slab, slabt, ssd, tl)
    x = _ssd_out(y, z, x, ssd, ln[1][0], ln[1][1], tl)
    x, ffn_cache1 = _ffn(x, ffn[1], ln[1][2], ln[1][3], tl)
    return x, gdn_cache, gdn_state, ssd_cache, ssd_state, jnp.stack([ffn_cache0, ffn_cache1])


def _sample_trunk(x, gdn_cache, gdn_state, ssd_cache, ssd_state, ffn_cache, gdn, ssd, ffn, ln, tb):
    n = x.shape[0]
    q, k, v, z, slab, pre = _gdn_in_step(x, gdn_cache, gdn)
    gdn_cache_new = jnp.concatenate([gdn_cache[:, 1:], pre[:, None]], axis=1)
    o, gdn_state_new = _gdn_step(q, k, v, slab, gdn_state, tb)
    x = _gdn_out(o[None], z[None], x[None], gdn, ln[0][0], ln[0][1], n)[0]
    x, pre = _ffn_step(x, ffn_cache[0], ffn[0], ln[0][2], ln[0][3])
    ffn_cache0 = jnp.concatenate([ffn_cache[0][:, 1:], pre[:, None]], axis=1)
    z, xs, bm, cm, xdt, dec, pre = _ssd_in_step(x, ssd_cache, ssd)
    ssd_cache_new = jnp.concatenate([ssd_cache[:, 1:], pre[:, None]], axis=1)
    y, ssd_state_new = _ssd_step(xs, bm, cm, xdt, dec, ssd_state, ssd, tb)
    x = _ssd_out(y[None], z[None], x[None], ssd, ln[1][0], ln[1][1], n)[0]
    x, pre = _ffn_step(x, ffn_cache[1], ffn[1], ln[1][2], ln[1][3])
    ffn_cache1 = jnp.concatenate([ffn_cache[1][:, 1:], pre[:, None]], axis=1)
    return x, gdn_cache_new, gdn_state_new, ssd_cache_new, ssd_state_new, jnp.stack([ffn_cache0, ffn_cache1])


PROMPT_TILE = 512
GDN_SCAN_TILE = 1024
SAMPLE_TOKENS = 8


def kernel(x_prompt, x_sample, cache_gdn_conv, state_gdn, cache_ssd_conv, state_ssd, cache_ffn_conv, gdn_w_in, gdn_conv_w, gdn_conv_b, gdn_a_log, gdn_dt_bias, gdn_norm_w, gdn_w_out, ssd_w_in, ssd_conv_w, ssd_conv_b, ssd_a_log, ssd_dt_bias, ssd_d, ssd_norm_w, ssd_w_out, ffn_w_up, ffn_conv_w, ffn_conv_b, ffn_w_down, ln1_g, ln1_b, ln2_g, ln2_b):
    gdn = _prep_gdn(gdn_w_in[0], gdn_conv_w[0], gdn_conv_b[0], gdn_a_log[0], gdn_dt_bias[0], gdn_norm_w[0],
                    gdn_w_out[0])
    ssd = _prep_ssd(ssd_w_in[0], ssd_conv_w[0], ssd_conv_b[0], ssd_a_log[0], ssd_dt_bias[0], ssd_d[0],
                    ssd_norm_w[0], ssd_w_out[0])
    ffn = [_prep_ffn(ffn_w_up[i], ffn_conv_w[i], ffn_conv_b[i], ffn_w_down[i]) for i in range(DEPTH)]
    ln = [(_row(ln1_g[i]), _row(ln1_b[i]), _row(ln2_g[i]), _row(ln2_b[i])) for i in range(DEPTH)]

    tl = min(PROMPT_TILE, x_prompt.shape[1])
    y_p, gcp, gsp, scp, ssp, fcp = _prompt_trunk(x_prompt, gdn, ssd, ffn, ln, tl)
    y_s, gcs, gss, scs, sss, fcs = _sample_trunk(
        x_sample[:, 0], cache_gdn_conv[0], state_gdn[0], cache_ssd_conv[0], state_ssd[0], cache_ffn_conv,
        gdn, ssd, ffn, ln, min(SAMPLE_TOKENS, x_sample.shape[0]))
    return (y_p, y_s[:, None], gcp[None], gcs[None], gsp[None], gss[None], scp[None], scs[None],
            ssp[None], sss[None], fcp, fcs)
```

```python
import functools

import jax
import jax.numpy as jnp
from jax import lax
from jax.experimental import pallas as pl
from jax.experimental.pallas import tpu as pltpu

F32 = jnp.float32
BF16 = jnp.bfloat16

D_MODEL = 1024
DEPTH = 2
CONV_W = 4
CHUNK = 64

GDN_HEADS = 8
GDN_DK = 128
GDN_DV = 128
GDN_QK = GDN_HEADS * GDN_DK
GDN_VD = GDN_HEADS * GDN_DV
GDN_CONV_DIM = 2 * GDN_QK + GDN_VD

SSD_INNER = 2 * D_MODEL
SSD_HEADDIM = 64
SSD_HEADS = SSD_INNER // SSD_HEADDIM
SSD_GROUPS = 4
SSD_STATE = 128
SSD_HPG = SSD_HEADS // SSD_GROUPS
SSD_GN = SSD_GROUPS * SSD_STATE
SSD_CONV_DIM = SSD_INNER + 2 * SSD_GN
SSD_GROUP_W = SSD_INNER // SSD_GROUPS

D_FF = 2816
FFN_CONV_W = 3
FFN_BLOCK = D_FF // 2

DN_ALPHA = (2 * DEPTH) ** 0.25
LN_EPS = 1e-5
RMS_EPS = 1e-6
L2_EPS = 1e-6

LANES = 128
SUBLANES = 8
VMEM_LIMIT = 56 * 1024 * 1024

NEG_BIG = -1e30


def _dot_nt(a, b):
    return lax.dot_general(a.astype(BF16), b.astype(BF16), (((1,), (1,)), ((), ())),
                           preferred_element_type=F32)


def _dot_tn(a, b):
    return lax.dot_general(a.astype(BF16), b.astype(BF16), (((0,), (0,)), ((), ())),
                           preferred_element_type=F32)


def _split3(x):
    hi = x.astype(BF16)
    r1 = x - hi.astype(F32)
    mid = r1.astype(BF16)
    lo = (r1 - mid.astype(F32)).astype(BF16)
    return hi, mid, lo


def _dot_exact_rhs(x, e):
    hi, mid, lo = _split3(x)
    return (jnp.dot(hi, e, preferred_element_type=F32) + jnp.dot(mid, e, preferred_element_type=F32)
            + jnp.dot(lo, e, preferred_element_type=F32))


def _silu(x):
    return x * jax.nn.sigmoid(x)


def _softplus(x):
    return jnp.maximum(x, 0.0) + jnp.log1p(jnp.exp(-jnp.abs(x)))


def _layer_norm(x, g, b):
    mu = jnp.mean(x, axis=-1, keepdims=True)
    xc = x - mu
    var = jnp.mean(xc * xc, axis=-1, keepdims=True)
    return xc * lax.rsqrt(var + LN_EPS) * g + b


def _l2norm(t):
    return t * lax.rsqrt(jnp.sum(t * t, axis=-1, keepdims=True) + L2_EPS)


def _col(a, i):
    return a[:, i:i + 1]


def _const_spec(shape):
    nd = len(shape)
    return pl.BlockSpec(shape, lambda *_: (0,) * nd)


def _params(n_grid):
    return pltpu.CompilerParams(dimension_semantics=("arbitrary",) * n_grid, vmem_limit_bytes=VMEM_LIMIT)


def _chunk_cumsum_matrix(tl):
    r = jnp.arange(tl)
    same = (r[:, None] // CHUNK) == (r[None, :] // CHUNK)
    return (same & (r[:, None] <= r[None, :])).astype(BF16)


def _gate_slabs(pre, n_keep, triu, slab_ref, slabt_ref, tl):
    pre_t = pre.T[0:2 * n_keep, :]
    cum_t = _dot_exact_rhs(pre_t, triu)
    row = lax.broadcasted_iota(jnp.int32, pre_t.shape, 0)
    st = jnp.where(row < n_keep, pre_t, cum_t)
    for c in range(tl // CHUNK):
        slabt_ref[c] = st[:, c * CHUNK:(c + 1) * CHUNK]
    slab_ref[...] = jnp.concatenate([st, jnp.zeros((LANES - 2 * n_keep, tl), F32)], axis=0).T


def _causal_conv(xx_ref, cw_ref, cb_ref, cs, tl, width):
    base = SUBLANES - (width - 1)
    acc = cb_ref[:, cs] + xx_ref[base:base + tl, cs] * cw_ref[0:1, cs]
    for k in range(1, width):
        acc = acc + xx_ref[base + k:base + k + tl, cs] * cw_ref[k:k + 1, cs]
    return acc


CONV_HEAD = (CONV_W - 1) * SUBLANES
CONV_ROWS = 64


def _permuted_conv(xb, w_ref, cw_ref, cb_ref, cache_ref, q3_ref, carry_ref, a3_ref, first, tl, emit, extra):
    n = tl // SUBLANES
    per = D_MODEL // LANES
    ngroup = w_ref.shape[1] // D_MODEL
    sub0 = lax.broadcasted_iota(jnp.int32, (SUBLANES, LANES), 0) == 0

    @pl.when(first)
    def _():
        carry_ref[...] = jnp.zeros(carry_ref.shape, F32)

    def project(j):
        cs = slice(j * D_MODEL, (j + 1) * D_MODEL)
        pre = jnp.dot(xb, w_ref[:, cs], preferred_element_type=F32)
        cache_ref[:, cs] = pre[tl - (CONV_W - 1):tl, :]
        for c in range(per):
            blk = j * per + c
            for a in range(SUBLANES):
                q3_ref[blk, pl.ds(CONV_HEAD + a, n, stride=SUBLANES), :] = (
                    pre[a * n:(a + 1) * n, c * LANES:(c + 1) * LANES])
            for m in range(CONV_W - 1):
                src = CONV_HEAD + SUBLANES * (n - (CONV_W - 1) + m)
                dst = slice(m * SUBLANES, (m + 1) * SUBLANES)
                rolled = pltpu.roll(q3_ref[blk, src:src + SUBLANES, :], 1, 0)
                q3_ref[blk, dst, :] = jnp.where(sub0, carry_ref[blk, dst, :], rolled)
                carry_ref[blk, dst, :] = rolled

    def convolve(j):
        for c in range(per):
            blk = j * per + c
            cs = slice(blk * LANES, (blk + 1) * LANES)
            for r0 in range(0, tl, CONV_ROWS):
                acc = cb_ref[:, cs] + q3_ref[blk, r0:r0 + CONV_ROWS, :] * cw_ref[0:1, cs]
                for k in range(1, CONV_W):
                    acc = acc + q3_ref[blk, r0 + k * SUBLANES:r0 + k * SUBLANES + CONV_ROWS, :] * cw_ref[k:k + 1, cs]
                a3_ref[c, r0:r0 + CONV_ROWS, :] = _silu(acc)
            for a in range(SUBLANES):
                emit(blk, slice(a * n, (a + 1) * n), a3_ref[c, pl.ds(a, n, stride=SUBLANES), :])

    project(0)
    for j in range(ngroup):
        if j + 1 < ngroup:
            project(j + 1)
        else:
            extra()
        convolve(j)


def _conv_scratch(ncols, tl):
    nblk = ncols // LANES
    return [pltpu.VMEM((nblk, CONV_HEAD + tl, LANES), F32), pltpu.VMEM((nblk, CONV_HEAD, LANES), F32),
            pltpu.VMEM((D_MODEL // LANES, tl, LANES), F32)]


def _gdn_in_kernel(x_ref, wqkv_ref, wz_ref, wba_ref, cw_ref, cb_ref, alog_ref, dtb_ref, triu_ref,
                   q_ref, k_ref, v_ref, z_ref, slab_ref, slabt_ref, cache_ref, q3_ref, carry_ref, a3_ref, *, tl):
    l = pl.program_id(1)
    xb = x_ref[...].astype(BF16)
    lane = lax.broadcasted_iota(jnp.int32, (tl, LANES), 1)

    ba = jnp.dot(xb, wba_ref[...], preferred_element_type=F32)
    g = -jnp.exp(alog_ref[...]) * _softplus(ba + dtb_ref[...])
    _gate_slabs(jnp.where(lane < GDN_HEADS, jax.nn.sigmoid(ba), g), GDN_HEADS, triu_ref[...], slab_ref, slabt_ref, tl)

    def z_proj():
        z_ref[...] = jnp.dot(xb, wz_ref[...], preferred_element_type=F32).astype(BF16)

    def emit(blk, rows, piece):
        j, h = divmod(blk, GDN_HEADS)
        hs = slice(h * GDN_DK, (h + 1) * GDN_DK)
        if j == 0:
            q_ref[rows, hs] = (_l2norm(piece) * (GDN_DK ** -0.5)).astype(BF16)
        elif j == 1:
            k_ref[rows, hs] = _l2norm(piece).astype(BF16)
        else:
            v_ref[rows, hs] = piece

    _permuted_conv(xb, wqkv_ref, cw_ref, cb_ref, cache_ref, q3_ref, carry_ref, a3_ref, l == 0, tl, emit, z_proj)


def _gdn_in(x, p, tl):
    bsz, seq, _ = x.shape
    nl = seq // tl
    triu = _chunk_cumsum_matrix(tl)
    row = lambda w: pl.BlockSpec((None, tl, w), lambda b, l: (b, l, 0))
    bf = lambda w: jax.ShapeDtypeStruct((bsz, seq, w), BF16)
    ff = lambda w: jax.ShapeDtypeStruct((bsz, seq, w), F32)
    out_shape = (bf(GDN_QK), bf(GDN_QK), ff(GDN_VD), bf(GDN_VD), ff(LANES),
                 jax.ShapeDtypeStruct((bsz, seq // CHUNK, 2 * GDN_HEADS, CHUNK), F32),
                 jax.ShapeDtypeStruct((bsz, CONV_W - 1, GDN_CONV_DIM), F32))
    out_specs = (row(GDN_QK), row(GDN_QK), row(GDN_VD), row(GDN_VD), row(LANES),
                 pl.BlockSpec((None, tl // CHUNK, 2 * GDN_HEADS, CHUNK), lambda b, l: (b, l, 0, 0)),
                 pl.BlockSpec((None, CONV_W - 1, GDN_CONV_DIM), lambda b, l: (b, 0, 0)))
    consts = (p["wqkv"], p["wz"], p["wba"], p["cw"], p["cb"], p["alog"], p["dtb"], triu)
    return pl.pallas_call(
        functools.partial(_gdn_in_kernel, tl=tl),
        grid=(bsz, nl),
        in_specs=[row(D_MODEL)] + [_const_spec(c.shape) for c in consts],
        out_specs=out_specs,
        out_shape=out_shape,
        scratch_shapes=_conv_scratch(GDN_CONV_DIM, tl),
        compiler_params=_params(2),
        name="gdn_in",
    )(x, *consts)


def _split2(x):
    hi = x.astype(BF16)
    return hi, (x - hi.astype(F32)).astype(BF16)


def _pair_blockdiag(x, left):
    z = jnp.zeros_like(x)
    return jnp.concatenate([jnp.where(left, x, z), jnp.where(left, z, x)], axis=0)


def _lane_blockdiag(a, b):
    z = jnp.zeros_like(a)
    return jnp.concatenate([jnp.concatenate([a, z], axis=1), jnp.concatenate([z, b], axis=1)], axis=0)


def _pair_matmul(lhs_parts, w_parts, left):
    w_hi = _pair_blockdiag(w_parts[0], left)
    w = jnp.concatenate([w_hi, w_hi, _pair_blockdiag(w_parts[1], left)], axis=0)
    lhs = [jnp.concatenate([hi, lo, hi], axis=1) for hi, lo in lhs_parts]
    r = jnp.dot(jnp.concatenate(lhs, axis=0) if len(lhs) > 1 else lhs[0], w, preferred_element_type=F32)
    return [r[i * CHUNK:(i + 1) * CHUNK] for i in range(len(lhs))]


def _gdn_prep_stages(c, slot, k_ref, q_ref, slab_ref, slabt_ref, tinv_ref, qkg_ref, masks):
    incl, strict, eye2, left = masks
    npair = GDN_HEADS // 2
    rows = pl.ds(pl.multiple_of(c * CHUNK, CHUNK), CHUNK)
    sl = slab_ref[rows, :]
    st = slabt_ref[c]
    ns = []
    for p in range(npair):
        ha, hb = 2 * p, 2 * p + 1
        ka = k_ref[rows, ha * GDN_DK:(ha + 1) * GDN_DK]
        kb = k_ref[rows, hb * GDN_DK:(hb + 1) * GDN_DK]
        kk = jnp.concatenate([_dot_nt(ka, ka), _dot_nt(kb, kb)], axis=1)
        qk = jnp.concatenate([_dot_nt(q_ref[rows, ha * GDN_DK:(ha + 1) * GDN_DK], ka),
                              _dot_nt(q_ref[rows, hb * GDN_DK:(hb + 1) * GDN_DK], kb)], axis=1)
        gcol = jnp.where(left, _col(sl, GDN_HEADS + ha), _col(sl, GDN_HEADS + hb))
        bcol = jnp.where(left, _col(sl, ha), _col(sl, hb))
        grow = jnp.concatenate([st[GDN_HEADS + ha:GDN_HEADS + ha + 1, :], st[GDN_HEADS + hb:GDN_HEADS + hb + 1, :]],
                               axis=1)
        gam = jnp.exp(jnp.where(incl, gcol - grow, NEG_BIG))
        qkg_ref[slot, p] = (qk * gam).astype(BF16)
        ns.append(jnp.where(strict, kk * gam * bcol, 0.0) * -1.0)
    yield
    ps = [eye2 + n for n in ns]
    nparts = [_split2(n) for n in ns]
    ns = [_pair_matmul([nparts[p]], nparts[p], left)[0] for p in range(npair)]
    yield
    k = 2
    while 2 * k < CHUNK:
        nparts = [_split2(n) for n in ns]
        outs = [_pair_matmul([_split2(ps[p]), nparts[p]], nparts[p], left) for p in range(npair)]
        ps = [ps[p] + outs[p][0] for p in range(npair)]
        ns = [outs[p][1] for p in range(npair)]
        k *= 2
        yield
    for p in range(npair):
        t_inv = ps[p] + _pair_matmul([_split2(ps[p])], _split2(ns[p]), left)[0]
        tinv_ref[slot, p] = t_inv.astype(BF16)
    yield


def _gdn_state_stages(c, slot, q_ref, k_ref, v_ref, slab_ref, slabt_ref, tinv_ref, qkg_ref, o_ref, s_ref):
    npair = GDN_HEADS // 2
    rows = pl.ds(pl.multiple_of(c * CHUNK, CHUNK), CHUNK)
    sl = slab_ref[rows, :]
    st = slabt_ref[c]
    eg = jnp.exp(sl)
    ed = jnp.exp(sl[CHUNK - 1:CHUNK, :] - sl)
    hcols = lambda h: slice(h * GDN_DK, (h + 1) * GDN_DK)
    r1 = []
    for h in range(GDN_HEADS):
        lhs = jnp.concatenate([k_ref[rows, hcols(h)], q_ref[rows, hcols(h)]], axis=0)
        r1.append(jnp.dot(lhs, s_ref[h].astype(BF16), preferred_element_type=F32))
    yield
    egc = [_col(eg, GDN_HEADS + h) for h in range(GDN_HEADS)]
    rhs = [(_col(sl, h) * (v_ref[rows, hcols(h)] - egc[h] * r1[h][0:CHUNK])).astype(BF16) for h in range(GDN_HEADS)]
    v_new = [jnp.dot(tinv_ref[slot, p], _lane_blockdiag(rhs[2 * p], rhs[2 * p + 1]), preferred_element_type=F32)
             for p in range(npair)]
    yield
    for p in range(npair):
        va, vb = v_new[p][:, 0:GDN_DV], v_new[p][:, GDN_DV:2 * GDN_DV]
        qs = jnp.concatenate([egc[2 * p] * r1[2 * p][CHUNK:2 * CHUNK],
                              egc[2 * p + 1] * r1[2 * p + 1][CHUNK:2 * CHUNK]], axis=1)
        o_ref[rows, 2 * p * GDN_DV:(2 * p + 2) * GDN_DV] = (qs + jnp.dot(
            qkg_ref[slot, p], _lane_blockdiag(va.astype(BF16), vb.astype(BF16)), preferred_element_type=F32)
        ).astype(BF16)
        for h, vh in ((2 * p, va), (2 * p + 1, vb)):
            decay = jnp.exp(st[GDN_HEADS + h:GDN_HEADS + h + 1, CHUNK - 1:CHUNK])
            s_ref[h] = s_ref[h] * decay + _dot_tn(k_ref[rows, hcols(h)], vh * _col(ed, GDN_HEADS + h))
    yield


def _interleave(order, streams):
    for ch in order:
        next(streams[ch])


def _gdn_scan_kernel(q_ref, k_ref, v_ref, slab_ref, slabt_ref, o_ref, sout_ref, s_ref, tinv_ref, qkg_ref, *, tl):
    l = pl.program_id(1)
    nchunk = tl // CHUNK

    @pl.when(l == 0)
    def _():
        s_ref[...] = jnp.zeros(s_ref.shape, F32)

    ri = lax.broadcasted_iota(jnp.int32, (CHUNK, 2 * CHUNK), 0)
    li = lax.broadcasted_iota(jnp.int32, (CHUNK, 2 * CHUNK), 1)
    ci = li & (CHUNK - 1)
    masks = (ri >= ci, ri > ci, (ri == ci).astype(F32), li < CHUNK)

    prep = functools.partial(_gdn_prep_stages, k_ref=k_ref, q_ref=q_ref, slab_ref=slab_ref, slabt_ref=slabt_ref,
                             tinv_ref=tinv_ref, qkg_ref=qkg_ref, masks=masks)
    state = functools.partial(_gdn_state_stages, q_ref=q_ref, k_ref=k_ref, v_ref=v_ref, slab_ref=slab_ref,
                              slabt_ref=slabt_ref, tinv_ref=tinv_ref, qkg_ref=qkg_ref, o_ref=o_ref, s_ref=s_ref)

    _interleave("A" * 7, {"A": prep(0, 0)})

    def chunk_body(c, carry):
        slot = c & 1
        nxt = jnp.minimum(c + 1, nchunk - 1)
        _interleave("ABAABAABAA", {"A": prep(nxt, 1 - slot), "B": state(c, slot)})
        return carry

    lax.fori_loop(0, nchunk, chunk_body, 0)

    @pl.when(l == pl.num_programs(1) - 1)
    def _():
        sout_ref[...] = s_ref[...]


def _gdn_scan(q, k, v, slab, slabt, tl):
    bsz, seq, _ = q.shape
    row = lambda w: pl.BlockSpec((None, tl, w), lambda b, l: (b, l, 0))
    pair_scratch = pltpu.VMEM((2, GDN_HEADS // 2, CHUNK, 2 * CHUNK), BF16)
    return pl.pallas_call(
        functools.partial(_gdn_scan_kernel, tl=tl),
        grid=(bsz, seq // tl),
        in_specs=[row(GDN_QK), row(GDN_QK), row(GDN_VD), row(LANES),
                  pl.BlockSpec((None, tl // CHUNK, 2 * GDN_HEADS, CHUNK), lambda b, l: (b, l, 0, 0))],
        out_specs=(row(GDN_VD), pl.BlockSpec((None, GDN_HEADS, GDN_DK, GDN_DV), lambda b, l: (b, 0, 0, 0))),
        out_shape=(jax.ShapeDtypeStruct((bsz, seq, GDN_VD), BF16),
                   jax.ShapeDtypeStruct((bsz, GDN_HEADS, GDN_DK, GDN_DV), F32)),
        scratch_shapes=[pltpu.VMEM((GDN_HEADS, GDN_DK, GDN_DV), F32), pair_scratch, pair_scratch],
        compiler_params=_params(2),
        name="gdn_scan",
    )(q, k, v, slab, slabt)


def _gdn_out_kernel(o_ref, z_ref, x_ref, nw_ref, wout_ref, g_ref, b_ref, y_ref):
    o = o_ref[...].astype(F32)
    z = z_ref[...].astype(F32)
    parts = []
    for h in range(GDN_HEADS):
        hs = slice(h * GDN_DV, (h + 1) * GDN_DV)
        oh = o[:, hs]
        rn = oh * lax.rsqrt(jnp.mean(oh * oh, axis=-1, keepdims=True) + RMS_EPS) * nw_ref[...]
        parts.append((rn * _silu(z[:, hs])).astype(BF16))
    mix = jnp.dot(jnp.concatenate(parts, axis=1), wout_ref[...], preferred_element_type=F32)
    y_ref[...] = _layer_norm(DN_ALPHA * x_ref[...] + mix, g_ref[...], b_ref[...])


def _gdn_out(o, z, x, p, ln_g, ln_b, tl):
    bsz, seq, _ = x.shape
    row = lambda w: pl.BlockSpec((None, tl, w), lambda b, l: (b, l, 0))
    consts = (p["nw"], p["wout"], ln_g, ln_b)
    return pl.pallas_call(
        _gdn_out_kernel,
        grid=(bsz, seq // tl),
        in_specs=[row(GDN_VD), row(GDN_VD), row(D_MODEL)] + [_const_spec(c.shape) for c in consts],
        out_specs=row(D_MODEL),
        out_shape=jax.ShapeDtypeStruct((bsz, seq, D_MODEL), F32),
        compiler_params=_params(2),
        name="gdn_out",
    )(o, z, x, *consts)


def _ffn_kernel(x_ref, wup_ref, cw_ref, cb_ref, wdown_ref, g_ref, b_ref, y_ref, cache_ref, gg_ref, *, tl):
    l = pl.program_id(1)

    @pl.when(l == 0)
    def _():
        gg_ref[0:SUBLANES, :] = jnp.zeros((SUBLANES, D_FF), F32)

    x = x_ref[...]
    xb = x.astype(BF16)
    acc = jnp.zeros((tl, D_MODEL), F32)
    for j in range(D_FF // FFN_BLOCK):
        cs = slice(j * FFN_BLOCK, (j + 1) * FFN_BLOCK)
        vs = slice(D_FF + j * FFN_BLOCK, D_FF + (j + 1) * FFN_BLOCK)
        gg_ref[SUBLANES:SUBLANES + tl, cs] = jnp.dot(xb, wup_ref[:, cs], preferred_element_type=F32)
        val = jnp.dot(xb, wup_ref[:, vs], preferred_element_type=F32)
        hid = _silu(_causal_conv(gg_ref, cw_ref, cb_ref, cs, tl, FFN_CONV_W)) * val
        acc = acc + jnp.dot(hid.astype(BF16), wdown_ref[cs, :], preferred_element_type=F32)
    cache_ref[...] = gg_ref[SUBLANES + tl - (FFN_CONV_W - 1):SUBLANES + tl, :]
    gg_ref[0:SUBLANES, :] = gg_ref[tl:tl + SUBLANES, :]
    y_ref[...] = _layer_norm(DN_ALPHA * x + acc, g_ref[...], b_ref[...])


def _ffn(x, p, ln_g, ln_b, tl):
    bsz, seq, _ = x.shape
    row = lambda w: pl.BlockSpec((None, tl, w), lambda b, l: (b, l, 0))
    consts = (p["wup"], p["cw"], p["cb"], p["wdown"], ln_g, ln_b)
    return pl.pallas_call(
        functools.partial(_ffn_kernel, tl=tl),
        grid=(bsz, seq // tl),
        in_specs=[row(D_MODEL)] + [_const_spec(c.shape) for c in consts],
        out_specs=(row(D_MODEL), pl.BlockSpec((None, FFN_CONV_W - 1, D_FF), lambda b, l: (b, 0, 0))),
        out_shape=(jax.ShapeDtypeStruct((bsz, seq, D_MODEL), F32),
                   jax.ShapeDtypeStruct((bsz, FFN_CONV_W - 1, D_FF), F32)),
        scratch_shapes=[pltpu.VMEM((tl + SUBLANES, D_FF), F32)],
        compiler_params=_params(2),
        name="ffn",
    )(x, *consts)


def _ssd_in_kernel(x_ref, wz_ref, wxbc_ref, wdt_ref, cw_ref, cb_ref, a_ref, dtb_ref, triu_ref,
                   z_ref, xs_ref, bm_ref, cm_ref, slab_ref, slabt_ref, cache_ref, q3_ref, carry_ref, a3_ref, *, tl):
    l = pl.program_id(1)
    xb = x_ref[...].astype(BF16)
    lane = lax.broadcasted_iota(jnp.int32, (tl, LANES), 1)

    dt = _softplus(jnp.dot(xb, wdt_ref[...], preferred_element_type=F32) + dtb_ref[...])
    dt = jnp.where(lane < 2 * SSD_HEADS, dt, 0.0)
    _gate_slabs(jnp.where(lane < SSD_HEADS, dt, dt * a_ref[...]), SSD_HEADS, triu_ref[...], slab_ref, slabt_ref, tl)

    def z_proj():
        z_ref[...] = jnp.dot(xb, wz_ref[...], preferred_element_type=F32).astype(BF16)

    def emit(blk, rows, piece):
        col = blk * LANES
        if col < SSD_INNER:
            xs_ref[rows, col:col + LANES] = piece
        elif col < SSD_INNER + SSD_GN:
            bm_ref[rows, col - SSD_INNER:col - SSD_INNER + LANES] = piece.astype(BF16)
        else:
            cm_ref[rows, col - SSD_INNER - SSD_GN:col - SSD_INNER - SSD_GN + LANES] = piece.astype(BF16)

    _permuted_conv(xb, wxbc_ref, cw_ref, cb_ref, cache_ref, q3_ref, carry_ref, a3_ref, l == 0, tl, emit, z_proj)


def _ssd_in(x, p, tl):
    bsz, seq, _ = x.shape
    triu = _chunk_cumsum_matrix(tl)
    row = lambda w: pl.BlockSpec((None, tl, w), lambda b, l: (b, l, 0))
    out_shape = (jax.ShapeDtypeStruct((bsz, seq, SSD_INNER), BF16), jax.ShapeDtypeStruct((bsz, seq, SSD_INNER), F32),
                 jax.ShapeDtypeStruct((bsz, seq, SSD_GN), BF16), jax.ShapeDtypeStruct((bsz, seq, SSD_GN), BF16),
                 jax.ShapeDtypeStruct((bsz, seq, LANES), F32),
                 jax.ShapeDtypeStruct((bsz, seq // CHUNK, 2 * SSD_HEADS, CHUNK), F32),
                 jax.ShapeDtypeStruct((bsz, CONV_W - 1, SSD_CONV_DIM), F32))
    out_specs = (row(SSD_INNER), row(SSD_INNER), row(SSD_GN), row(SSD_GN), row(LANES),
                 pl.BlockSpec((None, tl // CHUNK, 2 * SSD_HEADS, CHUNK), lambda b, l: (b, l, 0, 0)),
                 pl.BlockSpec((None, CONV_W - 1, SSD_CONV_DIM), lambda b, l: (b, 0, 0)))
    consts = (p["wz"], p["wxbc"], p["wdt"], p["cw"], p["cb"], p["a"], p["dtb"], triu)
    return pl.pallas_call(
        functools.partial(_ssd_in_kernel, tl=tl),
        grid=(bsz, seq // tl),
        in_specs=[row(D_MODEL)] + [_const_spec(c.shape) for c in consts],
        out_specs=out_specs,
        out_shape=out_shape,
        scratch_shapes=_conv_scratch(SSD_CONV_DIM, tl),
        compiler_params=_params(2),
        name="ssd_in",
    )(x, *consts)


def _ssd_scan_kernel(xs_ref, bm_ref, cm_ref, slab_ref, slabt_ref, edt_ref, eac_ref, dskip_ref,
                     y_ref, sout_ref, st_ref, dtx_ref, acx_ref, *, tl):
    l = pl.program_id(1)

    @pl.when(l == 0)
    def _():
        st_ref[...] = jnp.zeros(st_ref.shape, F32)

    hi, mid, lo = _split3(slab_ref[...])
    dtx_ref[...] = jnp.dot(jnp.concatenate([hi, mid], axis=1), edt_ref[...], preferred_element_type=F32)
    acx_ref[...] = jnp.dot(jnp.concatenate([hi, mid, lo], axis=1), eac_ref[...], preferred_element_type=F32)

    ri = lax.broadcasted_iota(jnp.int32, (CHUNK, 2 * CHUNK), 0)
    li = lax.broadcasted_iota(jnp.int32, (CHUNK, 2 * CHUNK), 1)
    incl = ri >= (li & (CHUNK - 1))
    left = li < CHUNK

    def chunk_body(c, carry):
        rows = pl.ds(pl.multiple_of(c * CHUNK, CHUNK), CHUNK)
        sl = slab_ref[rows, :]
        st = slabt_ref[c]
        ac_x = acx_ref[rows, :]
        xs = xs_ref[rows, :]
        xdt = xs * dtx_ref[rows, :]
        alast_x = ac_x[CHUNK - 1:CHUNK, :]
        xdec = (xdt * jnp.exp(alast_x - ac_x)).astype(BF16)
        xdt_b = xdt.astype(BF16)
        ea = jnp.exp(ac_x)
        sdecay = jnp.exp(alast_x)
        cb2, y_off = [], []
        for g in range(SSD_GROUPS):
            gs = slice(g * SSD_GROUP_W, (g + 1) * SSD_GROUP_W)
            ns = slice(g * SSD_STATE, (g + 1) * SSD_STATE)
            cg = cm_ref[rows, ns]
            bg = bm_ref[rows, ns]
            cb2.append(_dot_nt(cg, jnp.concatenate([bg, bg], axis=0)))
            st_g = st_ref[:, gs]
            y_off.append(jnp.dot(cg, st_g.astype(BF16), preferred_element_type=F32))
            st_ref[:, gs] = st_g * sdecay[:, gs] + _dot_tn(bg, xdec[:, gs])
        for g in range(SSD_GROUPS):
            gs = slice(g * SSD_GROUP_W, (g + 1) * SSD_GROUP_W)
            diag = []
            for pp in range(SSD_HPG // 2):
                ha = g * SSD_HPG + 2 * pp
                hb = ha + 1
                acol = jnp.where(left, _col(sl, SSD_HEADS + ha), _col(sl, SSD_HEADS + hb))
                arow = jnp.concatenate([st[SSD_HEADS + ha:SSD_HEADS + ha + 1, :],
                                        st[SSD_HEADS + hb:SSD_HEADS + hb + 1, :]], axis=1)
                seg = jnp.exp(jnp.where(incl, acol - arow, NEG_BIG))
                w = _pair_blockdiag(xdt_b[:, ha * SSD_HEADDIM:(ha + 2) * SSD_HEADDIM], left)
                diag.append(jnp.dot((seg * cb2[g]).astype(BF16), w, preferred_element_type=F32))
            y_ref[rows, gs] = (jnp.concatenate(diag, axis=1) + y_off[g] * ea[:, gs]
                               + dskip_ref[:, gs] * xs[:, gs]).astype(BF16)
        return carry

    lax.fori_loop(0, tl // CHUNK, chunk_body, 0)

    @pl.when(l == pl.num_programs(1) - 1)
    def _():
        sout_ref[...] = st_ref[...].T.reshape(SSD_HEADS, SSD_HEADDIM, SSD_STATE)


def _ssd_scan(xs, bm, cm, slab, slabt, p, tl):
    bsz, seq, _ = xs.shape
    row = lambda w: pl.BlockSpec((None, tl, w), lambda b, l: (b, l, 0))
    consts = (jnp.tile(p["edt"], (2, 1)), jnp.tile(p["eac"], (3, 1)), p["dskip"])
    return pl.pallas_call(
        functools.partial(_ssd_scan_kernel, tl=tl),
        grid=(bsz, seq // tl),
        in_specs=[row(SSD_INNER), row(SSD_GN), row(SSD_GN), row(LANES),
                  pl.BlockSpec((None, tl // CHUNK, 2 * SSD_HEADS, CHUNK), lambda b, l: (b, l, 0, 0))]
                 + [_const_spec(c.shape) for c in consts],
        out_specs=(row(SSD_INNER),
                   pl.BlockSpec((None, SSD_HEADS, SSD_HEADDIM, SSD_STATE), lambda b, l: (b, 0, 0, 0))),
        out_shape=(jax.ShapeDtypeStruct((bsz, seq, SSD_INNER), BF16),
                   jax.ShapeDtypeStruct((bsz, SSD_HEADS, SSD_HEADDIM, SSD_STATE), F32)),
        scratch_shapes=[pltpu.VMEM((SSD_STATE, SSD_INNER), F32), pltpu.VMEM((tl, SSD_INNER), F32),
                        pltpu.VMEM((tl, SSD_INNER), F32)],
        compiler_params=_params(2),
        name="ssd_scan",
    )(xs, bm, cm, slab, slabt, *consts)


def _ssd_out_kernel(y_ref, z_ref, x_ref, nw_ref, wout_ref, g_ref, b_ref, o_ref):
    t = y_ref[...].astype(F32) * _silu(z_ref[...].astype(F32))
    parts = []
    for g in range(SSD_GROUPS):
        gs = slice(g * SSD_GROUP_W, (g + 1) * SSD_GROUP_W)
        tg = t[:, gs]
        parts.append((tg * lax.rsqrt(jnp.mean(tg * tg, axis=-1, keepdims=True) + RMS_EPS) * nw_ref[:, gs]).astype(BF16))
    mix = jnp.dot(jnp.concatenate(parts, axis=1), wout_ref[...], preferred_element_type=F32)
    o_ref[...] = _layer_norm(DN_ALPHA * x_ref[...] + mix, g_ref[...], b_ref[...])


def _ssd_out(y, z, x, p, ln_g, ln_b, tl):
    bsz, seq, _ = x.shape
    row = lambda w: pl.BlockSpec((None, tl, w), lambda b, l: (b, l, 0))
    consts = (p["nw"], p["wout"], ln_g, ln_b)
    return pl.pallas_call(
        _ssd_out_kernel,
        grid=(bsz, seq // tl),
        in_specs=[row(SSD_INNER), row(SSD_INNER), row(D_MODEL)] + [_const_spec(c.shape) for c in consts],
        out_specs=row(D_MODEL),
        out_shape=jax.ShapeDtypeStruct((bsz, seq, D_MODEL), F32),
        compiler_params=_params(2),
        name="ssd_out",
    )(y, z, x, *consts)


def _step_conv(pre, c_refs, cw_ref, cb_ref):
    width = len(c_refs) + 1
    acc = cb_ref[...] + c_refs[0][...] * cw_ref[0:1, :]
    for k in range(1, width - 1):
        acc = acc + c_refs[k][...] * cw_ref[k:k + 1, :]
    return acc + pre * cw_ref[width - 1:width, :]


def _gdn_in_step_kernel(x_ref, c0_ref, c1_ref, c2_ref, wqkv_ref, wz_ref, wba_ref, cw_ref, cb_ref, alog_ref, dtb_ref,
                        q_ref, k_ref, v_ref, z_ref, slab_ref, pre_ref):
    xb = x_ref[...].astype(BF16)
    n = xb.shape[0]
    lane = lax.broadcasted_iota(jnp.int32, (n, LANES), 1)
    ba = jnp.dot(xb, wba_ref[...], preferred_element_type=F32)
    g = -jnp.exp(alog_ref[...]) * _softplus(ba + dtb_ref[...])
    slab_ref[...] = jnp.where(lane < GDN_HEADS, jax.nn.sigmoid(ba), g)
    z_ref[...] = jnp.dot(xb, wz_ref[...], preferred_element_type=F32)
    pre = jnp.dot(xb, wqkv_ref[...], preferred_element_type=F32)
    pre_ref[...] = pre
    act = _silu(_step_conv(pre, (c0_ref, c1_ref, c2_ref), cw_ref, cb_ref))
    for h in range(GDN_HEADS):
        hs = slice(h * GDN_DK, (h + 1) * GDN_DK)
        q_ref[:, hs] = _l2norm(act[:, hs]) * (GDN_DK ** -0.5)
        k_ref[:, hs] = _l2norm(act[:, GDN_QK + h * GDN_DK:GDN_QK + (h + 1) * GDN_DK])
    v_ref[...] = act[:, 2 * GDN_QK:]


def _gdn_in_step(x, cache, p):
    n = x.shape[0]
    ff = lambda w: jax.ShapeDtypeStruct((n, w), F32)
    return pl.pallas_call(
        _gdn_in_step_kernel,
        out_shape=(ff(GDN_QK), ff(GDN_QK), ff(GDN_VD), ff(GDN_VD), ff(LANES), ff(GDN_CONV_DIM)),
        compiler_params=pltpu.CompilerParams(vmem_limit_bytes=VMEM_LIMIT),
        name="gdn_in_step",
    )(x, cache[:, 0], cache[:, 1], cache[:, 2], p["wqkv"], p["wz"], p["wba"], p["cw"], p["cb"], p["alog"], p["dtb"])


def _gdn_step_kernel(q_ref, k_ref, v_ref, slab_ref, s_ref, o_ref, sout_ref, *, tb):
    zpad = jnp.zeros((LANES - GDN_HEADS, GDN_DK), F32)

    def token_body(t, carry):
        kt = jnp.concatenate([k_ref[t], zpad], axis=0).T
        qt = jnp.concatenate([q_ref[t], zpad], axis=0).T
        v = v_ref[t]
        sl = slab_ref[pl.ds(t, 1), :]
        alpha = jnp.exp(sl)
        for h in range(GDN_HEADS):
            s = s_ref[t, h]
            kc = _col(kt, h)
            a_h = alpha[:, GDN_HEADS + h:GDN_HEADS + h + 1]
            ks = jnp.sum(s * kc, axis=0, keepdims=True)
            v_new = sl[:, h:h + 1] * (v[h:h + 1, :] - a_h * ks)
            s_new = s * a_h + kc * v_new
            sout_ref[t, h] = s_new
            o_ref[t, h:h + 1, :] = jnp.sum(s_new * _col(qt, h), axis=0, keepdims=True)
        return carry

    lax.fori_loop(0, tb, token_body, 0)


def _gdn_step(q, k, v, slab, state, tb):
    n = q.shape[0]
    q3, k3, v3 = (t.reshape(n, GDN_HEADS, GDN_DK) for t in (q, k, v))
    tok = pl.BlockSpec((tb, GDN_HEADS, GDN_DK), lambda i: (i, 0, 0))
    st = pl.BlockSpec((tb, GDN_HEADS, GDN_DK, GDN_DV), lambda i: (i, 0, 0, 0))
    o, s_new = pl.pallas_call(
        functools.partial(_gdn_step_kernel, tb=tb),
        grid=(n // tb,),
        in_specs=[tok, tok, tok, pl.BlockSpec((tb, LANES), lambda i: (i, 0)), st],
        out_specs=(tok, st),
        out_shape=(jax.ShapeDtypeStruct((n, GDN_HEADS, GDN_DV), F32), jax.ShapeDtypeStruct(state.shape, F32)),
        compiler_params=_params(1),
        name="gdn_step",
    )(q3, k3, v3, slab, state)
    return o.reshape(n, GDN_VD), s_new


def _ffn_step_kernel(x_ref, c0_ref, c1_ref, wup_ref, cw_ref, cb_ref, wdown_ref, g_ref, b_ref, y_ref, pre_ref):
    x = x_ref[...]
    xb = x.astype(BF16)
    gate = jnp.dot(xb, wup_ref[:, 0:D_FF], preferred_element_type=F32)
    val = jnp.dot(xb, wup_ref[:, D_FF:2 * D_FF], preferred_element_type=F32)
    pre_ref[...] = gate
    hid = _silu(_step_conv(gate, (c0_ref, c1_ref), cw_ref, cb_ref)) * val
    y_ref[...] = _layer_norm(DN_ALPHA * x + jnp.dot(hid.astype(BF16), wdown_ref[...], preferred_element_type=F32),
                             g_ref[...], b_ref[...])


def _ffn_step(x, cache, p, ln_g, ln_b):
    n = x.shape[0]
    return pl.pallas_call(
        _ffn_step_kernel,
        out_shape=(jax.ShapeDtypeStruct((n, D_MODEL), F32), jax.ShapeDtypeStruct((n, D_FF), F32)),
        compiler_params=pltpu.CompilerParams(vmem_limit_bytes=VMEM_LIMIT),
        name="ffn_step",
    )(x, cache[:, 0], cache[:, 1], p["wup"], p["cw"], p["cb"], p["wdown"], ln_g, ln_b)


def _ssd_in_step_kernel(x_ref, c0_ref, c1_ref, c2_ref, wz_ref, wxbc_ref, wdt_ref, cw_ref, cb_ref, a_ref, dtb_ref,
                        edt_ref, eac_ref, z_ref, xs_ref, bm_ref, cm_ref, xdt_ref, dec_ref, pre_ref):
    xb = x_ref[...].astype(BF16)
    n = xb.shape[0]
    lane = lax.broadcasted_iota(jnp.int32, (n, LANES), 1)
    dt = _softplus(jnp.dot(xb, wdt_ref[...], preferred_element_type=F32) + dtb_ref[...])
    dt = jnp.where(lane < 2 * SSD_HEADS, dt, 0.0)
    slab = jnp.where(lane < SSD_HEADS, dt, dt * a_ref[...])
    z_ref[...] = jnp.dot(xb, wz_ref[...], preferred_element_type=F32)
    pre = jnp.dot(xb, wxbc_ref[...], preferred_element_type=F32)
    pre_ref[...] = pre
    act = _silu(_step_conv(pre, (c0_ref, c1_ref, c2_ref), cw_ref, cb_ref))
    xs = act[:, 0:SSD_INNER]
    xs_ref[...] = xs
    bm_ref[...] = act[:, SSD_INNER:SSD_INNER + SSD_GN]
    cm_ref[...] = act[:, SSD_INNER + SSD_GN:]
    xdt_ref[...] = xs * _dot_exact_rhs(slab, edt_ref[...])
    dec_ref[...] = jnp.exp(_dot_exact_rhs(slab, eac_ref[...]))


def _ssd_in_step(x, cache, p):
    n = x.shape[0]
    ff = lambda w: jax.ShapeDtypeStruct((n, w), F32)
    return pl.pallas_call(
        _ssd_in_step_kernel,
        out_shape=(ff(SSD_INNER), ff(SSD_INNER), ff(SSD_GN), ff(SSD_GN), ff(SSD_INNER), ff(SSD_INNER),
                   ff(SSD_CONV_DIM)),
        compiler_params=pltpu.CompilerParams(vmem_limit_bytes=VMEM_LIMIT),
        name="ssd_in_step",
    )(x, cache[:, 0], cache[:, 1], cache[:, 2], p["wz"], p["wxbc"], p["wdt"], p["cw"], p["cb"], p["a"], p["dtb"],
      p["edt"], p["eac"])


def _ssd_step_kernel(xs_ref, bm_ref, cm_ref, xdt_ref, dec_ref, dskip_ref, s_ref, y_ref, sout_ref, *, tb):
    hp = SSD_INNER
    rid = lax.broadcasted_iota(jnp.int32, (SUBLANES, hp), 0)
    gid = lax.broadcasted_iota(jnp.int32, (SUBLANES, hp), 1) // SSD_GROUP_W
    zpad_r = jnp.zeros((LANES - SUBLANES, hp), F32)

    def token_body(t, carry):
        row = pl.ds(t, 1)
        xdt = xdt_ref[row, :]
        dec = dec_ref[row, :]
        stack = jnp.where(rid == gid, jnp.broadcast_to(xdt, (SUBLANES, hp)), 0.0)
        cols = jnp.concatenate([stack, zpad_r], axis=0).T
        s = s_ref[t].reshape(hp, SSD_STATE)
        bm = bm_ref[row, :]
        bmat = jnp.concatenate([bm[:, g * SSD_STATE:(g + 1) * SSD_STATE] for g in range(SSD_GROUPS)]
                               + [jnp.zeros((LANES - SSD_GROUPS, SSD_STATE), F32)], axis=0)
        upd = jnp.dot(cols.astype(BF16), bmat.astype(BF16), preferred_element_type=F32)
        s_new = jnp.concatenate(
            [s[h * SSD_HEADDIM:(h + 1) * SSD_HEADDIM, :] * dec[:, h * SSD_HEADDIM:h * SSD_HEADDIM + 1]
             for h in range(SSD_HEADS)], axis=0) + upd
        sout_ref[t] = s_new.reshape(SSD_HEADS, SSD_HEADDIM, SSD_STATE)
        cm = cm_ref[row, :]
        cmat = jnp.concatenate([cm[:, g * SSD_STATE:(g + 1) * SSD_STATE] for g in range(SSD_GROUPS)]
                               + [jnp.zeros((SUBLANES - SSD_GROUPS, SSD_STATE), F32)], axis=0)
        yall = _dot_nt(cmat, s_new)
        y = jnp.sum(jnp.where(rid == gid, yall, 0.0), axis=0, keepdims=True)
        y_ref[row, :] = y + dskip_ref[...] * xs_ref[row, :]
        return carry

    lax.fori_loop(0, tb, token_body, 0)


def _ssd_step(xs, bm, cm, xdt, dec, state, p, tb):
    n = xs.shape[0]
    tok = lambda w: pl.BlockSpec((tb, w), lambda i: (i, 0))
    st = pl.BlockSpec((tb, SSD_HEADS, SSD_HEADDIM, SSD_STATE), lambda i: (i, 0, 0, 0))
    return pl.pallas_call(
        functools.partial(_ssd_step_kernel, tb=tb),
        grid=(n // tb,),
        in_specs=[tok(SSD_INNER), tok(SSD_GN), tok(SSD_GN), tok(SSD_INNER), tok(SSD_INNER),
                  _const_spec(p["dskip"].shape), st],
        out_specs=(tok(SSD_INNER), st),
        out_shape=(jax.ShapeDtypeStruct((n, SSD_INNER), F32), jax.ShapeDtypeStruct(state.shape, F32)),
        compiler_params=_params(1),
        name="ssd_step",
    )(xs, bm, cm, xdt, dec, p["dskip"], state)


def _row(v, width=None, offset=0):
    v = v.astype(F32).reshape(1, -1)
    if width is None:
        return v
    return jnp.pad(v, ((0, 0), (offset, width - offset - v.shape[1])))


def _prep_gdn(w_in, conv_w, conv_b, a_log, dt_bias, norm_w, w_out):
    ba = w_in[:, GDN_CONV_DIM + GDN_VD:]
    return {
        "wqkv": w_in[:, :GDN_CONV_DIM].astype(BF16),
        "wz": w_in[:, GDN_CONV_DIM:GDN_CONV_DIM + GDN_VD].astype(BF16),
        "wba": jnp.pad(ba, ((0, 0), (0, LANES - ba.shape[1]))).astype(BF16),
        "cw": conv_w.astype(F32),
        "cb": _row(conv_b),
        "alog": _row(a_log, LANES, GDN_HEADS),
        "dtb": _row(dt_bias, LANES, GDN_HEADS),
        "nw": _row(norm_w),
        "wout": w_out.astype(BF16),
    }


def _prep_ssd(w_in, conv_w, conv_b, a_log, dt_bias, d_skip, norm_w, w_out):
    wdt = w_in[:, SSD_INNER + SSD_CONV_DIM:]
    wdt2 = jnp.concatenate([wdt, wdt], axis=1)
    head_of_lane = jnp.arange(SSD_INNER) // SSD_HEADDIM
    sel = jnp.arange(LANES)[:, None]
    return {
        "wz": w_in[:, :SSD_INNER].astype(BF16),
        "wxbc": w_in[:, SSD_INNER:SSD_INNER + SSD_CONV_DIM].astype(BF16),
        "wdt": jnp.pad(wdt2, ((0, 0), (0, LANES - 2 * SSD_HEADS))).astype(BF16),
        "cw": conv_w.astype(F32),
        "cb": _row(conv_b),
        "a": _row(-jnp.exp(a_log.astype(F32)), LANES, SSD_HEADS),
        "dtb": _row(jnp.concatenate([dt_bias, dt_bias]), LANES, 0),
        "edt": (sel == head_of_lane[None, :]).astype(BF16),
        "eac": (sel == head_of_lane[None, :] + SSD_HEADS).astype(BF16),
        "dskip": _row(jnp.repeat(d_skip, SSD_HEADDIM)),
        "nw": _row(norm_w),
        "wout": w_out.astype(BF16),
    }


def _prep_ffn(w_up, conv_w, conv_b, w_down):
    return {"wup": w_up.astype(BF16), "cw": conv_w.astype(F32), "cb": _row(conv_b), "wdown": w_down.astype(BF16)}


def _prompt_trunk(x, gdn, ssd, ffn, ln, tl):
    q, k, v, z, slab, slabt, gdn_cache = _gdn_in(x, gdn, tl)
    o, gdn_state = _gdn_scan(q, k, v, slab, slabt, min(GDN_SCAN_TILE, x.shape[1]))
    x = _gdn_out(o, z, x, gdn, ln[0][0], ln[0][1], tl)
    x, ffn_cache0 = _ffn(x, ffn[0], ln[0][2], ln[0][3], tl)
    z, xs, bm, cm, slab, slabt, ssd_cache = _ssd_in(x, ssd, tl)
    y, ssd_state = _ssd_scan(xs, bm, cm, slab, slabt, ssd, tl)
    x = _ssd_out(y, z, x, ssd, ln[1][0], ln[1][1], tl)
    x, ffn_cache1 = _ffn(x, ffn[1], ln[1][2], ln[1][3], tl)
    return x, gdn_cache, gdn_state, ssd_cache, ssd_state, jnp.stack([ffn_cache0, ffn_cache1])


def _sample_trunk(x, gdn_cache, gdn_state, ssd_cache, ssd_state, ffn_cache, gdn, ssd, ffn, ln, tb):
    n = x.shape[0]
    q, k, v, z, slab, pre = _gdn_in_step(x, gdn_cache, gdn)
    gdn_cache_new = jnp.concatenate([gdn_cache[:, 1:], pre[:, None]], axis=1)
    o, gdn_state_new = _gdn_step(q, k, v, slab, gdn_state, tb)
    x = _gdn_out(o[None], z[None], x[None], gdn, ln[0][0], ln[0][1], n)[0]
    x, pre = _ffn_step(x, ffn_cache[0], ffn[0], ln[0][2], ln[0][3])
    ffn_cache0 = jnp.concatenate([ffn_cache[0][:, 1:], pre[:, None]], axis=1)
    z, xs, bm, cm, xdt, dec, pre = _ssd_in_step(x, ssd_cache, ssd)
    ssd_cache_new = jnp.concatenate([ssd_cache[:, 1:], pre[:, None]], axis=1)
    y, ssd_state_new = _ssd_step(xs, bm, cm, xdt, dec, ssd_state, ssd, tb)
    x = _ssd_out(y[None], z[None], x[None], ssd, ln[1][0], ln[1][1], n)[0]
    x, pre = _ffn_step(x, ffn_cache[1], ffn[1], ln[1][2], ln[1][3])
    ffn_cache1 = jnp.concatenate([ffn_cache[1][:, 1:], pre[:, None]], axis=1)
    return x, gdn_cache_new, gdn_state_new, ssd_cache_new, ssd_state_new, jnp.stack([ffn_cache0, ffn_cache1])


PROMPT_TILE = 512
GDN_SCAN_TILE = 1024
SAMPLE_TOKENS = 8


def kernel(x_prompt, x_sample, cache_gdn_conv, state_gdn, cache_ssd_conv, state_ssd, cache_ffn_conv, gdn_w_in, gdn_conv_w, gdn_conv_b, gdn_a_log, gdn_dt_bias, gdn_norm_w, gdn_w_out, ssd_w_in, ssd_conv_w, ssd_conv_b, ssd_a_log, ssd_dt_bias, ssd_d, ssd_norm_w, ssd_w_out, ffn_w_up, ffn_conv_w, ffn_conv_b, ffn_w_down, ln1_g, ln1_b, ln2_g, ln2_b):
    gdn = _prep_gdn(gdn_w_in[0], gdn_conv_w[0], gdn_conv_b[0], gdn_a_log[0], gdn_dt_bias[0], gdn_norm_w[0],
                    gdn_w_out[0])
    ssd = _prep_ssd(ssd_w_in[0], ssd_conv_w[0], ssd_conv_b[0], ssd_a_log[0], ssd_dt_bias[0], ssd_d[0],
                    ssd_norm_w[0], ssd_w_out[0])
    ffn = [_prep_ffn(ffn_w_up[i], ffn_conv_w[i], ffn_conv_b[i], ffn_w_down[i]) for i in range(DEPTH)]
    ln = [(_row(ln1_g[i]), _row(ln1_b[i]), _row(ln2_g[i]), _row(ln2_b[i])) for i in range(DEPTH)]

    tl = min(PROMPT_TILE, x_prompt.shape[1])
    y_p, gcp, gsp, scp, ssp, fcp = _prompt_trunk(x_prompt, gdn, ssd, ffn, ln, tl)
    y_s, gcs, gss, scs, sss, fcs = _sample_trunk(
        x_sample[:, 0], cache_gdn_conv[0], state_gdn[0], cache_ssd_conv[0], state_ssd[0], cache_ffn_conv,
        gdn, ssd, ffn, ln, min(SAMPLE_TOKENS, x_sample.shape[0]))
    return (y_p, y_s[:, None], gcp[None], gcs[None], gsp[None], gss[None], scp[None], scs[None],
            ssp[None], sss[None], fcp, fcs)
```

```python
import functools

import jax
import jax.numpy as jnp
from jax import lax
from jax.experimental import pallas as pl
from jax.experimental.pallas import tpu as pltpu

F32 = jnp.float32
BF16 = jnp.bfloat16

D_MODEL = 1024
DEPTH = 2
CONV_W = 4
CHUNK = 64

GDN_HEADS = 8
GDN_DK = 128
GDN_DV = 128
GDN_QK = GDN_HEADS * GDN_DK
GDN_VD = GDN_HEADS * GDN_DV
GDN_CONV_DIM = 2 * GDN_QK + GDN_VD

SSD_INNER = 2 * D_MODEL
SSD_HEADDIM = 64
SSD_HEADS = SSD_INNER // SSD_HEADDIM
SSD_GROUPS = 4
SSD_STATE = 128
SSD_HPG = SSD_HEADS // SSD_GROUPS
SSD_GN = SSD_GROUPS * SSD_STATE
SSD_CONV_DIM = SSD_INNER + 2 * SSD_GN
SSD_GROUP_W = SSD_INNER // SSD_GROUPS

D_FF = 2816
FFN_CONV_W = 3
FFN_BLOCK = D_FF

DN_ALPHA = (2 * DEPTH) ** 0.25
LN_EPS = 1e-5
RMS_EPS = 1e-6
L2_EPS = 1e-6

LANES = 128
SUBLANES = 8
VMEM_LIMIT = 56 * 1024 * 1024

NEG_BIG = -1e30


def _dot_nt(a, b):
    return lax.dot_general(a.astype(BF16), b.astype(BF16), (((1,), (1,)), ((), ())),
                           preferred_element_type=F32)


def _dot_tn(a, b):
    return lax.dot_general(a.astype(BF16), b.astype(BF16), (((0,), (0,)), ((), ())),
                           preferred_element_type=F32)


def _split3(x):
    hi = x.astype(BF16)
    r1 = x - hi.astype(F32)
    mid = r1.astype(BF16)
    lo = (r1 - mid.astype(F32)).astype(BF16)
    return hi, mid, lo


def _dot_exact_rhs(x, e):
    hi, mid, lo = _split3(x)
    return (jnp.dot(hi, e, preferred_element_type=F32) + jnp.dot(mid, e, preferred_element_type=F32)
            + jnp.dot(lo, e, preferred_element_type=F32))


def _silu(x):
    return x * jax.nn.sigmoid(x)


def _softplus(x):
    return jnp.maximum(x, 0.0) + jnp.log1p(jnp.exp(-jnp.abs(x)))


def _layer_norm(x, g, b):
    mu = jnp.mean(x, axis=-1, keepdims=True)
    xc = x - mu
    var = jnp.mean(xc * xc, axis=-1, keepdims=True)
    return xc * lax.rsqrt(var + LN_EPS) * g + b


def _l2norm(t):
    return t * lax.rsqrt(jnp.sum(t * t, axis=-1, keepdims=True) + L2_EPS)


def _col(a, i):
    return a[:, i:i + 1]


def _const_spec(shape):
    nd = len(shape)
    return pl.BlockSpec(shape, lambda *_: (0,) * nd)


def _params(n_grid):
    return pltpu.CompilerParams(dimension_semantics=("arbitrary",) * n_grid, vmem_limit_bytes=VMEM_LIMIT)


def _chunk_cumsum_matrix(tl):
    r = jnp.arange(tl)
    same = (r[:, None] // CHUNK) == (r[None, :] // CHUNK)
    return (same & (r[:, None] <= r[None, :])).astype(BF16)


def _gate_slabs(pre, n_keep, triu, slab_ref, slabt_ref, tl):
    pre_t = pre.T[0:2 * n_keep, :]
    cum_t = _dot_exact_rhs(pre_t, triu)
    row = lax.broadcasted_iota(jnp.int32, pre_t.shape, 0)
    st = jnp.where(row < n_keep, pre_t, cum_t)
    for c in range(tl // CHUNK):
        slabt_ref[c] = st[:, c * CHUNK:(c + 1) * CHUNK]
    slab_ref[...] = jnp.concatenate([st, jnp.zeros((LANES - 2 * n_keep, tl), F32)], axis=0).T


def _causal_conv(xx_ref, cw_ref, cb_ref, cs, tl, width):
    base = SUBLANES - (width - 1)
    acc = cb_ref[:, cs] + xx_ref[base:base + tl, cs] * cw_ref[0:1, cs]
    for k in range(1, width):
        acc = acc + xx_ref[base + k:base + k + tl, cs] * cw_ref[k:k + 1, cs]
    return acc


CONV_HEAD = (CONV_W - 1) * SUBLANES
CONV_ROWS = 64


def _permuted_conv(xb, w_ref, cw_ref, cb_ref, cache_ref, q3_ref, carry_ref, a3_ref, first, tl, emit, extra):
    n = tl // SUBLANES
    per = D_MODEL // LANES
    ngroup = w_ref.shape[1] // D_MODEL
    sub0 = lax.broadcasted_iota(jnp.int32, (SUBLANES, LANES), 0) == 0

    @pl.when(first)
    def _():
        carry_ref[...] = jnp.zeros(carry_ref.shape, F32)

    def project(j):
        cs = slice(j * D_MODEL, (j + 1) * D_MODEL)
        pre = jnp.dot(xb, w_ref[:, cs], preferred_element_type=F32)
        cache_ref[:, cs] = pre[tl - (CONV_W - 1):tl, :]
        for c in range(per):
            blk = j * per + c
            for a in range(SUBLANES):
                q3_ref[blk, pl.ds(CONV_HEAD + a, n, stride=SUBLANES), :] = (
                    pre[a * n:(a + 1) * n, c * LANES:(c + 1) * LANES])
            for m in range(CONV_W - 1):
                src = CONV_HEAD + SUBLANES * (n - (CONV_W - 1) + m)
                dst = slice(m * SUBLANES, (m + 1) * SUBLANES)
                rolled = pltpu.roll(q3_ref[blk, src:src + SUBLANES, :], 1, 0)
                q3_ref[blk, dst, :] = jnp.where(sub0, carry_ref[blk, dst, :], rolled)
                carry_ref[blk, dst, :] = rolled

    def convolve(j):
        for c in range(per):
            blk = j * per + c
            cs = slice(blk * LANES, (blk + 1) * LANES)
            for r0 in range(0, tl, CONV_ROWS):
                acc = cb_ref[:, cs] + q3_ref[blk, r0:r0 + CONV_ROWS, :] * cw_ref[0:1, cs]
                for k in range(1, CONV_W):
                    acc = acc + q3_ref[blk, r0 + k * SUBLANES:r0 + k * SUBLANES + CONV_ROWS, :] * cw_ref[k:k + 1, cs]
                a3_ref[c, r0:r0 + CONV_ROWS, :] = _silu(acc)
            for a in range(SUBLANES):
                emit(blk, slice(a * n, (a + 1) * n), a3_ref[c, pl.ds(a, n, stride=SUBLANES), :])

    project(0)
    for j in range(ngroup):
        if j + 1 < ngroup:
            project(j + 1)
        else:
            extra()
        convolve(j)


def _conv_scratch(ncols, tl):
    nblk = ncols // LANES
    return [pltpu.VMEM((nblk, CONV_HEAD + tl, LANES), F32), pltpu.VMEM((nblk, CONV_HEAD, LANES), F32),
            pltpu.VMEM((D_MODEL // LANES, tl, LANES), F32)]


def _gdn_in_kernel(x_ref, wqkv_ref, wz_ref, wba_ref, cw_ref, cb_ref, alog_ref, dtb_ref, triu_ref,
                   q_ref, k_ref, v_ref, z_ref, slab_ref, slabt_ref, cache_ref, q3_ref, carry_ref, a3_ref, *, tl):
    l = pl.program_id(1)
    xb = x_ref[...].astype(BF16)
    lane = lax.broadcasted_iota(jnp.int32, (tl, LANES), 1)

    ba = jnp.dot(xb, wba_ref[...], preferred_element_type=F32)
    g = -jnp.exp(alog_ref[...]) * _softplus(ba + dtb_ref[...])
    _gate_slabs(jnp.where(lane < GDN_HEADS, jax.nn.sigmoid(ba), g), GDN_HEADS, triu_ref[...], slab_ref, slabt_ref, tl)

    def z_proj():
        z_ref[...] = jnp.dot(xb, wz_ref[...], preferred_element_type=F32).astype(BF16)

    def emit(blk, rows, piece):
        j, h = divmod(blk, GDN_HEADS)
        hs = slice(h * GDN_DK, (h + 1) * GDN_DK)
        if j == 0:
            q_ref[rows, hs] = (_l2norm(piece) * (GDN_DK ** -0.5)).astype(BF16)
        elif j == 1:
            k_ref[rows, hs] = _l2norm(piece).astype(BF16)
        else:
            v_ref[rows, hs] = piece

    _permuted_conv(xb, wqkv_ref, cw_ref, cb_ref, cache_ref, q3_ref, carry_ref, a3_ref, l == 0, tl, emit, z_proj)


def _gdn_in(x, p, tl):
    bsz, seq, _ = x.shape
    nl = seq // tl
    triu = _chunk_cumsum_matrix(tl)
    row = lambda w: pl.BlockSpec((None, tl, w), lambda b, l: (b, l, 0))
    bf = lambda w: jax.ShapeDtypeStruct((bsz, seq, w), BF16)
    ff = lambda w: jax.ShapeDtypeStruct((bsz, seq, w), F32)
    out_shape = (bf(GDN_QK), bf(GDN_QK), ff(GDN_VD), bf(GDN_VD), ff(LANES),
                 jax.ShapeDtypeStruct((bsz, seq // CHUNK, 2 * GDN_HEADS, CHUNK), F32),
                 jax.ShapeDtypeStruct((bsz, CONV_W - 1, GDN_CONV_DIM), F32))
    out_specs = (row(GDN_QK), row(GDN_QK), row(GDN_VD), row(GDN_VD), row(LANES),
                 pl.BlockSpec((None, tl // CHUNK, 2 * GDN_HEADS, CHUNK), lambda b, l: (b, l, 0, 0)),
                 pl.BlockSpec((None, CONV_W - 1, GDN_CONV_DIM), lambda b, l: (b, 0, 0)))
    consts = (p["wqkv"], p["wz"], p["wba"], p["cw"], p["cb"], p["alog"], p["dtb"], triu)
    return pl.pallas_call(
        functools.partial(_gdn_in_kernel, tl=tl),
        grid=(bsz, nl),
        in_specs=[row(D_MODEL)] + [_const_spec(c.shape) for c in consts],
        out_specs=out_specs,
        out_shape=out_shape,
        scratch_shapes=_conv_scratch(GDN_CONV_DIM, tl),
        compiler_params=_params(2),
        name="gdn_in",
    )(x, *consts)


def _split2(x):
    hi = x.astype(BF16)
    return hi, (x - hi.astype(F32)).astype(BF16)


def _pair_blockdiag(x, left):
    z = jnp.zeros_like(x)
    return jnp.concatenate([jnp.where(left, x, z), jnp.where(left, z, x)], axis=0)


def _lane_blockdiag(a, b):
    z = jnp.zeros_like(a)
    return jnp.concatenate([jnp.concatenate([a, z], axis=1), jnp.concatenate([z, b], axis=1)], axis=0)


def _pair_matmul(lhs_parts, w_parts, left):
    w_hi = _pair_blockdiag(w_parts[0], left)
    w = jnp.concatenate([w_hi, w_hi, _pair_blockdiag(w_parts[1], left)], axis=0)
    lhs = [jnp.concatenate([hi, lo, hi], axis=1) for hi, lo in lhs_parts]
    r = jnp.dot(jnp.concatenate(lhs, axis=0) if len(lhs) > 1 else lhs[0], w, preferred_element_type=F32)
    return [r[i * CHUNK:(i + 1) * CHUNK] for i in range(len(lhs))]


def _gdn_prep_stages(c, slot, k_ref, q_ref, slab_ref, slabt_ref, tinv_ref, qkg_ref, masks):
    incl, strict, eye2, left = masks
    npair = GDN_HEADS // 2
    rows = pl.ds(pl.multiple_of(c * CHUNK, CHUNK), CHUNK)
    sl = slab_ref[rows, :]
    st = slabt_ref[c]
    ns = []
    for p in range(npair):
        ha, hb = 2 * p, 2 * p + 1
        ka = k_ref[rows, ha * GDN_DK:(ha + 1) * GDN_DK]
        kb = k_ref[rows, hb * GDN_DK:(hb + 1) * GDN_DK]
        kk = jnp.concatenate([_dot_nt(ka, ka), _dot_nt(kb, kb)], axis=1)
        qk = jnp.concatenate([_dot_nt(q_ref[rows, ha * GDN_DK:(ha + 1) * GDN_DK], ka),
                              _dot_nt(q_ref[rows, hb * GDN_DK:(hb + 1) * GDN_DK], kb)], axis=1)
        gcol = jnp.where(left, _col(sl, GDN_HEADS + ha), _col(sl, GDN_HEADS + hb))
        bcol = jnp.where(left, _col(sl, ha), _col(sl, hb))
        grow = jnp.concatenate([st[GDN_HEADS + ha:GDN_HEADS + ha + 1, :], st[GDN_HEADS + hb:GDN_HEADS + hb + 1, :]],
                               axis=1)
        gam = jnp.exp(jnp.where(incl, gcol - grow, NEG_BIG))
        qkg_ref[slot, p] = (qk * gam).astype(BF16)
        ns.append(jnp.where(strict, kk * gam * bcol, 0.0) * -1.0)
    yield
    ps = [eye2 + n for n in ns]
    nparts = [_split2(n) for n in ns]
    ns = [_pair_matmul([nparts[p]], nparts[p], left)[0] for p in range(npair)]
    yield
    k = 2
    while 2 * k < CHUNK:
        nparts = [_split2(n) for n in ns]
        outs = [_pair_matmul([_split2(ps[p]), nparts[p]], nparts[p], left) for p in range(npair)]
        ps = [ps[p] + outs[p][0] for p in range(npair)]
        ns = [outs[p][1] for p in range(npair)]
        k *= 2
        yield
    for p in range(npair):
        t_inv = ps[p] + _pair_matmul([_split2(ps[p])], _split2(ns[p]), left)[0]
        tinv_ref[slot, p] = t_inv.astype(BF16)
    yield


def _gdn_state_stages(c, slot, q_ref, k_ref, v_ref, slab_ref, slabt_ref, tinv_ref, qkg_ref, o_ref, s_ref):
    npair = GDN_HEADS // 2
    rows = pl.ds(pl.multiple_of(c * CHUNK, CHUNK), CHUNK)
    sl = slab_ref[rows, :]
    st = slabt_ref[c]
    eg = jnp.exp(sl)
    ed = jnp.exp(sl[CHUNK - 1:CHUNK, :] - sl)
    hcols = lambda h: slice(h * GDN_DK, (h + 1) * GDN_DK)
    r1 = []
    for h in range(GDN_HEADS):
        lhs = jnp.concatenate([k_ref[rows, hcols(h)], q_ref[rows, hcols(h)]], axis=0)
        r1.append(jnp.dot(lhs, s_ref[h].astype(BF16), preferred_element_type=F32))
    yield
    egc = [_col(eg, GDN_HEADS + h) for h in range(GDN_HEADS)]
    rhs = [(_col(sl, h) * (v_ref[rows, hcols(h)] - egc[h] * r1[h][0:CHUNK])).astype(BF16) for h in range(GDN_HEADS)]
    v_new = [jnp.dot(tinv_ref[slot, p], _lane_blockdiag(rhs[2 * p], rhs[2 * p + 1]), preferred_element_type=F32)
             for p in range(npair)]
    yield
    for p in range(npair):
        va, vb = v_new[p][:, 0:GDN_DV], v_new[p][:, GDN_DV:2 * GDN_DV]
        qs = jnp.concatenate([egc[2 * p] * r1[2 * p][CHUNK:2 * CHUNK],
                              egc[2 * p + 1] * r1[2 * p + 1][CHUNK:2 * CHUNK]], axis=1)
        o_ref[rows, 2 * p * GDN_DV:(2 * p + 2) * GDN_DV] = (qs + jnp.dot(
            qkg_ref[slot, p], _lane_blockdiag(va.astype(BF16), vb.astype(BF16)), preferred_element_type=F32)
        ).astype(BF16)
        for h, vh in ((2 * p, va), (2 * p + 1, vb)):
            decay = jnp.exp(st[GDN_HEADS + h:GDN_HEADS + h + 1, CHUNK - 1:CHUNK])
            s_ref[h] = s_ref[h] * decay + _dot_tn(k_ref[rows, hcols(h)], vh * _col(ed, GDN_HEADS + h))
    yield


def _interleave(order, streams):
    for ch in order:
        next(streams[ch])


def _gdn_scan_kernel(q_ref, k_ref, v_ref, slab_ref, slabt_ref, o_ref, sout_ref, s_ref, tinv_ref, qkg_ref, *, tl):
    l = pl.program_id(1)
    nchunk = tl // CHUNK

    @pl.when(l == 0)
    def _():
        s_ref[...] = jnp.zeros(s_ref.shape, F32)

    ri = lax.broadcasted_iota(jnp.int32, (CHUNK, 2 * CHUNK), 0)
    li = lax.broadcasted_iota(jnp.int32, (CHUNK, 2 * CHUNK), 1)
    ci = li & (CHUNK - 1)
    masks = (ri >= ci, ri > ci, (ri == ci).astype(F32), li < CHUNK)

    prep = functools.partial(_gdn_prep_stages, k_ref=k_ref, q_ref=q_ref, slab_ref=slab_ref, slabt_ref=slabt_ref,
                             tinv_ref=tinv_ref, qkg_ref=qkg_ref, masks=masks)
    state = functools.partial(_gdn_state_stages, q_ref=q_ref, k_ref=k_ref, v_ref=v_ref, slab_ref=slab_ref,
                              slabt_ref=slabt_ref, tinv_ref=tinv_ref, qkg_ref=qkg_ref, o_ref=o_ref, s_ref=s_ref)

    _interleave("A" * 7, {"A": prep(0, 0)})

    def chunk_body(c, carry):
        slot = c & 1
        nxt = jnp.minimum(c + 1, nchunk - 1)
        _interleave("ABAABAABAA", {"A": prep(nxt, 1 - slot), "B": state(c, slot)})
        return carry

    lax.fori_loop(0, nchunk, chunk_body, 0)

    @pl.when(l == pl.num_programs(1) - 1)
    def _():
        sout_ref[...] = s_ref[...]


def _gdn_scan(q, k, v, slab, slabt, tl):
    bsz, seq, _ = q.shape
    row = lambda w: pl.BlockSpec((None, tl, w), lambda b, l: (b, l, 0))
    pair_scratch = pltpu.VMEM((2, GDN_HEADS // 2, CHUNK, 2 * CHUNK), BF16)
    return pl.pallas_call(
        functools.partial(_gdn_scan_kernel, tl=tl),
        grid=(bsz, seq // tl),
        in_specs=[row(GDN_QK), row(GDN_QK), row(GDN_VD), row(LANES),
                  pl.BlockSpec((None, tl // CHUNK, 2 * GDN_HEADS, CHUNK), lambda b, l: (b, l, 0, 0))],
        out_specs=(row(GDN_VD), pl.BlockSpec((None, GDN_HEADS, GDN_DK, GDN_DV), lambda b, l: (b, 0, 0, 0))),
        out_shape=(jax.ShapeDtypeStruct((bsz, seq, GDN_VD), BF16),
                   jax.ShapeDtypeStruct((bsz, GDN_HEADS, GDN_DK, GDN_DV), F32)),
        scratch_shapes=[pltpu.VMEM((GDN_HEADS, GDN_DK, GDN_DV), F32), pair_scratch, pair_scratch],
        compiler_params=_params(2),
        name="gdn_scan",
    )(q, k, v, slab, slabt)


def _gdn_out_kernel(o_ref, z_ref, x_ref, nw_ref, wout_ref, g_ref, b_ref, y_ref):
    o = o_ref[...].astype(F32)
    z = z_ref[...].astype(F32)
    parts = []
    for h in range(GDN_HEADS):
        hs = slice(h * GDN_DV, (h + 1) * GDN_DV)
        oh = o[:, hs]
        rn = oh * lax.rsqrt(jnp.mean(oh * oh, axis=-1, keepdims=True) + RMS_EPS) * nw_ref[...]
        parts.append((rn * _silu(z[:, hs])).astype(BF16))
    mix = jnp.dot(jnp.concatenate(parts, axis=1), wout_ref[...], preferred_element_type=F32)
    y_ref[...] = _layer_norm(DN_ALPHA * x_ref[...] + mix, g_ref[...], b_ref[...])


def _gdn_out(o, z, x, p, ln_g, ln_b, tl):
    bsz, seq, _ = x.shape
    row = lambda w: pl.BlockSpec((None, tl, w), lambda b, l: (b, l, 0))
    consts = (p["nw"], p["wout"], ln_g, ln_b)
    return pl.pallas_call(
        _gdn_out_kernel,
        grid=(bsz, seq // tl),
        in_specs=[row(GDN_VD), row(GDN_VD), row(D_MODEL)] + [_const_spec(c.shape) for c in consts],
        out_specs=row(D_MODEL),
        out_shape=jax.ShapeDtypeStruct((bsz, seq, D_MODEL), F32),
        compiler_params=_params(2),
        name="gdn_out",
    )(o, z, x, *consts)


def _ffn_kernel(x_ref, wup_ref, cw_ref, cb_ref, wdown_ref, g_ref, b_ref, y_ref, cache_ref, gg_ref, *, tl):
    l = pl.program_id(1)

    @pl.when(l == 0)
    def _():
        gg_ref[0:SUBLANES, :] = jnp.zeros((SUBLANES, D_FF), F32)

    x = x_ref[...]
    xb = x.astype(BF16)
    acc = jnp.zeros((tl, D_MODEL), F32)
    for j in range(D_FF // FFN_BLOCK):
        cs = slice(j * FFN_BLOCK, (j + 1) * FFN_BLOCK)
        vs = slice(D_FF + j * FFN_BLOCK, D_FF + (j + 1) * FFN_BLOCK)
        gg_ref[SUBLANES:SUBLANES + tl, cs] = jnp.dot(xb, wup_ref[:, cs], preferred_element_type=F32)
        val = jnp.dot(xb, wup_ref[:, vs], preferred_element_type=F32)
        hid = _silu(_causal_conv(gg_ref, cw_ref, cb_ref, cs, tl, FFN_CONV_W)) * val
        acc = acc + jnp.dot(hid.astype(BF16), wdown_ref[cs, :], preferred_element_type=F32)
    cache_ref[...] = gg_ref[SUBLANES + tl - (FFN_CONV_W - 1):SUBLANES + tl, :]
    gg_ref[0:SUBLANES, :] = gg_ref[tl:tl + SUBLANES, :]
    y_ref[...] = _layer_norm(DN_ALPHA * x + acc, g_ref[...], b_ref[...])


def _ffn(x, p, ln_g, ln_b, tl):
    bsz, seq, _ = x.shape
    row = lambda w: pl.BlockSpec((None, tl, w), lambda b, l: (b, l, 0))
    consts = (p["wup"], p["cw"], p["cb"], p["wdown"], ln_g, ln_b)
    return pl.pallas_call(
        functools.partial(_ffn_kernel, tl=tl),
        grid=(bsz, seq // tl),
        in_specs=[row(D_MODEL)] + [_const_spec(c.shape) for c in consts],
        out_specs=(row(D_MODEL), pl.BlockSpec((None, FFN_CONV_W - 1, D_FF), lambda b, l: (b, 0, 0))),
        out_shape=(jax.ShapeDtypeStruct((bsz, seq, D_MODEL), F32),
                   jax.ShapeDtypeStruct((bsz, FFN_CONV_W - 1, D_FF), F32)),
        scratch_shapes=[pltpu.VMEM((tl + SUBLANES, D_FF), F32)],
        compiler_params=_params(2),
        name="ffn",
    )(x, *consts)


def _ssd_in_kernel(x_ref, wz_ref, wxbc_ref, wdt_ref, cw_ref, cb_ref, a_ref, dtb_ref, triu_ref,
                   z_ref, xs_ref, bm_ref, cm_ref, slab_ref, slabt_ref, cache_ref, q3_ref, carry_ref, a3_ref, *, tl):
    l = pl.program_id(1)
    xb = x_ref[...].astype(BF16)
    lane = lax.broadcasted_iota(jnp.int32, (tl, LANES), 1)

    dt = _softplus(jnp.dot(xb, wdt_ref[...], preferred_element_type=F32) + dtb_ref[...])
    dt = jnp.where(lane < 2 * SSD_HEADS, dt, 0.0)
    _gate_slabs(jnp.where(lane < SSD_HEADS, dt, dt * a_ref[...]), SSD_HEADS, triu_ref[...], slab_ref, slabt_ref, tl)

    def z_proj():
        z_ref[...] = jnp.dot(xb, wz_ref[...], preferred_element_type=F32).astype(BF16)

    def emit(blk, rows, piece):
        col = blk * LANES
        if col < SSD_INNER:
            xs_ref[rows, col:col + LANES] = piece
        elif col < SSD_INNER + SSD_GN:
            bm_ref[rows, col - SSD_INNER:col - SSD_INNER + LANES] = piece.astype(BF16)
        else:
            cm_ref[rows, col - SSD_INNER - SSD_GN:col - SSD_INNER - SSD_GN + LANES] = piece.astype(BF16)

    _permuted_conv(xb, wxbc_ref, cw_ref, cb_ref, cache_ref, q3_ref, carry_ref, a3_ref, l == 0, tl, emit, z_proj)


def _ssd_in(x, p, tl):
    bsz, seq, _ = x.shape
    triu = _chunk_cumsum_matrix(tl)
    row = lambda w: pl.BlockSpec((None, tl, w), lambda b, l: (b, l, 0))
    out_shape = (jax.ShapeDtypeStruct((bsz, seq, SSD_INNER), BF16), jax.ShapeDtypeStruct((bsz, seq, SSD_INNER), F32),
                 jax.ShapeDtypeStruct((bsz, seq, SSD_GN), BF16), jax.ShapeDtypeStruct((bsz, seq, SSD_GN), BF16),
                 jax.ShapeDtypeStruct((bsz, seq, LANES), F32),
                 jax.ShapeDtypeStruct((bsz, seq // CHUNK, 2 * SSD_HEADS, CHUNK), F32),
                 jax.ShapeDtypeStruct((bsz, CONV_W - 1, SSD_CONV_DIM), F32))
    out_specs = (row(SSD_INNER), row(SSD_INNER), row(SSD_GN), row(SSD_GN), row(LANES),
                 pl.BlockSpec((None, tl // CHUNK, 2 * SSD_HEADS, CHUNK), lambda b, l: (b, l, 0, 0)),
                 pl.BlockSpec((None, CONV_W - 1, SSD_CONV_DIM), lambda b, l: (b, 0, 0)))
    consts = (p["wz"], p["wxbc"], p["wdt"], p["cw"], p["cb"], p["a"], p["dtb"], triu)
    return pl.pallas_call(
        functools.partial(_ssd_in_kernel, tl=tl),
        grid=(bsz, seq // tl),
        in_specs=[row(D_MODEL)] + [_const_spec(c.shape) for c in consts],
        out_specs=out_specs,
        out_shape=out_shape,
        scratch_shapes=_conv_scratch(SSD_CONV_DIM, tl),
        compiler_params=_params(2),
        name="ssd_in",
    )(x, *consts)


def _ssd_scan_kernel(xs_ref, bm_ref, cm_ref, slab_ref, slabt_ref, edt_ref, eac_ref, dskip_ref,
                     y_ref, sout_ref, st_ref, dtx_ref, acx_ref, *, tl):
    l = pl.program_id(1)

    @pl.when(l == 0)
    def _():
        st_ref[...] = jnp.zeros(st_ref.shape, F32)

    hi, mid, lo = _split3(slab_ref[...])
    dtx_ref[...] = jnp.dot(jnp.concatenate([hi, mid], axis=1), edt_ref[...], preferred_element_type=F32)
    acx_ref[...] = jnp.dot(jnp.concatenate([hi, mid, lo], axis=1), eac_ref[...], preferred_element_type=F32)

    ri = lax.broadcasted_iota(jnp.int32, (CHUNK, 2 * CHUNK), 0)
    li = lax.broadcasted_iota(jnp.int32, (CHUNK, 2 * CHUNK), 1)
    incl = ri >= (li & (CHUNK - 1))
    left = li < CHUNK

    def chunk_body(c, carry):
        rows = pl.ds(pl.multiple_of(c * CHUNK, CHUNK), CHUNK)
        sl = slab_ref[rows, :]
        st = slabt_ref[c]
        ac_x = acx_ref[rows, :]
        xs = xs_ref[rows, :]
        xdt = xs * dtx_ref[rows, :]
        alast_x = ac_x[CHUNK - 1:CHUNK, :]
        xdec = (xdt * jnp.exp(alast_x - ac_x)).astype(BF16)
        xdt_b = xdt.astype(BF16)
        ea = jnp.exp(ac_x)
        sdecay = jnp.exp(alast_x)
        cb2, y_off = [], []
        for g in range(SSD_GROUPS):
            gs = slice(g * SSD_GROUP_W, (g + 1) * SSD_GROUP_W)
            ns = slice(g * SSD_STATE, (g + 1) * SSD_STATE)
            cg = cm_ref[rows, ns]
            bg = bm_ref[rows, ns]
            cb2.append(_dot_nt(cg, jnp.concatenate([bg, bg], axis=0)))
            st_g = st_ref[:, gs]
            y_off.append(jnp.dot(cg, st_g.astype(BF16), preferred_element_type=F32))
            st_ref[:, gs] = st_g * sdecay[:, gs] + _dot_tn(bg, xdec[:, gs])
        for g in range(SSD_GROUPS):
            gs = slice(g * SSD_GROUP_W, (g + 1) * SSD_GROUP_W)
            diag = []
            for pp in range(SSD_HPG // 2):
                ha = g * SSD_HPG + 2 * pp
                hb = ha + 1
                acol = jnp.where(left, _col(sl, SSD_HEADS + ha), _col(sl, SSD_HEADS + hb))
                arow = jnp.concatenate([st[SSD_HEADS + ha:SSD_HEADS + ha + 1, :],
                                        st[SSD_HEADS + hb:SSD_HEADS + hb + 1, :]], axis=1)
                seg = jnp.exp(jnp.where(incl, acol - arow, NEG_BIG))
                w = _pair_blockdiag(xdt_b[:, ha * SSD_HEADDIM:(ha + 2) * SSD_HEADDIM], left)
                diag.append(jnp.dot((seg * cb2[g]).astype(BF16), w, preferred_element_type=F32))
            y_ref[rows, gs] = (jnp.concatenate(diag, axis=1) + y_off[g] * ea[:, gs]
                               + dskip_ref[:, gs] * xs[:, gs]).astype(BF16)
        return carry

    lax.fori_loop(0, tl // CHUNK, chunk_body, 0)

    @pl.when(l == pl.num_programs(1) - 1)
    def _():
        sout_ref[...] = st_ref[...].T.reshape(SSD_HEADS, SSD_HEADDIM, SSD_STATE)


def _ssd_scan(xs, bm, cm, slab, slabt, p, tl):
    bsz, seq, _ = xs.shape
    row = lambda w: pl.BlockSpec((None, tl, w), lambda b, l: (b, l, 0))
    consts = (jnp.tile(p["edt"], (2, 1)), jnp.tile(p["eac"], (3, 1)), p["dskip"])
    return pl.pallas_call(
        functools.partial(_ssd_scan_kernel, tl=tl),
        grid=(bsz, seq // tl),
        in_specs=[row(SSD_INNER), row(SSD_GN), row(SSD_GN), row(LANES),
                  pl.BlockSpec((None, tl // CHUNK, 2 * SSD_HEADS, CHUNK), lambda b, l: (b, l, 0, 0))]
                 + [_const_spec(c.shape) for c in consts],
        out_specs=(row(SSD_INNER),
                   pl.BlockSpec((None, SSD_HEADS, SSD_HEADDIM, SSD_STATE), lambda b, l: (b, 0, 0, 0))),
        out_shape=(jax.ShapeDtypeStruct((bsz, seq, SSD_INNER), BF16),
                   jax.ShapeDtypeStruct((bsz, SSD_HEADS, SSD_HEADDIM, SSD_STATE), F32)),
        scratch_shapes=[pltpu.VMEM((SSD_STATE, SSD_INNER), F32), pltpu.VMEM((tl, SSD_INNER), F32),
                        pltpu.VMEM((tl, SSD_INNER), F32)],
        compiler_params=_params(2),
        name="ssd_scan",
    )(xs, bm, cm, slab, slabt, *consts)


def _ssd_out_kernel(y_ref, z_ref, x_ref, nw_ref, wout_ref, g_ref, b_ref, o_ref):
    t = y_ref[...].astype(F32) * _silu(z_ref[...].astype(F32))
    parts = []
    for g in range(SSD_GROUPS):
        gs = slice(g * SSD_GROUP_W, (g + 1) * SSD_GROUP_W)
        tg = t[:, gs]
        parts.append((tg * lax.rsqrt(jnp.mean(tg * tg, axis=-1, keepdims=True) + RMS_EPS) * nw_ref[:, gs]).astype(BF16))
    mix = jnp.dot(jnp.concatenate(parts, axis=1), wout_ref[...], preferred_element_type=F32)
    o_ref[...] = _layer_norm(DN_ALPHA * x_ref[...] + mix, g_ref[...], b_ref[...])


def _ssd_out(y, z, x, p, ln_g, ln_b, tl):
    bsz, seq, _ = x.shape
    row = lambda w: pl.BlockSpec((None, tl, w), lambda b, l: (b, l, 0))
    consts = (p["nw"], p["wout"], ln_g, ln_b)
    return pl.pallas_call(
        _ssd_out_kernel,
        grid=(bsz, seq // tl),
        in_specs=[row(SSD_INNER), row(SSD_INNER), row(D_MODEL)] + [_const_spec(c.shape) for c in consts],
        out_specs=row(D_MODEL),
        out_shape=jax.ShapeDtypeStruct((bsz, seq, D_MODEL), F32),
        compiler_params=_params(2),
        name="ssd_out",
    )(y, z, x, *consts)


def _step_conv(pre, c_refs, cw_ref, cb_ref):
    width = len(c_refs) + 1
    acc = cb_ref[...] + c_refs[0][...] * cw_ref[0:1, :]
    for k in range(1, width - 1):
        acc = acc + c_refs[k][...] * cw_ref[k:k + 1, :]
    return acc + pre * cw_ref[width - 1:width, :]


def _gdn_in_step_kernel(x_ref, c0_ref, c1_ref, c2_ref, wqkv_ref, wz_ref, wba_ref, cw_ref, cb_ref, alog_ref, dtb_ref,
                        q_ref, k_ref, v_ref, z_ref, slab_ref, pre_ref):
    xb = x_ref[...].astype(BF16)
    n = xb.shape[0]
    lane = lax.broadcasted_iota(jnp.int32, (n, LANES), 1)
    ba = jnp.dot(xb, wba_ref[...], preferred_element_type=F32)
    g = -jnp.exp(alog_ref[...]) * _softplus(ba + dtb_ref[...])
    slab_ref[...] = jnp.where(lane < GDN_HEADS, jax.nn.sigmoid(ba), g)
    z_ref[...] = jnp.dot(xb, wz_ref[...], preferred_element_type=F32)
    pre = jnp.dot(xb, wqkv_ref[...], preferred_element_type=F32)
    pre_ref[...] = pre
    act = _silu(_step_conv(pre, (c0_ref, c1_ref, c2_ref), cw_ref, cb_ref))
    for h in range(GDN_HEADS):
        hs = slice(h * GDN_DK, (h + 1) * GDN_DK)
        q_ref[:, hs] = _l2norm(act[:, hs]) * (GDN_DK ** -0.5)
        k_ref[:, hs] = _l2norm(act[:, GDN_QK + h * GDN_DK:GDN_QK + (h + 1) * GDN_DK])
    v_ref[...] = act[:, 2 * GDN_QK:]


def _gdn_in_step(x, cache, p):
    n = x.shape[0]
    ff = lambda w: jax.ShapeDtypeStruct((n, w), F32)
    return pl.pallas_call(
        _gdn_in_step_kernel,
        out_shape=(ff(GDN_QK), ff(GDN_QK), ff(GDN_VD), ff(GDN_VD), ff(LANES), ff(GDN_CONV_DIM)),
        compiler_params=pltpu.CompilerParams(vmem_limit_bytes=VMEM_LIMIT),
        name="gdn_in_step",
    )(x, cache[:, 0], cache[:, 1], cache[:, 2], p["wqkv"], p["wz"], p["wba"], p["cw"], p["cb"], p["alog"], p["dtb"])


def _gdn_step_kernel(q_ref, k_ref, v_ref, slab_ref, s_ref, o_ref, sout_ref, *, tb):
    zpad = jnp.zeros((LANES - GDN_HEADS, GDN_DK), F32)

    def token_body(t, carry):
        kt = jnp.concatenate([k_ref[t], zpad], axis=0).T
        qt = jnp.concatenate([q_ref[t], zpad], axis=0).T
        v = v_ref[t]
        sl = slab_ref[pl.ds(t, 1), :]
        alpha = jnp.exp(sl)
        for h in range(GDN_HEADS):
            s = s_ref[t, h]
            kc = _col(kt, h)
            a_h = alpha[:, GDN_HEADS + h:GDN_HEADS + h + 1]
            ks = jnp.sum(s * kc, axis=0, keepdims=True)
            v_new = sl[:, h:h + 1] * (v[h:h + 1, :] - a_h * ks)
            s_new = s * a_h + kc * v_new
            sout_ref[t, h] = s_new
            o_ref[t, h:h + 1, :] = jnp.sum(s_new * _col(qt, h), axis=0, keepdims=True)
        return carry

    lax.fori_loop(0, tb, token_body, 0)


def _gdn_step(q, k, v, slab, state, tb):
    n = q.shape[0]
    q3, k3, v3 = (t.reshape(n, GDN_HEADS, GDN_DK) for t in (q, k, v))
    tok = pl.BlockSpec((tb, GDN_HEADS, GDN_DK), lambda i: (i, 0, 0))
    st = pl.BlockSpec((tb, GDN_HEADS, GDN_DK, GDN_DV), lambda i: (i, 0, 0, 0))
    o, s_new = pl.pallas_call(
        functools.partial(_gdn_step_kernel, tb=tb),
        grid=(n // tb,),
        in_specs=[tok, tok, tok, pl.BlockSpec((tb, LANES), lambda i: (i, 0)), st],
        out_specs=(tok, st),
        out_shape=(jax.ShapeDtypeStruct((n, GDN_HEADS, GDN_DV), F32), jax.ShapeDtypeStruct(state.shape, F32)),
        compiler_params=_params(1),
        name="gdn_step",
    )(q3, k3, v3, slab, state)
    return o.reshape(n, GDN_VD), s_new


def _ffn_step_kernel(x_ref, c0_ref, c1_ref, wup_ref, cw_ref, cb_ref, wdown_ref, g_ref, b_ref, y_ref, pre_ref):
    x = x_ref[...]
    xb = x.astype(BF16)
    gate = jnp.dot(xb, wup_ref[:, 0:D_FF], preferred_element_type=F32)
    val = jnp.dot(xb, wup_ref[:, D_FF:2 * D_FF], preferred_element_type=F32)
    pre_ref[...] = gate
    hid = _silu(_step_conv(gate, (c0_ref, c1_ref), cw_ref, cb_ref)) * val
    y_ref[...] = _layer_norm(DN_ALPHA * x + jnp.dot(hid.astype(BF16), wdown_ref[...], preferred_element_type=F32),
                             g_ref[...], b_ref[...])


def _ffn_step(x, cache, p, ln_g, ln_b):
    n = x.shape[0]
    return pl.pallas_call(
        _ffn_step_kernel,
        out_shape=(jax.ShapeDtypeStruct((n, D_MODEL), F32), jax.ShapeDtypeStruct((n, D_FF), F32)),
        compiler_params=pltpu.CompilerParams(vmem_limit_bytes=VMEM_LIMIT),
        name="ffn_step",
    )(x, cache[:, 0], cache[:, 1], p["wup"], p["cw"], p["cb"], p["wdown"], ln_g, ln_b)


def _ssd_in_step_kernel(x_ref, c0_ref, c1_ref, c2_ref, wz_ref, wxbc_ref, wdt_ref, cw_ref, cb_ref, a_ref, dtb_ref,
                        edt_ref, eac_ref, z_ref, xs_ref, bm_ref, cm_ref, xdt_ref, dec_ref, pre_ref):
    xb = x_ref[...].astype(BF16)
    n = xb.shape[0]
    lane = lax.broadcasted_iota(jnp.int32, (n, LANES), 1)
    dt = _softplus(jnp.dot(xb, wdt_ref[...], preferred_element_type=F32) + dtb_ref[...])
    dt = jnp.where(lane < 2 * SSD_HEADS, dt, 0.0)
    slab = jnp.where(lane < SSD_HEADS, dt, dt * a_ref[...])
    z_ref[...] = jnp.dot(xb, wz_ref[...], preferred_element_type=F32)
    pre = jnp.dot(xb, wxbc_ref[...], preferred_element_type=F32)
    pre_ref[...] = pre
    act = _silu(_step_conv(pre, (c0_ref, c1_ref, c2_ref), cw_ref, cb_ref))
    xs = act[:, 0:SSD_INNER]
    xs_ref[...] = xs
    bm_ref[...] = act[:, SSD_INNER:SSD_INNER + SSD_GN]
    cm_ref[...] = act[:, SSD_INNER + SSD_GN:]
    xdt_ref[...] = xs * _dot_exact_rhs(slab, edt_ref[...])
    dec_ref[...] = jnp.exp(_dot_exact_rhs(slab, eac_ref[...]))


def _ssd_in_step(x, cache, p):
    n = x.shape[0]
    ff = lambda w: jax.ShapeDtypeStruct((n, w), F32)
    return pl.pallas_call(
        _ssd_in_step_kernel,
        out_shape=(ff(SSD_INNER), ff(SSD_INNER), ff(SSD_GN), ff(SSD_GN), ff(SSD_INNER), ff(SSD_INNER),
                   ff(SSD_CONV_DIM)),
        compiler_params=pltpu.CompilerParams(vmem_limit_bytes=VMEM_LIMIT),
        name="ssd_in_step",
    )(x, cache[:, 0], cache[:, 1], cache[:, 2], p["wz"], p["wxbc"], p["wdt"], p["cw"], p["cb"], p["a"], p["dtb"],
      p["edt"], p["eac"])


def _ssd_step_kernel(xs_ref, bm_ref, cm_ref, xdt_ref, dec_ref, dskip_ref, s_ref, y_ref, sout_ref, *, tb):
    hp = SSD_INNER
    rid = lax.broadcasted_iota(jnp.int32, (SUBLANES, hp), 0)
    gid = lax.broadcasted_iota(jnp.int32, (SUBLANES, hp), 1) // SSD_GROUP_W
    zpad_r = jnp.zeros((LANES - SUBLANES, hp), F32)

    def token_body(t, carry):
        row = pl.ds(t, 1)
        xdt = xdt_ref[row, :]
        dec = dec_ref[row, :]
        stack = jnp.where(rid == gid, jnp.broadcast_to(xdt, (SUBLANES, hp)), 0.0)
        cols = jnp.concatenate([stack, zpad_r], axis=0).T
        s = s_ref[t].reshape(hp, SSD_STATE)
        bm = bm_ref[row, :]
        bmat = jnp.concatenate([bm[:, g * SSD_STATE:(g + 1) * SSD_STATE] for g in range(SSD_GROUPS)]
                               + [jnp.zeros((LANES - SSD_GROUPS, SSD_STATE), F32)], axis=0)
        upd = jnp.dot(cols.astype(BF16), bmat.astype(BF16), preferred_element_type=F32)
        s_new = jnp.concatenate(
            [s[h * SSD_HEADDIM:(h + 1) * SSD_HEADDIM, :] * dec[:, h * SSD_HEADDIM:h * SSD_HEADDIM + 1]
             for h in range(SSD_HEADS)], axis=0) + upd
        sout_ref[t] = s_new.reshape(SSD_HEADS, SSD_HEADDIM, SSD_STATE)
        cm = cm_ref[row, :]
        cmat = jnp.concatenate([cm[:, g * SSD_STATE:(g + 1) * SSD_STATE] for g in range(SSD_GROUPS)]
                               + [jnp.zeros((SUBLANES - SSD_GROUPS, SSD_STATE), F32)], axis=0)
        yall = _dot_nt(cmat, s_new)
        y = jnp.sum(jnp.where(rid == gid, yall, 0.0), axis=0, keepdims=True)
        y_ref[row, :] = y + dskip_ref[...] * xs_ref[row, :]
        return carry

    lax.fori_loop(0, tb, token_body, 0)


def _ssd_step(xs, bm, cm, xdt, dec, state, p, tb):
    n = xs.shape[0]
    tok = lambda w: pl.BlockSpec((tb, w), lambda i: (i, 0))
    st = pl.BlockSpec((tb, SSD_HEADS, SSD_HEADDIM, SSD_STATE), lambda i: (i, 0, 0, 0))
    return pl.pallas_call(
        functools.partial(_ssd_step_kernel, tb=tb),
        grid=(n // tb,),
        in_specs=[tok(SSD_INNER), tok(SSD_GN), tok(SSD_GN), tok(SSD_INNER), tok(SSD_INNER),
                  _const_spec(p["dskip"].shape), st],
        out_specs=(tok(SSD_INNER), st),
        out_shape=(jax.ShapeDtypeStruct((n, SSD_INNER), F32), jax.ShapeDtypeStruct(state.shape, F32)),
        compiler_params=_params(1),
        name="ssd_step",
    )(xs, bm, cm, xdt, dec, p["dskip"], state)


def _row(v, width=None, offset=0):
    v = v.astype(F32).reshape(1, -1)
    if width is None:
        return v
    return jnp.pad(v, ((0, 0), (offset, width - offset - v.shape[1])))


def _prep_gdn(w_in, conv_w, conv_b, a_log, dt_bias, norm_w, w_out):
    ba = w_in[:, GDN_CONV_DIM + GDN_VD:]
    return {
        "wqkv": w_in[:, :GDN_CONV_DIM].astype(BF16),
        "wz": w_in[:, GDN_CONV_DIM:GDN_CONV_DIM + GDN_VD].astype(BF16),
        "wba": jnp.pad(ba, ((0, 0), (0, LANES - ba.shape[1]))).astype(BF16),
        "cw": conv_w.astype(F32),
        "cb": _row(conv_b),
        "alog": _row(a_log, LANES, GDN_HEADS),
        "dtb": _row(dt_bias, LANES, GDN_HEADS),
        "nw": _row(norm_w),
        "wout": w_out.astype(BF16),
    }


def _prep_ssd(w_in, conv_w, conv_b, a_log, dt_bias, d_skip, norm_w, w_out):
    wdt = w_in[:, SSD_INNER + SSD_CONV_DIM:]
    wdt2 = jnp.concatenate([wdt, wdt], axis=1)
    head_of_lane = jnp.arange(SSD_INNER) // SSD_HEADDIM
    sel = jnp.arange(LANES)[:, None]
    return {
        "wz": w_in[:, :SSD_INNER].astype(BF16),
        "wxbc": w_in[:, SSD_INNER:SSD_INNER + SSD_CONV_DIM].astype(BF16),
        "wdt": jnp.pad(wdt2, ((0, 0), (0, LANES - 2 * SSD_HEADS))).astype(BF16),
        "cw": conv_w.astype(F32),
        "cb": _row(conv_b),
        "a": _row(-jnp.exp(a_log.astype(F32)), LANES, SSD_HEADS),
        "dtb": _row(jnp.concatenate([dt_bias, dt_bias]), LANES, 0),
        "edt": (sel == head_of_lane[None, :]).astype(BF16),
        "eac": (sel == head_of_lane[None, :] + SSD_HEADS).astype(BF16),
        "dskip": _row(jnp.repeat(d_skip, SSD_HEADDIM)),
        "nw": _row(norm_w),
        "wout": w_out.astype(BF16),
    }


def _prep_ffn(w_up, conv_w, conv_b, w_down):
    return {"wup": w_up.astype(BF16), "cw": conv_w.astype(F32), "cb": _row(conv_b), "wdown": w_down.astype(BF16)}


def _prompt_trunk(x, gdn, ssd, ffn, ln, tl):
    q, k, v, z, slab, slabt, gdn_cache = _gdn_in(x, gdn, tl)
    o, gdn_state = _gdn_scan(q, k, v, slab, slabt, min(GDN_SCAN_TILE, x.shape[1]))
    x = _gdn_out(o, z, x, gdn, ln[0][0], ln[0][1], tl)
    x, ffn_cache0 = _ffn(x, ffn[0], ln[0][2], ln[0][3], tl)
    z, xs, bm, cm, slab, slabt, ssd_cache = _ssd_in(x, ssd, tl)
    y, ssd_state = _ssd_scan(xs, bm, cm, slab, slabt, ssd, tl)
    x = _ssd_out(y, z, x, ssd, ln[1][0], ln[1][1], tl)
    x, ffn_cache1 = _ffn(x, ffn[1], ln[1][2], ln[1][3], tl)
    return x, gdn_cache, gdn_state, ssd_cache, ssd_state, jnp.stack([ffn_cache0, ffn_cache1])


def _sample_trunk(x, gdn_cache, gdn_state, ssd_cache, ssd_state, ffn_cache, gdn, ssd, ffn, ln, tb):
    n = x.shape[0]
    q, k, v, z, slab, pre = _gdn_in_step(x, gdn_cache, gdn)
    gdn_cache_new = jnp.concatenate([gdn_cache[:, 1:], pre[:, None]], axis=1)
    o, gdn_state_new = _gdn_step(q, k, v, slab, gdn_state, tb)
    x = _gdn_out(o[None], z[None], x[None], gdn, ln[0][0], ln[0][1], n)[0]
    x, pre = _ffn_step(x, ffn_cache[0], ffn[0], ln[0][2], ln[0][3])
    ffn_cache0 = jnp.concatenate([ffn_cache[0][:, 1:], pre[:, None]], axis=1)
    z, xs, bm, cm, xdt, dec, pre = _ssd_in_step(x, ssd_cache, ssd)
    ssd_cache_new = jnp.concatenate([ssd_cache[:, 1:], pre[:, None]], axis=1)
    y, ssd_state_new = _ssd_step(xs, bm, cm, xdt, dec, ssd_state, ssd, tb)
    x = _ssd_out(y[None], z[None], x[None], ssd, ln[1][0], ln[1][1], n)[0]
    x, pre = _ffn_step(x, ffn_cache[1], ffn[1], ln[1][2], ln[1][3])
    ffn_cache1 = jnp.concatenate([ffn_cache[1][:, 1:], pre[:, None]], axis=1)
    return x, gdn_cache_new, gdn_state_new, ssd_cache_new, ssd_state_new, jnp.stack([ffn_cache0, ffn_cache1])


PROMPT_TILE = 512
GDN_SCAN_TILE = 1024
SAMPLE_TOKENS = 8


def kernel(x_prompt, x_sample, cache_gdn_conv, state_gdn, cache_ssd_conv, state_ssd, cache_ffn_conv, gdn_w_in, gdn_conv_w, gdn_conv_b, gdn_a_log, gdn_dt_bias, gdn_norm_w, gdn_w_out, ssd_w_in, ssd_conv_w, ssd_conv_b, ssd_a_log, ssd_dt_bias, ssd_d, ssd_norm_w, ssd_w_out, ffn_w_up, ffn_conv_w, ffn_conv_b, ffn_w_down, ln1_g, ln1_b, ln2_g, ln2_b):
    gdn = _prep_gdn(gdn_w_in[0], gdn_conv_w[0], gdn_conv_b[0], gdn_a_log[0], gdn_dt_bias[0], gdn_norm_w[0],
                    gdn_w_out[0])
    ssd = _prep_ssd(ssd_w_in[0], ssd_conv_w[0], ssd_conv_b[0], ssd_a_log[0], ssd_dt_bias[0], ssd_d[0],
                    ssd_norm_w[0], ssd_w_out[0])
    ffn = [_prep_ffn(ffn_w_up[i], ffn_conv_w[i], ffn_conv_b[i], ffn_w_down[i]) for i in range(DEPTH)]
    ln = [(_row(ln1_g[i]), _row(ln1_b[i]), _row(ln2_g[i]), _row(ln2_b[i])) for i in range(DEPTH)]

    tl = min(PROMPT_TILE, x_prompt.shape[1])
    y_p, gcp, gsp, scp, ssp, fcp = _prompt_trunk(x_prompt, gdn, ssd, ffn, ln, tl)
    y_s, gcs, gss, scs, sss, fcs = _sample_trunk(
        x_sample[:, 0], cache_gdn_conv[0], state_gdn[0], cache_ssd_conv[0], state_ssd[0], cache_ffn_conv,
        gdn, ssd, ffn, ln, min(SAMPLE_TOKENS, x_sample.shape[0]))
    return (y_p, y_s[:, None], gcp[None], gcs[None], gsp[None], gss[None], scp[None], scs[None],
            ssp[None], sss[None], fcp, fcs)
```

```python
import functools

import jax
import jax.numpy as jnp
from jax import lax
from jax.experimental import pallas as pl
from jax.experimental.pallas import tpu as pltpu

F32 = jnp.float32
BF16 = jnp.bfloat16

D_MODEL = 1024
DEPTH = 2
CONV_W = 4
CHUNK = 64

GDN_HEADS = 8
GDN_DK = 128
GDN_DV = 128
GDN_QK = GDN_HEADS * GDN_DK
GDN_VD = GDN_HEADS * GDN_DV
GDN_CONV_DIM = 2 * GDN_QK + GDN_VD

SSD_INNER = 2 * D_MODEL
SSD_HEADDIM = 64
SSD_HEADS = SSD_INNER // SSD_HEADDIM
SSD_GROUPS = 4
SSD_STATE = 128
SSD_HPG = SSD_HEADS // SSD_GROUPS
SSD_GN = SSD_GROUPS * SSD_STATE
SSD_CONV_DIM = SSD_INNER + 2 * SSD_GN
SSD_GROUP_W = SSD_INNER // SSD_GROUPS

D_FF = 2816
FFN_CONV_W = 3
FFN_BLOCK = D_FF

DN_ALPHA = (2 * DEPTH) ** 0.25
LN_EPS = 1e-5
RMS_EPS = 1e-6
L2_EPS = 1e-6

LANES = 128
SUBLANES = 8
VMEM_LIMIT = 56 * 1024 * 1024

NEG_BIG = -1e30


def _dot_nt(a, b):
    return lax.dot_general(a.astype(BF16), b.astype(BF16), (((1,), (1,)), ((), ())),
                           preferred_element_type=F32)


def _dot_tn(a, b):
    return lax.dot_general(a.astype(BF16), b.astype(BF16), (((0,), (0,)), ((), ())),
                           preferred_element_type=F32)


def _split3(x):
    hi = x.astype(BF16)
    r1 = x - hi.astype(F32)
    mid = r1.astype(BF16)
    lo = (r1 - mid.astype(F32)).astype(BF16)
    return hi, mid, lo


def _dot_exact_rhs(x, e):
    hi, mid, lo = _split3(x)
    return (jnp.dot(hi, e, preferred_element_type=F32) + jnp.dot(mid, e, preferred_element_type=F32)
            + jnp.dot(lo, e, preferred_element_type=F32))


def _silu(x):
    return x * jax.nn.sigmoid(x)


def _softplus(x):
    return jnp.maximum(x, 0.0) + jnp.log1p(jnp.exp(-jnp.abs(x)))


def _layer_norm(x, g, b):
    mu = jnp.mean(x, axis=-1, keepdims=True)
    xc = x - mu
    var = jnp.mean(xc * xc, axis=-1, keepdims=True)
    return xc * lax.rsqrt(var + LN_EPS) * g + b


def _l2norm(t):
    return t * lax.rsqrt(jnp.sum(t * t, axis=-1, keepdims=True) + L2_EPS)


def _col(a, i):
    return a[:, i:i + 1]


def _const_spec(shape):
    nd = len(shape)
    return pl.BlockSpec(shape, lambda *_: (0,) * nd)


def _params(n_grid):
    return pltpu.CompilerParams(dimension_semantics=("arbitrary",) * n_grid, vmem_limit_bytes=VMEM_LIMIT)


def _chunk_cumsum_matrix(tl):
    r = jnp.arange(tl)
    same = (r[:, None] // CHUNK) == (r[None, :] // CHUNK)
    return (same & (r[:, None] <= r[None, :])).astype(BF16)


def _gate_slabs(pre, n_keep, triu, slab_ref, slabt_ref, tl):
    pre_t = pre.T[0:2 * n_keep, :]
    cum_t = _dot_exact_rhs(pre_t, triu)
    row = lax.broadcasted_iota(jnp.int32, pre_t.shape, 0)
    st = jnp.where(row < n_keep, pre_t, cum_t)
    for c in range(tl // CHUNK):
        slabt_ref[c] = st[:, c * CHUNK:(c + 1) * CHUNK]
    slab_ref[...] = jnp.concatenate([st, jnp.zeros((LANES - 2 * n_keep, tl), F32)], axis=0).T


def _causal_conv(xx_ref, cw_ref, cb_ref, cs, tl, width):
    base = SUBLANES - (width - 1)
    acc = cb_ref[:, cs] + xx_ref[base:base + tl, cs] * cw_ref[0:1, cs]
    for k in range(1, width):
        acc = acc + xx_ref[base + k:base + k + tl, cs] * cw_ref[k:k + 1, cs]
    return acc


CONV_HEAD = (CONV_W - 1) * SUBLANES
CONV_ROWS = 64


def _permuted_conv(xb, w_ref, cw_ref, cb_ref, cache_ref, q3_ref, carry_ref, a3_ref, first, tl, emit, extra):
    n = tl // SUBLANES
    per = D_MODEL // LANES
    ngroup = w_ref.shape[1] // D_MODEL
    sub0 = lax.broadcasted_iota(jnp.int32, (SUBLANES, LANES), 0) == 0

    @pl.when(first)
    def _():
        carry_ref[...] = jnp.zeros(carry_ref.shape, F32)

    def project(j):
        cs = slice(j * D_MODEL, (j + 1) * D_MODEL)
        pre = jnp.dot(xb, w_ref[:, cs], preferred_element_type=F32)
        cache_ref[:, cs] = pre[tl - (CONV_W - 1):tl, :]
        for c in range(per):
            blk = j * per + c
            for a in range(SUBLANES):
                q3_ref[blk, pl.ds(CONV_HEAD + a, n, stride=SUBLANES), :] = (
                    pre[a * n:(a + 1) * n, c * LANES:(c + 1) * LANES])
            for m in range(CONV_W - 1):
                src = CONV_HEAD + SUBLANES * (n - (CONV_W - 1) + m)
                dst = slice(m * SUBLANES, (m + 1) * SUBLANES)
                rolled = pltpu.roll(q3_ref[blk, src:src + SUBLANES, :], 1, 0)
                q3_ref[blk, dst, :] = jnp.where(sub0, carry_ref[blk, dst, :], rolled)
                carry_ref[blk, dst, :] = rolled

    def convolve(j):
        for c in range(per):
            blk = j * per + c
            cs = slice(blk * LANES, (blk + 1) * LANES)
            for r0 in range(0, tl, CONV_ROWS):
                acc = cb_ref[:, cs] + q3_ref[blk, r0:r0 + CONV_ROWS, :] * cw_ref[0:1, cs]
                for k in range(1, CONV_W):
                    acc = acc + q3_ref[blk, r0 + k * SUBLANES:r0 + k * SUBLANES + CONV_ROWS, :] * cw_ref[k:k + 1, cs]
                a3_ref[c, r0:r0 + CONV_ROWS, :] = _silu(acc)
            for a in range(SUBLANES):
                emit(blk, slice(a * n, (a + 1) * n), a3_ref[c, pl.ds(a, n, stride=SUBLANES), :])

    project(0)
    for j in range(ngroup):
        if j + 1 < ngroup:
            project(j + 1)
        else:
            extra()
        convolve(j)


def _conv_scratch(ncols, tl):
    nblk = ncols // LANES
    return [pltpu.VMEM((nblk, CONV_HEAD + tl, LANES), F32), pltpu.VMEM((nblk, CONV_HEAD, LANES), F32),
            pltpu.VMEM((D_MODEL // LANES, tl, LANES), F32)]


def _gdn_in_kernel(x_ref, wqkv_ref, wz_ref, wba_ref, cw_ref, cb_ref, alog_ref, dtb_ref, triu_ref,
                   q_ref, k_ref, v_ref, z_ref, slab_ref, slabt_ref, cache_ref, q3_ref, carry_ref, a3_ref, *, tl):
    l = pl.program_id(1)
    xb = x_ref[...].astype(BF16)
    lane = lax.broadcasted_iota(jnp.int32, (tl, LANES), 1)

    ba = jnp.dot(xb, wba_ref[...], preferred_element_type=F32)
    g = -jnp.exp(alog_ref[...]) * _softplus(ba + dtb_ref[...])
    _gate_slabs(jnp.where(lane < GDN_HEADS, jax.nn.sigmoid(ba), g), GDN_HEADS, triu_ref[...], slab_ref, slabt_ref, tl)

    def z_proj():
        z_ref[...] = jnp.dot(xb, wz_ref[...], preferred_element_type=F32).astype(BF16)

    def emit(blk, rows, piece):
        j, h = divmod(blk, GDN_HEADS)
        hs = slice(h * GDN_DK, (h + 1) * GDN_DK)
        if j == 0:
            q_ref[rows, hs] = (_l2norm(piece) * (GDN_DK ** -0.5)).astype(BF16)
        elif j == 1:
            k_ref[rows, hs] = _l2norm(piece).astype(BF16)
        else:
            v_ref[rows, hs] = piece

    _permuted_conv(xb, wqkv_ref, cw_ref, cb_ref, cache_ref, q3_ref, carry_ref, a3_ref, l == 0, tl, emit, z_proj)


def _gdn_in(x, p, tl):
    bsz, seq, _ = x.shape
    nl = seq // tl
    triu = _chunk_cumsum_matrix(tl)
    row = lambda w: pl.BlockSpec((None, tl, w), lambda b, l: (b, l, 0))
    bf = lambda w: jax.ShapeDtypeStruct((bsz, seq, w), BF16)
    ff = lambda w: jax.ShapeDtypeStruct((bsz, seq, w), F32)
    out_shape = (bf(GDN_QK), bf(GDN_QK), ff(GDN_VD), bf(GDN_VD), ff(LANES),
                 jax.ShapeDtypeStruct((bsz, seq // CHUNK, 2 * GDN_HEADS, CHUNK), F32),
                 jax.ShapeDtypeStruct((bsz, CONV_W - 1, GDN_CONV_DIM), F32))
    out_specs = (row(GDN_QK), row(GDN_QK), row(GDN_VD), row(GDN_VD), row(LANES),
                 pl.BlockSpec((None, tl // CHUNK, 2 * GDN_HEADS, CHUNK), lambda b, l: (b, l, 0, 0)),
                 pl.BlockSpec((None, CONV_W - 1, GDN_CONV_DIM), lambda b, l: (b, 0, 0)))
    consts = (p["wqkv"], p["wz"], p["wba"], p["cw"], p["cb"], p["alog"], p["dtb"], triu)
    return pl.pallas_call(
        functools.partial(_gdn_in_kernel, tl=tl),
        grid=(bsz, nl),
        in_specs=[row(D_MODEL)] + [_const_spec(c.shape) for c in consts],
        out_specs=out_specs,
        out_shape=out_shape,
        scratch_shapes=_conv_scratch(GDN_CONV_DIM, tl),
        compiler_params=_params(2),
        name="gdn_in",
    )(x, *consts)


def _split2(x):
    hi = x.astype(BF16)
    return hi, (x - hi.astype(F32)).astype(BF16)


def _pair_blockdiag(x, left):
    z = jnp.zeros_like(x)
    return jnp.concatenate([jnp.where(left, x, z), jnp.where(left, z, x)], axis=0)


def _lane_blockdiag(a, b):
    z = jnp.zeros_like(a)
    return jnp.concatenate([jnp.concatenate([a, z], axis=1), jnp.concatenate([z, b], axis=1)], axis=0)


def _pair_matmul(lhs_parts, w_parts, left):
    w_hi = _pair_blockdiag(w_parts[0], left)
    w = jnp.concatenate([w_hi, w_hi, _pair_blockdiag(w_parts[1], left)], axis=0)
    lhs = [jnp.concatenate([hi, lo, hi], axis=1) for hi, lo in lhs_parts]
    r = jnp.dot(jnp.concatenate(lhs, axis=0) if len(lhs) > 1 else lhs[0], w, preferred_element_type=F32)
    return [r[i * CHUNK:(i + 1) * CHUNK] for i in range(len(lhs))]


def _gdn_prep_stages(c, slot, k_ref, q_ref, slab_ref, slabt_ref, tinv_ref, qkg_ref, masks):
    incl, strict, eye2, left = masks
    npair = GDN_HEADS // 2
    rows = pl.ds(pl.multiple_of(c * CHUNK, CHUNK), CHUNK)
    sl = slab_ref[rows, :]
    st = slabt_ref[c]
    ns = []
    for p in range(npair):
        ha, hb = 2 * p, 2 * p + 1
        ka = k_ref[rows, ha * GDN_DK:(ha + 1) * GDN_DK]
        kb = k_ref[rows, hb * GDN_DK:(hb + 1) * GDN_DK]
        kk = jnp.concatenate([_dot_nt(ka, ka), _dot_nt(kb, kb)], axis=1)
        qk = jnp.concatenate([_dot_nt(q_ref[rows, ha * GDN_DK:(ha + 1) * GDN_DK], ka),
                              _dot_nt(q_ref[rows, hb * GDN_DK:(hb + 1) * GDN_DK], kb)], axis=1)
        gcol = jnp.where(left, _col(sl, GDN_HEADS + ha), _col(sl, GDN_HEADS + hb))
        bcol = jnp.where(left, _col(sl, ha), _col(sl, hb))
        grow = jnp.concatenate([st[GDN_HEADS + ha:GDN_HEADS + ha + 1, :], st[GDN_HEADS + hb:GDN_HEADS + hb + 1, :]],
                               axis=1)
        gam = jnp.exp(jnp.where(incl, gcol - grow, NEG_BIG))
        qkg_ref[slot, p] = (qk * gam).astype(BF16)
        ns.append(jnp.where(strict, kk * gam * bcol, 0.0) * -1.0)
    yield
    ps = [eye2 + n for n in ns]
    nparts = [_split2(n) for n in ns]
    ns = [_pair_matmul([nparts[p]], nparts[p], left)[0] for p in range(npair)]
    yield
    k = 2
    while 2 * k < CHUNK:
        nparts = [_split2(n) for n in ns]
        outs = [_pair_matmul([_split2(ps[p]), nparts[p]], nparts[p], left) for p in range(npair)]
        ps = [ps[p] + outs[p][0] for p in range(npair)]
        ns = [outs[p][1] for p in range(npair)]
        k *= 2
        yield
    for p in range(npair):
        t_inv = ps[p] + _pair_matmul([_split2(ps[p])], _split2(ns[p]), left)[0]
        tinv_ref[slot, p] = t_inv.astype(BF16)
    yield


def _gdn_state_stages(c, slot, q_ref, k_ref, v_ref, slab_ref, slabt_ref, tinv_ref, qkg_ref, o_ref, s_ref):
    npair = GDN_HEADS // 2
    rows = pl.ds(pl.multiple_of(c * CHUNK, CHUNK), CHUNK)
    sl = slab_ref[rows, :]
    st = slabt_ref[c]
    eg = jnp.exp(sl)
    ed = jnp.exp(sl[CHUNK - 1:CHUNK, :] - sl)
    hcols = lambda h: slice(h * GDN_DK, (h + 1) * GDN_DK)
    r1 = []
    for h in range(GDN_HEADS):
        lhs = jnp.concatenate([k_ref[rows, hcols(h)], q_ref[rows, hcols(h)]], axis=0)
        r1.append(jnp.dot(lhs, s_ref[h].astype(BF16), preferred_element_type=F32))
    yield
    egc = [_col(eg, GDN_HEADS + h) for h in range(GDN_HEADS)]
    rhs = [(_col(sl, h) * (v_ref[rows, hcols(h)] - egc[h] * r1[h][0:CHUNK])).astype(BF16) for h in range(GDN_HEADS)]
    v_new = [jnp.dot(tinv_ref[slot, p], _lane_blockdiag(rhs[2 * p], rhs[2 * p + 1]), preferred_element_type=F32)
             for p in range(npair)]
    yield
    for p in range(npair):
        va, vb = v_new[p][:, 0:GDN_DV], v_new[p][:, GDN_DV:2 * GDN_DV]
        qs = jnp.concatenate([egc[2 * p] * r1[2 * p][CHUNK:2 * CHUNK],
                              egc[2 * p + 1] * r1[2 * p + 1][CHUNK:2 * CHUNK]], axis=1)
        o_ref[rows, 2 * p * GDN_DV:(2 * p + 2) * GDN_DV] = (qs + jnp.dot(
            qkg_ref[slot, p], _lane_blockdiag(va.astype(BF16), vb.astype(BF16)), preferred_element_type=F32)
        ).astype(BF16)
        for h, vh in ((2 * p, va), (2 * p + 1, vb)):
            decay = jnp.exp(st[GDN_HEADS + h:GDN_HEADS + h + 1, CHUNK - 1:CHUNK])
            s_ref[h] = s_ref[h] * decay + _dot_tn(k_ref[rows, hcols(h)], vh * _col(ed, GDN_HEADS + h))
    yield


def _interleave(order, streams):
    for ch in order:
        next(streams[ch])


def _gdn_out_math(o, z, x, nw_ref, wout_ref, g_ref, b_ref):
    parts = []
    for h in range(GDN_HEADS):
        hs = slice(h * GDN_DV, (h + 1) * GDN_DV)
        oh = o[:, hs]
        rn = oh * lax.rsqrt(jnp.mean(oh * oh, axis=-1, keepdims=True) + RMS_EPS) * nw_ref[...]
        parts.append((rn * _silu(z[:, hs])).astype(BF16))
    mix = jnp.dot(jnp.concatenate(parts, axis=1), wout_ref[...], preferred_element_type=F32)
    return _layer_norm(DN_ALPHA * x + mix, g_ref[...], b_ref[...])


def _gdn_scan_kernel(q_ref, k_ref, v_ref, slab_ref, slabt_ref, z_ref, x_ref, nw_ref, wout_ref, g_ref, b_ref,
                     y_ref, sout_ref, s_ref, tinv_ref, qkg_ref, o_ref, *, tl):
    l = pl.program_id(1)
    nchunk = tl // CHUNK

    @pl.when(l == 0)
    def _():
        s_ref[...] = jnp.zeros(s_ref.shape, F32)

    ri = lax.broadcasted_iota(jnp.int32, (CHUNK, 2 * CHUNK), 0)
    li = lax.broadcasted_iota(jnp.int32, (CHUNK, 2 * CHUNK), 1)
    ci = li & (CHUNK - 1)
    masks = (ri >= ci, ri > ci, (ri == ci).astype(F32), li < CHUNK)

    prep = functools.partial(_gdn_prep_stages, k_ref=k_ref, q_ref=q_ref, slab_ref=slab_ref, slabt_ref=slabt_ref,
                             tinv_ref=tinv_ref, qkg_ref=qkg_ref, masks=masks)
    state = functools.partial(_gdn_state_stages, q_ref=q_ref, k_ref=k_ref, v_ref=v_ref, slab_ref=slab_ref,
                              slabt_ref=slabt_ref, tinv_ref=tinv_ref, qkg_ref=qkg_ref, o_ref=o_ref, s_ref=s_ref)

    _interleave("A" * 7, {"A": prep(0, 0)})

    def chunk_body(c, carry):
        slot = c & 1
        nxt = jnp.minimum(c + 1, nchunk - 1)
        _interleave("ABAABAABAA", {"A": prep(nxt, 1 - slot), "B": state(c, slot)})
        return carry

    lax.fori_loop(0, nchunk, chunk_body, 0)

    for r0 in range(0, tl, GDN_OUT_ROWS):
        rows = slice(r0, r0 + GDN_OUT_ROWS)
        y_ref[rows, :] = _gdn_out_math(o_ref[rows, :].astype(F32), z_ref[rows, :].astype(F32), x_ref[rows, :],
                                       nw_ref, wout_ref, g_ref, b_ref)

    @pl.when(l == pl.num_programs(1) - 1)
    def _():
        sout_ref[...] = s_ref[...]


def _gdn_scan(q, k, v, slab, slabt, z, x, p, ln_g, ln_b, tl):
    bsz, seq, _ = q.shape
    row = lambda w: pl.BlockSpec((None, tl, w), lambda b, l: (b, l, 0))
    pair_scratch = pltpu.VMEM((2, GDN_HEADS // 2, CHUNK, 2 * CHUNK), BF16)
    consts = (p["nw"], p["wout"], ln_g, ln_b)
    return pl.pallas_call(
        functools.partial(_gdn_scan_kernel, tl=tl),
        grid=(bsz, seq // tl),
        in_specs=[row(GDN_QK), row(GDN_QK), row(GDN_VD), row(LANES),
                  pl.BlockSpec((None, tl // CHUNK, 2 * GDN_HEADS, CHUNK), lambda b, l: (b, l, 0, 0)),
                  row(GDN_VD), row(D_MODEL)] + [_const_spec(c.shape) for c in consts],
        out_specs=(row(D_MODEL), pl.BlockSpec((None, GDN_HEADS, GDN_DK, GDN_DV), lambda b, l: (b, 0, 0, 0))),
        out_shape=(jax.ShapeDtypeStruct((bsz, seq, D_MODEL), F32),
                   jax.ShapeDtypeStruct((bsz, GDN_HEADS, GDN_DK, GDN_DV), F32)),
        scratch_shapes=[pltpu.VMEM((GDN_HEADS, GDN_DK, GDN_DV), F32), pair_scratch, pair_scratch,
                        pltpu.VMEM((tl, GDN_VD), BF16)],
        compiler_params=_params(2),
        name="gdn_scan",
    )(q, k, v, slab, slabt, z, x, *consts)


def _gdn_out_kernel(o_ref, z_ref, x_ref, nw_ref, wout_ref, g_ref, b_ref, y_ref):
    y_ref[...] = _gdn_out_math(o_ref[...].astype(F32), z_ref[...].astype(F32), x_ref[...],
                               nw_ref, wout_ref, g_ref, b_ref)


def _gdn_out(o, z, x, p, ln_g, ln_b, tl):
    bsz, seq, _ = x.shape
    row = lambda w: pl.BlockSpec((None, tl, w), lambda b, l: (b, l, 0))
    consts = (p["nw"], p["wout"], ln_g, ln_b)
    return pl.pallas_call(
        _gdn_out_kernel,
        grid=(bsz, seq // tl),
        in_specs=[row(GDN_VD), row(GDN_VD), row(D_MODEL)] + [_const_spec(c.shape) for c in consts],
        out_specs=row(D_MODEL),
        out_shape=jax.ShapeDtypeStruct((bsz, seq, D_MODEL), F32),
        compiler_params=_params(2),
        name="gdn_out",
    )(o, z, x, *consts)


def _ffn_kernel(x_ref, wup_ref, cw_ref, cb_ref, wdown_ref, g_ref, b_ref, y_ref, cache_ref, gg_ref, *, tl):
    l = pl.program_id(1)

    @pl.when(l == 0)
    def _():
        gg_ref[0:SUBLANES, :] = jnp.zeros((SUBLANES, D_FF), F32)

    x = x_ref[...]
    xb = x.astype(BF16)
    acc = jnp.zeros((tl, D_MODEL), F32)
    for j in range(D_FF // FFN_BLOCK):
        cs = slice(j * FFN_BLOCK, (j + 1) * FFN_BLOCK)
        vs = slice(D_FF + j * FFN_BLOCK, D_FF + (j + 1) * FFN_BLOCK)
        gg_ref[SUBLANES:SUBLANES + tl, cs] = jnp.dot(xb, wup_ref[:, cs], preferred_element_type=F32)
        val = jnp.dot(xb, wup_ref[:, vs], preferred_element_type=F32)
        hid = _silu(_causal_conv(gg_ref, cw_ref, cb_ref, cs, tl, FFN_CONV_W)) * val
        acc = acc + jnp.dot(hid.astype(BF16), wdown_ref[cs, :], preferred_element_type=F32)
    cache_ref[...] = gg_ref[SUBLANES + tl - (FFN_CONV_W - 1):SUBLANES + tl, :]
    gg_ref[0:SUBLANES, :] = gg_ref[tl:tl + SUBLANES, :]
    y_ref[...] = _layer_norm(DN_ALPHA * x + acc, g_ref[...], b_ref[...])


def _ffn(x, p, ln_g, ln_b, tl):
    bsz, seq, _ = x.shape
    row = lambda w: pl.BlockSpec((None, tl, w), lambda b, l: (b, l, 0))
    consts = (p["wup"], p["cw"], p["cb"], p["wdown"], ln_g, ln_b)
    return pl.pallas_call(
        functools.partial(_ffn_kernel, tl=tl),
        grid=(bsz, seq // tl),
        in_specs=[row(D_MODEL)] + [_const_spec(c.shape) for c in consts],
        out_specs=(row(D_MODEL), pl.BlockSpec((None, FFN_CONV_W - 1, D_FF), lambda b, l: (b, 0, 0))),
        out_shape=(jax.ShapeDtypeStruct((bsz, seq, D_MODEL), F32),
                   jax.ShapeDtypeStruct((bsz, FFN_CONV_W - 1, D_FF), F32)),
        scratch_shapes=[pltpu.VMEM((tl + SUBLANES, D_FF), F32)],
        compiler_params=_params(2),
        name="ffn",
    )(x, *consts)


def _ssd_in_kernel(x_ref, wz_ref, wxbc_ref, wdt_ref, cw_ref, cb_ref, a_ref, dtb_ref, triu_ref,
                   z_ref, xs_ref, bm_ref, cm_ref, slab_ref, slabt_ref, cache_ref, q3_ref, carry_ref, a3_ref, *, tl):
    l = pl.program_id(1)
    xb = x_ref[...].astype(BF16)
    lane = lax.broadcasted_iota(jnp.int32, (tl, LANES), 1)

    dt = _softplus(jnp.dot(xb, wdt_ref[...], preferred_element_type=F32) + dtb_ref[...])
    dt = jnp.where(lane < 2 * SSD_HEADS, dt, 0.0)
    _gate_slabs(jnp.where(lane < SSD_HEADS, dt, dt * a_ref[...]), SSD_HEADS, triu_ref[...], slab_ref, slabt_ref, tl)

    def z_proj():
        z_ref[...] = jnp.dot(xb, wz_ref[...], preferred_element_type=F32).astype(BF16)

    def emit(blk, rows, piece):
        col = blk * LANES
        if col < SSD_INNER:
            xs_ref[rows, col:col + LANES] = piece
        elif col < SSD_INNER + SSD_GN:
            bm_ref[rows, col - SSD_INNER:col - SSD_INNER + LANES] = piece.astype(BF16)
        else:
            cm_ref[rows, col - SSD_INNER - SSD_GN:col - SSD_INNER - SSD_GN + LANES] = piece.astype(BF16)

    _permuted_conv(xb, wxbc_ref, cw_ref, cb_ref, cache_ref, q3_ref, carry_ref, a3_ref, l == 0, tl, emit, z_proj)


def _ssd_in(x, p, tl):
    bsz, seq, _ = x.shape
    triu = _chunk_cumsum_matrix(tl)
    row = lambda w: pl.BlockSpec((None, tl, w), lambda b, l: (b, l, 0))
    out_shape = (jax.ShapeDtypeStruct((bsz, seq, SSD_INNER), BF16), jax.ShapeDtypeStruct((bsz, seq, SSD_INNER), F32),
                 jax.ShapeDtypeStruct((bsz, seq, SSD_GN), BF16), jax.ShapeDtypeStruct((bsz, seq, SSD_GN), BF16),
                 jax.ShapeDtypeStruct((bsz, seq, LANES), F32),
                 jax.ShapeDtypeStruct((bsz, seq // CHUNK, 2 * SSD_HEADS, CHUNK), F32),
                 jax.ShapeDtypeStruct((bsz, CONV_W - 1, SSD_CONV_DIM), F32))
    out_specs = (row(SSD_INNER), row(SSD_INNER), row(SSD_GN), row(SSD_GN), row(LANES),
                 pl.BlockSpec((None, tl // CHUNK, 2 * SSD_HEADS, CHUNK), lambda b, l: (b, l, 0, 0)),
                 pl.BlockSpec((None, CONV_W - 1, SSD_CONV_DIM), lambda b, l: (b, 0, 0)))
    consts = (p["wz"], p["wxbc"], p["wdt"], p["cw"], p["cb"], p["a"], p["dtb"], triu)
    return pl.pallas_call(
        functools.partial(_ssd_in_kernel, tl=tl),
        grid=(bsz, seq // tl),
        in_specs=[row(D_MODEL)] + [_const_spec(c.shape) for c in consts],
        out_specs=out_specs,
        out_shape=out_shape,
        scratch_shapes=_conv_scratch(SSD_CONV_DIM, tl),
        compiler_params=_params(2),
        name="ssd_in",
    )(x, *consts)


def _ssd_scan_kernel(xs_ref, bm_ref, cm_ref, slab_ref, slabt_ref, edt_ref, eac_ref, dskip_ref,
                     y_ref, sout_ref, st_ref, dtx_ref, acx_ref, *, tl):
    l = pl.program_id(1)

    @pl.when(l == 0)
    def _():
        st_ref[...] = jnp.zeros(st_ref.shape, F32)

    hi, mid, lo = _split3(slab_ref[...])
    dtx_ref[...] = jnp.dot(jnp.concatenate([hi, mid], axis=1), edt_ref[...], preferred_element_type=F32)
    acx_ref[...] = jnp.dot(jnp.concatenate([hi, mid, lo], axis=1), eac_ref[...], preferred_element_type=F32)

    ri = lax.broadcasted_iota(jnp.int32, (CHUNK, 2 * CHUNK), 0)
    li = lax.broadcasted_iota(jnp.int32, (CHUNK, 2 * CHUNK), 1)
    incl = ri >= (li & (CHUNK - 1))
    left = li < CHUNK

    def chunk_body(c, carry):
        rows = pl.ds(pl.multiple_of(c * CHUNK, CHUNK), CHUNK)
        sl = slab_ref[rows, :]
        st = slabt_ref[c]
        ac_x = acx_ref[rows, :]
        xs = xs_ref[rows, :]
        xdt = xs * dtx_ref[rows, :]
        alast_x = ac_x[CHUNK - 1:CHUNK, :]
        xdec = (xdt * jnp.exp(alast_x - ac_x)).astype(BF16)
        xdt_b = xdt.astype(BF16)
        ea = jnp.exp(ac_x)
        sdecay = jnp.exp(alast_x)
        cb2, y_off = [], []
        for g in range(SSD_GROUPS):
            gs = slice(g * SSD_GROUP_W, (g + 1) * SSD_GROUP_W)
            ns = slice(g * SSD_STATE, (g + 1) * SSD_STATE)
            cg = cm_ref[rows, ns]
            bg = bm_ref[rows, ns]
            cb2.append(_dot_nt(cg, jnp.concatenate([bg, bg], axis=0)))
            st_g = st_ref[:, gs]
            y_off.append(jnp.dot(cg, st_g.astype(BF16), preferred_element_type=F32))
            st_ref[:, gs] = st_g * sdecay[:, gs] + _dot_tn(bg, xdec[:, gs])
        for g in range(SSD_GROUPS):
            gs = slice(g * SSD_GROUP_W, (g + 1) * SSD_GROUP_W)
            diag = []
            for pp in range(SSD_HPG // 2):
                ha = g * SSD_HPG + 2 * pp
                hb = ha + 1
                acol = jnp.where(left, _col(sl, SSD_HEADS + ha), _col(sl, SSD_HEADS + hb))
                arow = jnp.concatenate([st[SSD_HEADS + ha:SSD_HEADS + ha + 1, :],
                                        st[SSD_HEADS + hb:SSD_HEADS + hb + 1, :]], axis=1)
                seg = jnp.exp(jnp.where(incl, acol - arow, NEG_BIG))
                w = _pair_blockdiag(xdt_b[:, ha * SSD_HEADDIM:(ha + 2) * SSD_HEADDIM], left)
                diag.append(jnp.dot((seg * cb2[g]).astype(BF16), w, preferred_element_type=F32))
            y_ref[rows, gs] = (jnp.concatenate(diag, axis=1) + y_off[g] * ea[:, gs]
                               + dskip_ref[:, gs] * xs[:, gs]).astype(BF16)
        return carry

    lax.fori_loop(0, tl // CHUNK, chunk_body, 0)

    @pl.when(l == pl.num_programs(1) - 1)
    def _():
        sout_ref[...] = st_ref[...].T.reshape(SSD_HEADS, SSD_HEADDIM, SSD_STATE)


def _ssd_scan(xs, bm, cm, slab, slabt, p, tl):
    bsz, seq, _ = xs.shape
    row = lambda w: pl.BlockSpec((None, tl, w), lambda b, l: (b, l, 0))
    consts = (jnp.tile(p["edt"], (2, 1)), jnp.tile(p["eac"], (3, 1)), p["dskip"])
    return pl.pallas_call(
        functools.partial(_ssd_scan_kernel, tl=tl),
        grid=(bsz, seq // tl),
        in_specs=[row(SSD_INNER), row(SSD_GN), row(SSD_GN), row(LANES),
                  pl.BlockSpec((None, tl // CHUNK, 2 * SSD_HEADS, CHUNK), lambda b, l: (b, l, 0, 0))]
                 + [_const_spec(c.shape) for c in consts],
        out_specs=(row(SSD_INNER),
                   pl.BlockSpec((None, SSD_HEADS, SSD_HEADDIM, SSD_STATE), lambda b, l: (b, 0, 0, 0))),
        out_shape=(jax.ShapeDtypeStruct((bsz, seq, SSD_INNER), BF16),
                   jax.ShapeDtypeStruct((bsz, SSD_HEADS, SSD_HEADDIM, SSD_STATE), F32)),
        scratch_shapes=[pltpu.VMEM((SSD_STATE, SSD_INNER), F32), pltpu.VMEM((tl, SSD_INNER), F32),
                        pltpu.VMEM((tl, SSD_INNER), F32)],
        compiler_params=_params(2),
        name="ssd_scan",
    )(xs, bm, cm, slab, slabt, *consts)


def _ssd_out_kernel(y_ref, z_ref, x_ref, nw_ref, wout_ref, g_ref, b_ref, o_ref):
    t = y_ref[...].astype(F32) * _silu(z_ref[...].astype(F32))
    parts = []
    for g in range(SSD_GROUPS):
        gs = slice(g * SSD_GROUP_W, (g + 1) * SSD_GROUP_W)
        tg = t[:, gs]
        parts.append((tg * lax.rsqrt(jnp.mean(tg * tg, axis=-1, keepdims=True) + RMS_EPS) * nw_ref[:, gs]).astype(BF16))
    mix = jnp.dot(jnp.concatenate(parts, axis=1), wout_ref[...], preferred_element_type=F32)
    o_ref[...] = _layer_norm(DN_ALPHA * x_ref[...] + mix, g_ref[...], b_ref[...])


def _ssd_out(y, z, x, p, ln_g, ln_b, tl):
    bsz, seq, _ = x.shape
    row = lambda w: pl.BlockSpec((None, tl, w), lambda b, l: (b, l, 0))
    consts = (p["nw"], p["wout"], ln_g, ln_b)
    return pl.pallas_call(
        _ssd_out_kernel,
        grid=(bsz, seq // tl),
        in_specs=[row(SSD_INNER), row(SSD_INNER), row(D_MODEL)] + [_const_spec(c.shape) for c in consts],
        out_specs=row(D_MODEL),
        out_shape=jax.ShapeDtypeStruct((bsz, seq, D_MODEL), F32),
        compiler_params=_params(2),
        name="ssd_out",
    )(y, z, x, *consts)


def _step_conv(pre, c_refs, cw_ref, cb_ref):
    width = len(c_refs) + 1
    acc = cb_ref[...] + c_refs[0][...] * cw_ref[0:1, :]
    for k in range(1, width - 1):
        acc = acc + c_refs[k][...] * cw_ref[k:k + 1, :]
    return acc + pre * cw_ref[width - 1:width, :]


def _gdn_in_step_kernel(x_ref, c0_ref, c1_ref, c2_ref, wqkv_ref, wz_ref, wba_ref, cw_ref, cb_ref, alog_ref, dtb_ref,
                        q_ref, k_ref, v_ref, z_ref, slab_ref, pre_ref):
    xb = x_ref[...].astype(BF16)
    n = xb.shape[0]
    lane = lax.broadcasted_iota(jnp.int32, (n, LANES), 1)
    ba = jnp.dot(xb, wba_ref[...], preferred_element_type=F32)
    g = -jnp.exp(alog_ref[...]) * _softplus(ba + dtb_ref[...])
    slab_ref[...] = jnp.where(lane < GDN_HEADS, jax.nn.sigmoid(ba), g)
    z_ref[...] = jnp.dot(xb, wz_ref[...], preferred_element_type=F32)
    pre = jnp.dot(xb, wqkv_ref[...], preferred_element_type=F32)
    pre_ref[...] = pre
    act = _silu(_step_conv(pre, (c0_ref, c1_ref, c2_ref), cw_ref, cb_ref))
    for h in range(GDN_HEADS):
        hs = slice(h * GDN_DK, (h + 1) * GDN_DK)
        q_ref[:, hs] = _l2norm(act[:, hs]) * (GDN_DK ** -0.5)
        k_ref[:, hs] = _l2norm(act[:, GDN_QK + h * GDN_DK:GDN_QK + (h + 1) * GDN_DK])
    v_ref[...] = act[:, 2 * GDN_QK:]


def _gdn_in_step(x, cache, p):
    n = x.shape[0]
    ff = lambda w: jax.ShapeDtypeStruct((n, w), F32)
    return pl.pallas_call(
        _gdn_in_step_kernel,
        out_shape=(ff(GDN_QK), ff(GDN_QK), ff(GDN_VD), ff(GDN_VD), ff(LANES), ff(GDN_CONV_DIM)),
        compiler_params=pltpu.CompilerParams(vmem_limit_bytes=VMEM_LIMIT),
        name="gdn_in_step",
    )(x, cache[:, 0], cache[:, 1], cache[:, 2], p["wqkv"], p["wz"], p["wba"], p["cw"], p["cb"], p["alog"], p["dtb"])


def _gdn_step_kernel(q_ref, k_ref, v_ref, slab_ref, s_ref, o_ref, sout_ref, *, tb):
    zpad = jnp.zeros((LANES - GDN_HEADS, GDN_DK), F32)

    def token_body(t, carry):
        kt = jnp.concatenate([k_ref[t], zpad], axis=0).T
        qt = jnp.concatenate([q_ref[t], zpad], axis=0).T
        v = v_ref[t]
        sl = slab_ref[pl.ds(t, 1), :]
        alpha = jnp.exp(sl)
        for h in range(GDN_HEADS):
            s = s_ref[t, h]
            kc = _col(kt, h)
            a_h = alpha[:, GDN_HEADS + h:GDN_HEADS + h + 1]
            ks = jnp.sum(s * kc, axis=0, keepdims=True)
            v_new = sl[:, h:h + 1] * (v[h:h + 1, :] - a_h * ks)
            s_new = s * a_h + kc * v_new
            sout_ref[t, h] = s_new
            o_ref[t, h:h + 1, :] = jnp.sum(s_new * _col(qt, h), axis=0, keepdims=True)
        return carry

    lax.fori_loop(0, tb, token_body, 0)


def _gdn_step(q, k, v, slab, state, tb):
    n = q.shape[0]
    q3, k3, v3 = (t.reshape(n, GDN_HEADS, GDN_DK) for t in (q, k, v))
    tok = pl.BlockSpec((tb, GDN_HEADS, GDN_DK), lambda i: (i, 0, 0))
    st = pl.BlockSpec((tb, GDN_HEADS, GDN_DK, GDN_DV), lambda i: (i, 0, 0, 0))
    o, s_new = pl.pallas_call(
        functools.partial(_gdn_step_kernel, tb=tb),
        grid=(n // tb,),
        in_specs=[tok, tok, tok, pl.BlockSpec((tb, LANES), lambda i: (i, 0)), st],
        out_specs=(tok, st),
        out_shape=(jax.ShapeDtypeStruct((n, GDN_HEADS, GDN_DV), F32), jax.ShapeDtypeStruct(state.shape, F32)),
        compiler_params=_params(1),
        name="gdn_step",
    )(q3, k3, v3, slab, state)
    return o.reshape(n, GDN_VD), s_new


def _ffn_step_kernel(x_ref, c0_ref, c1_ref, wup_ref, cw_ref, cb_ref, wdown_ref, g_ref, b_ref, y_ref, pre_ref):
    x = x_ref[...]
    xb = x.astype(BF16)
    gate = jnp.dot(xb, wup_ref[:, 0:D_FF], preferred_element_type=F32)
    val = jnp.dot(xb, wup_ref[:, D_FF:2 * D_FF], preferred_element_type=F32)
    pre_ref[...] = gate
    hid = _silu(_step_conv(gate, (c0_ref, c1_ref), cw_ref, cb_ref)) * val
    y_ref[...] = _layer_norm(DN_ALPHA * x + jnp.dot(hid.astype(BF16), wdown_ref[...], preferred_element_type=F32),
                             g_ref[...], b_ref[...])


def _ffn_step(x, cache, p, ln_g, ln_b):
    n = x.shape[0]
    return pl.pallas_call(
        _ffn_step_kernel,
        out_shape=(jax.ShapeDtypeStruct((n, D_MODEL), F32), jax.ShapeDtypeStruct((n, D_FF), F32)),
        compiler_params=pltpu.CompilerParams(vmem_limit_bytes=VMEM_LIMIT),
        name="ffn_step",
    )(x, cache[:, 0], cache[:, 1], p["wup"], p["cw"], p["cb"], p["wdown"], ln_g, ln_b)


def _ssd_in_step_kernel(x_ref, c0_ref, c1_ref, c2_ref, wz_ref, wxbc_ref, wdt_ref, cw_ref, cb_ref, a_ref, dtb_ref,
                        edt_ref, eac_ref, z_ref, xs_ref, bm_ref, cm_ref, xdt_ref, dec_ref, pre_ref):
    xb = x_ref[...].astype(BF16)
    n = xb.shape[0]
    lane = lax.broadcasted_iota(jnp.int32, (n, LANES), 1)
    dt = _softplus(jnp.dot(xb, wdt_ref[...], preferred_element_type=F32) + dtb_ref[...])
    dt = jnp.where(lane < 2 * SSD_HEADS, dt, 0.0)
    slab = jnp.where(lane < SSD_HEADS, dt, dt * a_ref[...])
    z_ref[...] = jnp.dot(xb, wz_ref[...], preferred_element_type=F32)
    pre = jnp.dot(xb, wxbc_ref[...], preferred_element_type=F32)
    pre_ref[...] = pre
    act = _silu(_step_conv(pre, (c0_ref, c1_ref, c2_ref), cw_ref, cb_ref))
    xs = act[:, 0:SSD_INNER]
    xs_ref[...] = xs
    bm_ref[...] = act[:, SSD_INNER:SSD_INNER + SSD_GN]
    cm_ref[...] = act[:, SSD_INNER + SSD_GN:]
    xdt_ref[...] = xs * _dot_exact_rhs(slab, edt_ref[...])
    dec_ref[...] = jnp.exp(_dot_exact_rhs(slab, eac_ref[...]))


def _ssd_in_step(x, cache, p):
    n = x.shape[0]
    ff = lambda w: jax.ShapeDtypeStruct((n, w), F32)
    return pl.pallas_call(
        _ssd_in_step_kernel,
        out_shape=(ff(SSD_INNER), ff(SSD_INNER), ff(SSD_GN), ff(SSD_GN), ff(SSD_INNER), ff(SSD_INNER),
                   ff(SSD_CONV_DIM)),
        compiler_params=pltpu.CompilerParams(vmem_limit_bytes=VMEM_LIMIT),
        name="ssd_in_step",
    )(x, cache[:, 0], cache[:, 1], cache[:, 2], p["wz"], p["wxbc"], p["wdt"], p["cw"], p["cb"], p["a"], p["dtb"],
      p["edt"], p["eac"])


def _ssd_step_kernel(xs_ref, bm_ref, cm_ref, xdt_ref, dec_ref, dskip_ref, s_ref, y_ref, sout_ref, *, tb):
    hp = SSD_INNER
    rid = lax.broadcasted_iota(jnp.int32, (SUBLANES, hp), 0)
    gid = lax.broadcasted_iota(jnp.int32, (SUBLANES, hp), 1) // SSD_GROUP_W
    zpad_r = jnp.zeros((LANES - SUBLANES, hp), F32)

    def token_body(t, carry):
        row = pl.ds(t, 1)
        xdt = xdt_ref[row, :]
        dec = dec_ref[row, :]
        stack = jnp.where(rid == gid, jnp.broadcast_to(xdt, (SUBLANES, hp)), 0.0)
        cols = jnp.concatenate([stack, zpad_r], axis=0).T
        s = s_ref[t].reshape(hp, SSD_STATE)
        bm = bm_ref[row, :]
        bmat = jnp.concatenate([bm[:, g * SSD_STATE:(g + 1) * SSD_STATE] for g in range(SSD_GROUPS)]
                               + [jnp.zeros((LANES - SSD_GROUPS, SSD_STATE), F32)], axis=0)
        upd = jnp.dot(cols.astype(BF16), bmat.astype(BF16), preferred_element_type=F32)
        s_new = jnp.concatenate(
            [s[h * SSD_HEADDIM:(h + 1) * SSD_HEADDIM, :] * dec[:, h * SSD_HEADDIM:h * SSD_HEADDIM + 1]
             for h in range(SSD_HEADS)], axis=0) + upd
        sout_ref[t] = s_new.reshape(SSD_HEADS, SSD_HEADDIM, SSD_STATE)
        cm = cm_ref[row, :]
        cmat = jnp.concatenate([cm[:, g * SSD_STATE:(g + 1) * SSD_STATE] for g in range(SSD_GROUPS)]
                               + [jnp.zeros((SUBLANES - SSD_GROUPS, SSD_STATE), F32)], axis=0)
        yall = _dot_nt(cmat, s_new)
        y = jnp.sum(jnp.where(rid == gid, yall, 0.0), axis=0, keepdims=True)
        y_ref[row, :] = y + dskip_ref[...] * xs_ref[row, :]
        return carry

    lax.fori_loop(0, tb, token_body, 0)


def _ssd_step(xs, bm, cm, xdt, dec, state, p, tb):
    n = xs.shape[0]
    tok = lambda w: pl.BlockSpec((tb, w), lambda i: (i, 0))
    st = pl.BlockSpec((tb, SSD_HEADS, SSD_HEADDIM, SSD_STATE), lambda i: (i, 0, 0, 0))
    return pl.pallas_call(
        functools.partial(_ssd_step_kernel, tb=tb),
        grid=(n // tb,),
        in_specs=[tok(SSD_INNER), tok(SSD_GN), tok(SSD_GN), tok(SSD_INNER), tok(SSD_INNER),
                  _const_spec(p["dskip"].shape), st],
        out_specs=(tok(SSD_INNER), st),
        out_shape=(jax.ShapeDtypeStruct((n, SSD_INNER), F32), jax.ShapeDtypeStruct(state.shape, F32)),
        compiler_params=_params(1),
        name="ssd_step",
    )(xs, bm, cm, xdt, dec, p["dskip"], state)


def _row(v, width=None, offset=0):
    v = v.astype(F32).reshape(1, -1)
    if width is None:
        return v
    return jnp.pad(v, ((0, 0), (offset, width - offset - v.shape[1])))


def _prep_gdn(w_in, conv_w, conv_b, a_log, dt_bias, norm_w, w_out):
    ba = w_in[:, GDN_CONV_DIM + GDN_VD:]
    return {
        "wqkv": w_in[:, :GDN_CONV_DIM].astype(BF16),
        "wz": w_in[:, GDN_CONV_DIM:GDN_CONV_DIM + GDN_VD].astype(BF16),
        "wba": jnp.pad(ba, ((0, 0), (0, LANES - ba.shape[1]))).astype(BF16),
        "cw": conv_w.astype(F32),
        "cb": _row(conv_b),
        "alog": _row(a_log, LANES, GDN_HEADS),
        "dtb": _row(dt_bias, LANES, GDN_HEADS),
        "nw": _row(norm_w),
        "wout": w_out.astype(BF16),
    }


def _prep_ssd(w_in, conv_w, conv_b, a_log, dt_bias, d_skip, norm_w, w_out):
    wdt = w_in[:, SSD_INNER + SSD_CONV_DIM:]
    wdt2 = jnp.concatenate([wdt, wdt], axis=1)
    head_of_lane = jnp.arange(SSD_INNER) // SSD_HEADDIM
    sel = jnp.arange(LANES)[:, None]
    return {
        "wz": w_in[:, :SSD_INNER].astype(BF16),
        "wxbc": w_in[:, SSD_INNER:SSD_INNER + SSD_CONV_DIM].astype(BF16),
        "wdt": jnp.pad(wdt2, ((0, 0), (0, LANES - 2 * SSD_HEADS))).astype(BF16),
        "cw": conv_w.astype(F32),
        "cb": _row(conv_b),
        "a": _row(-jnp.exp(a_log.astype(F32)), LANES, SSD_HEADS),
        "dtb": _row(jnp.concatenate([dt_bias, dt_bias]), LANES, 0),
        "edt": (sel == head_of_lane[None, :]).astype(BF16),
        "eac": (sel == head_of_lane[None, :] + SSD_HEADS).astype(BF16),
        "dskip": _row(jnp.repeat(d_skip, SSD_HEADDIM)),
        "nw": _row(norm_w),
        "wout": w_out.astype(BF16),
    }


def _prep_ffn(w_up, conv_w, conv_b, w_down):
    return {"wup": w_up.astype(BF16), "cw": conv_w.astype(F32), "cb": _row(conv_b), "wdown": w_down.astype(BF16)}


def _prompt_trunk(x, gdn, ssd, ffn, ln, tl):
    q, k, v, z, slab, slabt, gdn_cache = _gdn_in(x, gdn, tl)
    x, gdn_state = _gdn_scan(q, k, v, slab, slabt, z, x, gdn, ln[0][0], ln[0][1], min(GDN_SCAN_TILE, x.shape[1]))
    x, ffn_cache0 = _ffn(x, ffn[0], ln[0][2], ln[0][3], tl)
    z, xs, bm, cm, slab, slabt, ssd_cache = _ssd_in(x, ssd, tl)
    y, ssd_state = _ssd_scan(xs, bm, cm, slab, slabt, ssd, tl)
    x = _ssd_out(y, z, x, ssd, ln[1][0], ln[1][1], tl)
    x, ffn_cache1 = _ffn(x, ffn[1], ln[1][2], ln[1][3], tl)
    return x, gdn_cache, gdn_state, ssd_cache, ssd_state, jnp.stack([ffn_cache0, ffn_cache1])


def _sample_trunk(x, gdn_cache, gdn_state, ssd_cache, ssd_state, ffn_cache, gdn, ssd, ffn, ln, tb):
    n = x.shape[0]
    q, k, v, z, slab, pre = _gdn_in_step(x, gdn_cache, gdn)
    gdn_cache_new = jnp.concatenate([gdn_cache[:, 1:], pre[:, None]], axis=1)
    o, gdn_state_new = _gdn_step(q, k, v, slab, gdn_state, tb)
    x = _gdn_out(o[None], z[None], x[None], gdn, ln[0][0], ln[0][1], n)[0]
    x, pre = _ffn_step(x, ffn_cache[0], ffn[0], ln[0][2], ln[0][3])
    ffn_cache0 = jnp.concatenate([ffn_cache[0][:, 1:], pre[:, None]], axis=1)
    z, xs, bm, cm, xdt, dec, pre = _ssd_in_step(x, ssd_cache, ssd)
    ssd_cache_new = jnp.concatenate([ssd_cache[:, 1:], pre[:, None]], axis=1)
    y, ssd_state_new = _ssd_step(xs, bm, cm, xdt, dec, ssd_state, ssd, tb)
    x = _ssd_out(y[None], z[None], x[None], ssd, ln[1][0], ln[1][1], n)[0]
    x, pre = _ffn_step(x, ffn_cache[1], ffn[1], ln[1][2], ln[1][3])
    ffn_cache1 = jnp.concatenate([ffn_cache[1][:, 1:], pre[:, None]], axis=1)
    return x, gdn_cache_new, gdn_state_new, ssd_cache_new, ssd_state_new, jnp.stack([ffn_cache0, ffn_cache1])


PROMPT_TILE = 512
GDN_SCAN_TILE = 1024
GDN_OUT_ROWS = 256
SAMPLE_TOKENS = 8


def kernel(x_prompt, x_sample, cache_gdn_conv, state_gdn, cache_ssd_conv, state_ssd, cache_ffn_conv, gdn_w_in, gdn_conv_w, gdn_conv_b, gdn_a_log, gdn_dt_bias, gdn_norm_w, gdn_w_out, ssd_w_in, ssd_conv_w, ssd_conv_b, ssd_a_log, ssd_dt_bias, ssd_d, ssd_norm_w, ssd_w_out, ffn_w_up, ffn_conv_w, ffn_conv_b, ffn_w_down, ln1_g, ln1_b, ln2_g, ln2_b):
    gdn = _prep_gdn(gdn_w_in[0], gdn_conv_w[0], gdn_conv_b[0], gdn_a_log[0], gdn_dt_bias[0], gdn_norm_w[0],
                    gdn_w_out[0])
    ssd = _prep_ssd(ssd_w_in[0], ssd_conv_w[0], ssd_conv_b[0], ssd_a_log[0], ssd_dt_bias[0], ssd_d[0],
                    ssd_norm_w[0], ssd_w_out[0])
    ffn = [_prep_ffn(ffn_w_up[i], ffn_conv_w[i], ffn_conv_b[i], ffn_w_down[i]) for i in range(DEPTH)]
    ln = [(_row(ln1_g[i]), _row(ln1_b[i]), _row(ln2_g[i]), _row(ln2_b[i])) for i in range(DEPTH)]

    tl = min(PROMPT_TILE, x_prompt.shape[1])
    y_p, gcp, gsp, scp, ssp, fcp = _prompt_trunk(x_prompt, gdn, ssd, ffn, ln, tl)
    y_s, gcs, gss, scs, sss, fcs = _sample_trunk(
        x_sample[:, 0], cache_gdn_conv[0], state_gdn[0], cache_ssd_conv[0], state_ssd[0], cache_ffn_conv,
        gdn, ssd, ffn, ln, min(SAMPLE_TOKENS, x_sample.shape[0]))
    return (y_p, y_s[:, None], gcp[None], gcs[None], gsp[None], gss[None], scp[None], scs[None],
            ssp[None], sss[None], fcp, fcs)
```
